```python
import math
import jax
import jax.numpy as jnp
from jax import lax
import numpy as np

D_MODEL = 1024
BATCH = 1
SEQ = 16384
DEPTH = 1
DEC_BATCH = 128
DEC_SEQ = 8
PAST_LEN = 8192
PAGE_SIZE = 128

HG_HEADS = 8
HG_DK = 128
HG_DV = D_MODEL // HG_HEADS
HG_CHUNK = 64
MB_HEADS = 8
MB_KV_HEADS = 2
MB_HD = 128
MB_GROUP = MB_HEADS // MB_KV_HEADS
MB_BLOCK = 256
MB_TOPK = 3
MB_QCHUNK = 64
RB_BUCKETS = 32
RB_MAX_DIST = 128
MEM_LEN = 256
MX_HEADS = 4
MX_HD = D_MODEL // MX_HEADS
D_FF = -((-8 * D_MODEL) // (3 * 256)) * 256
EPS = 1e-6

IN_SIZES = (HG_HEADS * HG_DK, HG_HEADS * HG_DK, HG_HEADS * HG_DV, HG_HEADS * HG_DV,
            MB_HEADS * MB_HD, MB_KV_HEADS * MB_HD, MB_KV_HEADS * MB_HD, 2 * D_MODEL)
IN_SPLITS = tuple(int(s) for s in np.cumsum(IN_SIZES)[:-1])
IN_COLS = int(sum(IN_SIZES))

kernel_name = "hgrn2_moba_gated_hybrid_step"


def rmsnorm(x, g):
    x32 = x.astype(jnp.float32)
    y = x32 * lax.rsqrt(jnp.mean(x32 * x32, axis=-1, keepdims=True) + EPS)
    return (y * g.astype(jnp.float32)).astype(x.dtype)


def rel_bucket(dist):
    n = jnp.maximum(dist, 0)
    max_exact = RB_BUCKETS // 2
    large = max_exact + (jnp.log(jnp.maximum(n, 1).astype(jnp.float32) / max_exact)
                         / math.log(RB_MAX_DIST / max_exact) * (RB_BUCKETS - max_exact)).astype(jnp.int32)
    large = jnp.minimum(large, RB_BUCKETS - 1)
    return jnp.where(n < max_exact, n, large)


def chunk_gla(q, k, v, logf, S0, chunk):
    B, L, H, DK = q.shape
    DV = v.shape[-1]
    n = L // chunk

    def blocks(a):
        return a.reshape(B, n, chunk, H, a.shape[-1]).swapaxes(0, 1)

    causal = jnp.tril(jnp.ones((chunk, chunk), bool))[None, :, :, None, None]

    def step(S, inp):
        qc, kc, vc, gc = inp
        cum = jnp.cumsum(gc, axis=1)
        decay = jnp.exp(jnp.where(causal, cum[:, :, None] - cum[:, None, :], -jnp.inf))
        att = jnp.einsum('btshk,bshk->bhts', qc[:, :, None] * decay, kc)
        o = (jnp.einsum('bthk,bhkv->bthv', qc * jnp.exp(cum), S)
             + jnp.einsum('bhts,bshv->bthv', att, vc))
        last = cum[:, -1]
        S = (jnp.exp(last)[..., None] * S
             + jnp.einsum('bshk,bshv->bhkv', kc * jnp.exp(last[:, None] - cum), vc))
        return S, o

    S, o = lax.scan(step, S0, (blocks(q), blocks(k), blocks(v), blocks(logf)))
    return o.swapaxes(0, 1).reshape(B, L, H, DV), S


def hgrn2_branch(hq, hf, hi, hg, lb, norm_g, S0, chunk):
    B, L, _ = hq.shape
    f = lb + (1.0 - lb) * jax.nn.sigmoid(hf.astype(jnp.float32))

    def sh(a):
        return a.reshape(B, L, HG_HEADS, -1)

    o, S = chunk_gla(sh(hq.astype(jnp.float32)), sh(1.0 - f), sh(hi.astype(jnp.float32)),
                     sh(jnp.log(f)), S0.astype(jnp.float32), chunk)
    o = (o * lax.rsqrt(jnp.mean(o * o, axis=-1, keepdims=True) + EPS) * norm_g.astype(jnp.float32)
         * jax.nn.silu(sh(hg.astype(jnp.float32))))
    return o.reshape(B, L, -1), S


def moba_prompt(q, k, v, rel_bias):
    B, L = q.shape[:2]
    nb_full = L // MB_BLOCK
    nb_all = -(-L // MB_BLOCK)
    pad = nb_all * MB_BLOCK - L
    padw = ((0, 0), (0, pad), (0, 0), (0, 0))
    kb = jnp.pad(k.astype(jnp.float32), padw).reshape(B, nb_all, MB_BLOCK, MB_KV_HEADS, MB_HD)
    vb = jnp.pad(v.astype(jnp.float32), padw).reshape(B, nb_all, MB_BLOCK, MB_KV_HEADS, MB_HD)
    ksel = min(MB_TOPK, nb_full)
    kvh = jnp.arange(MB_HEADS) // MB_GROUP
    hidx = jnp.arange(MB_HEADS)
    bidx = jnp.arange(B)[:, None, None, None]
    scale = MB_HD ** -0.5
    means = kb[:, :nb_full].mean(axis=2)[:, :, kvh]
    q32 = q.astype(jnp.float32)

    def chunk_out(c):
        q0 = c * MB_QCHUNK
        qc = lax.dynamic_slice_in_dim(q32, q0, MB_QCHUNK, axis=1)
        tpos = q0 + jnp.arange(MB_QCHUNK)
        cur = q0 // MB_BLOCK
        ko = lax.dynamic_index_in_dim(kb, cur, axis=1, keepdims=False)[:, :, kvh]
        vo = lax.dynamic_index_in_dim(vb, cur, axis=1, keepdims=False)[:, :, kvh]
        d_own = tpos[:, None] - (cur * MB_BLOCK + jnp.arange(MB_BLOCK))[None, :]
        l_own = (jnp.einsum('bqhd,bkhd->bqhk', qc, ko) * scale
                 + rel_bias[rel_bucket(d_own)].transpose(0, 2, 1)[None].astype(jnp.float32))
        l_own = jnp.where((d_own >= 0)[None, :, None, :], l_own, -jnp.inf)
        if ksel == 0:
            p = jax.nn.softmax(l_own, axis=-1)
            return jnp.einsum('bqhk,bkhd->bqhd', p, vo)
        gate = jnp.einsum('bqhd,bnhd->bqhn', qc, means)
        gate = jnp.where(jnp.arange(nb_full) < cur, gate, -jnp.inf)
        _, sel = lax.top_k(gate, ksel)
        kg = kb[bidx, sel, :, kvh[None, None, :, None]]
        vg = vb[bidx, sel, :, kvh[None, None, :, None]]
        d_sel = tpos[None, :, None, None, None] - (sel[..., None] * MB_BLOCK + jnp.arange(MB_BLOCK))
        l_sel = (jnp.einsum('bqhd,bqhnkd->bqhnk', qc, kg) * scale
                 + rel_bias[rel_bucket(d_sel), hidx[None, None, :, None, None]].astype(jnp.float32))
        l_sel = jnp.where((sel < cur)[..., None], l_sel, -jnp.inf)
        ns = ksel * MB_BLOCK
        p = jax.nn.softmax(jnp.concatenate([l_sel.reshape(B, MB_QCHUNK, MB_HEADS, ns), l_own], -1), axis=-1)
        return (jnp.einsum('bqhm,bqhmd->bqhd', p[..., :ns], vg.reshape(B, MB_QCHUNK, MB_HEADS, ns, MB_HD))
                + jnp.einsum('bqhk,bkhd->bqhd', p[..., ns:], vo))

    o = lax.map(chunk_out, jnp.arange(L // MB_QCHUNK))
    return o.swapaxes(0, 1).reshape(B, L, MB_HEADS, MB_HD)


def moba_sample(q, k_new, v_new, cache_k, cache_v, layer, page_table, rel_bias):
    T = q.shape[1]
    cur = PAST_LEN // MB_BLOCK
    own0 = cur * MB_BLOCK
    ppb = MB_BLOCK // PAGE_SIZE
    ksel = min(MB_TOPK, cur)
    R = PAST_LEN - own0 + T
    tpos = PAST_LEN + jnp.arange(T)
    kvh = jnp.arange(MB_HEADS) // MB_GROUP
    hidx = jnp.arange(MB_HEADS)
    scale = MB_HD ** -0.5
    d_own = tpos[:, None] - (own0 + jnp.arange(R))[None, :]
    bias_own = rel_bias[rel_bucket(d_own)].transpose(0, 2, 1).astype(jnp.float32)
    own_mask = (d_own >= 0)[:, None, :]

    def one(args):
        qb, kn, vn, pt = args
        qb = qb.astype(jnp.float32)
        own_pages = pt[own0 // PAGE_SIZE: PAST_LEN // PAGE_SIZE]
        k_own = jnp.concatenate([cache_k[layer, own_pages].reshape(-1, MB_KV_HEADS, MB_HD).astype(jnp.float32),
                                 kn.astype(jnp.float32)], 0)[:, kvh]
        v_own = jnp.concatenate([cache_v[layer, own_pages].reshape(-1, MB_KV_HEADS, MB_HD).astype(jnp.float32),
                                 vn.astype(jnp.float32)], 0)[:, kvh]
        l_own = jnp.where(own_mask, jnp.einsum('thd,rhd->thr', qb, k_own) * scale + bias_own, -jnp.inf)
        if ksel == 0:
            p = jax.nn.softmax(l_own, axis=-1)
            return jnp.einsum('thr,rhd->thd', p, v_own)
        k_past = cache_k[layer, pt[:cur * ppb]].astype(jnp.float32).reshape(cur, MB_BLOCK, MB_KV_HEADS, MB_HD)
        means = k_past.mean(axis=1)[:, kvh]
        gate = jnp.einsum('thd,nhd->thn', qb, means)
        _, sel = lax.top_k(gate, ksel)
        kg = k_past[sel, :, kvh[None, :, None]]
        phys = pt[sel[..., None] * ppb + jnp.arange(ppb)]
        ns = ksel * MB_BLOCK
        vg = cache_v[layer, phys, :, kvh[None, :, None, None]].astype(jnp.float32).reshape(T, MB_HEADS, ns, MB_HD)
        d_sel = tpos[:, None, None, None] - (sel[..., None] * MB_BLOCK + jnp.arange(MB_BLOCK))
        l_sel = (jnp.einsum('thd,thnkd->thnk', qb, kg) * scale
                 + rel_bias[rel_bucket(d_sel), hidx[None, :, None, None]].astype(jnp.float32))
        p = jax.nn.softmax(jnp.concatenate([l_sel.reshape(T, MB_HEADS, ns), l_own], -1), axis=-1)
        return (jnp.einsum('thm,thmd->thd', p[..., :ns], vg)
                + jnp.einsum('thr,rhd->thd', p[..., ns:], v_own))

    return lax.map(one, (q, k_new, v_new, page_table))


def mixer_block(x, S0, chunk, moba_fn, norm_g, w_in_l, lb, hg_norm_g_l, w_pa, w_pb, w_o):
    B, L, _ = x.shape
    xn = rmsnorm(x, norm_g)
    z = xn @ w_in_l
    hq, hf, hi, hg, mq, mk, mv, gates = jnp.split(z, IN_SPLITS, axis=-1)
    o_a, S = hgrn2_branch(hq, hf, hi, hg, lb, hg_norm_g_l, S0, chunk)
    kh = mk.reshape(B, L, MB_KV_HEADS, MB_HD)
    vh = mv.reshape(B, L, MB_KV_HEADS, MB_HD)
    o_b = moba_fn(mq.reshape(B, L, MB_HEADS, MB_HD), kh, vh).reshape(B, L, -1)
    ga, gb = jnp.split(jax.nn.sigmoid(gates.astype(jnp.float32)).astype(x.dtype), 2, axis=-1)
    merged = ga * (o_a.astype(x.dtype) @ w_pa) + gb * (o_b.astype(x.dtype) @ w_pb)
    return merged @ w_o, S, kh, vh


def cross_attn(x, mk, mv, norm_g, w_q, w_o):
    B, L, _ = x.shape
    q = (rmsnorm(x, norm_g) @ w_q).reshape(B, L, MX_HEADS, MX_HD).astype(jnp.float32)
    s = jnp.einsum('blhd,bmhd->bhlm', q, mk.astype(jnp.float32)) * (MX_HD ** -0.5)
    p = jax.nn.softmax(s, axis=-1)
    o = jnp.einsum('bhlm,bmhd->blhd', p, mv.astype(jnp.float32)).reshape(B, L, -1).astype(x.dtype)
    return o @ w_o


def ffn_block(x, norm_g, w_gate, w_up, w_down):
    xn = rmsnorm(x, norm_g)
    return (jax.nn.silu(xn @ w_gate) * (xn @ w_up)) @ w_down


def setup_inputs(seed: int = 0) -> dict:
    key = jax.random.key(seed)
    ks = jax.random.split(key, 32)
    f32 = jnp.float32
    n_pages = PAST_LEN // PAGE_SIZE
    n_used = DEC_BATCH * n_pages
    n_phys = n_used + max(1, n_used // 4)

    def nrm(k, shape, s):
        return jax.random.normal(k, shape, f32) * s

    def gain(k, shape):
        return 1.0 + 0.02 * jax.random.normal(k, shape, f32)

    D = D_MODEL
    return {
        "x_prompt": nrm(ks[0], (BATCH, SEQ, D), 1.0),
        "x_sample": nrm(ks[1], (DEC_BATCH, DEC_SEQ, D), 1.0),
        "cache_k": nrm(ks[2], (DEPTH, n_phys, PAGE_SIZE, MB_KV_HEADS, MB_HD), 1.0),
        "cache_v": nrm(ks[3], (DEPTH, n_phys, PAGE_SIZE, MB_KV_HEADS, MB_HD), 1.0),
        "cache_mem_k": nrm(ks[4], (DEPTH, DEC_BATCH, MEM_LEN, MX_HEADS, MX_HD), 1.0),
        "cache_mem_v": nrm(ks[5], (DEPTH, DEC_BATCH, MEM_LEN, MX_HEADS, MX_HD), 1.0),
        "state_hgrn": nrm(ks[6], (DEPTH, DEC_BATCH, HG_HEADS, HG_DK, HG_DV), 0.3),
        "page_table": jax.random.permutation(ks[7], n_phys)[:n_used].reshape(DEC_BATCH, n_pages).astype(jnp.int32),
        "mem_prompt": nrm(ks[8], (BATCH, MEM_LEN, D), 1.0),
        "norm_mix_g": gain(ks[9], (DEPTH, D)),
        "w_in": nrm(ks[10], (DEPTH, D, IN_COLS), D ** -0.5),
        "hg_lb_logits": nrm(ks[11], (DEPTH + 1, HG_HEADS * HG_DK), 0.5),
        "hg_norm_g": gain(ks[12], (DEPTH, HG_DV)),
        "w_proj_a": nrm(ks[13], (DEPTH, HG_HEADS * HG_DV, D), (HG_HEADS * HG_DV) ** -0.5),
        "w_proj_b": nrm(ks[14], (DEPTH, MB_HEADS * MB_HD, D), (MB_HEADS * MB_HD) ** -0.5),
        "w_out": nrm(ks[15], (DEPTH, D, D), D ** -0.5),
        "rel_bias": nrm(ks[16], (RB_BUCKETS, MB_HEADS), 0.5),
        "norm_x_g": gain(ks[17], (DEPTH, D)),
        "w_xq": nrm(ks[18], (DEPTH, D, D), D ** -0.5),
        "w_mk": nrm(ks[19], (DEPTH, D, D), D ** -0.5),
        "w_mv": nrm(ks[20], (DEPTH, D, D), D ** -0.5),
        "w_xo": nrm(ks[21], (DEPTH, D, D), D ** -0.5),
        "norm_ffn_g": gain(ks[22], (DEPTH, D)),
        "w_gate": nrm(ks[23], (DEPTH, D, D_FF), D ** -0.5),
        "w_up": nrm(ks[24], (DEPTH, D, D_FF), D ** -0.5),
        "w_down": nrm(ks[25], (DEPTH, D_FF, D), D_FF ** -0.5),
        "final_norm_g": gain(ks[26], (D,)),
    }


def reference(x_prompt, x_sample, cache_k, cache_v, cache_mem_k, cache_mem_v, state_hgrn, page_table,
              mem_prompt, norm_mix_g, w_in, hg_lb_logits, hg_norm_g, w_proj_a, w_proj_b, w_out, rel_bias,
              norm_x_g, w_xq, w_mk, w_mv, w_xo, norm_ffn_g, w_gate, w_up, w_down, final_norm_g):
    f32 = jnp.float32
    lb_all = jnp.cumsum(jax.nn.softmax(hg_lb_logits.astype(f32), axis=0), axis=0)
    hp, hs = x_prompt, x_sample
    Bp, Lp = x_prompt.shape[0], x_prompt.shape[1]
    sp_l, kp_l, vp_l, mkp_l, mvp_l, ss_l, ks_l, vs_l = [], [], [], [], [], [], [], []
    for l in range(DEPTH):
        lb = lb_all[l]
        s0 = jnp.zeros((Bp, HG_HEADS, HG_DK, HG_DV), f32)
        dp, Sp, kp, vp = mixer_block(hp, s0, math.gcd(Lp, HG_CHUNK),
                                     lambda q, k, v: moba_prompt(q, k, v, rel_bias),
                                     norm_mix_g[l], w_in[l], lb, hg_norm_g[l], w_proj_a[l], w_proj_b[l], w_out[l])
        hp = hp + dp
        mkp = (mem_prompt @ w_mk[l]).reshape(Bp, -1, MX_HEADS, MX_HD)
        mvp = (mem_prompt @ w_mv[l]).reshape(Bp, -1, MX_HEADS, MX_HD)
        hp = hp + cross_attn(hp, mkp, mvp, norm_x_g[l], w_xq[l], w_xo[l])
        hp = hp + ffn_block(hp, norm_ffn_g[l], w_gate[l], w_up[l], w_down[l])
        ds, Ss, ksn, vsn = mixer_block(hs, state_hgrn[l], hs.shape[1],
                                       lambda q, k, v, l=l: moba_sample(q, k, v, cache_k, cache_v, l, page_table, rel_bias),
                                       norm_mix_g[l], w_in[l], lb, hg_norm_g[l], w_proj_a[l], w_proj_b[l], w_out[l])
        hs = hs + ds
        hs = hs + cross_attn(hs, cache_mem_k[l], cache_mem_v[l], norm_x_g[l], w_xq[l], w_xo[l])
        hs = hs + ffn_block(hs, norm_ffn_g[l], w_gate[l], w_up[l], w_down[l])
        sp_l.append(Sp.astype(state_hgrn.dtype))
        kp_l.append(kp)
        vp_l.append(vp)
        mkp_l.append(mkp)
        mvp_l.append(mvp)
        ss_l.append(Ss.astype(state_hgrn.dtype))
        ks_l.append(ksn)
        vs_l.append(vsn)
    y_prompt = rmsnorm(hp, final_norm_g)
    y_sample = rmsnorm(hs, final_norm_g)
    return (y_prompt, y_sample, jnp.stack(sp_l), jnp.stack(kp_l), jnp.stack(vp_l), jnp.stack(mkp_l),
            jnp.stack(mvp_l), jnp.stack(ss_l), jnp.stack(ks_l), jnp.stack(vs_l))
```

```python
import functools
import math

import jax
import jax.numpy as jnp
import numpy as np
from jax import lax
from jax.experimental import pallas as pl
from jax.experimental.pallas import tpu as pltpu

F32 = jnp.float32
BF16 = jnp.bfloat16

EPS = 1e-6
HG_HEADS = 8
HG_DK = 128
HG_DV = 128
HG_SUB = 16
MB_HEADS = 8
MB_KV_HEADS = 2
MB_GROUP = MB_HEADS // MB_KV_HEADS
MB_HD = 128
MB_BLOCK = 256
MB_TOPK = 3
RB_BUCKETS = 32
RB_MAX_DIST = 128
MX_HEADS = 4
PAGE_SIZE = 128
NEG = -1e30
LANES = 128
VMEM_LIMIT = 56 * 1024 * 1024


def _params(*sem):
    return pltpu.CompilerParams(dimension_semantics=sem, vmem_limit_bytes=VMEM_LIMIT)


def _sigmoid(x):
    return 1.0 / (1.0 + jnp.exp(-x))


def _silu(x):
    return x * _sigmoid(x)


def _rms(x, g):
    return x * lax.rsqrt(jnp.mean(x * x, axis=-1, keepdims=True) + EPS) * g


def _matmul_kernel(*refs, norm, residual):
    refs = list(refs)
    x_ref = refs.pop(0)
    g_ref = refs.pop(0) if norm else None
    w_ref = refs.pop(0)
    r_ref = refs.pop(0) if residual else None
    o_ref, xs_ref = refs

    @pl.when(pl.program_id(1) == 0)
    def _():
        x = x_ref[...]
        if norm:
            x = _rms(x, g_ref[...])
        xs_ref[...] = x.astype(BF16)

    acc = jnp.dot(xs_ref[...], w_ref[...], preferred_element_type=F32)
    if residual:
        acc = acc + r_ref[...]
    o_ref[...] = acc


def _matmul(x, w, g=None, res=None, tm=512, tn=512):
    M, K = x.shape
    N = w.shape[1]
    tm = min(tm, M)
    tn = min(tn, N)
    assert M % tm == 0 and N % tn == 0
    args = [x]
    specs = [pl.BlockSpec((tm, K), lambda i, j: (i, 0))]
    if g is not None:
        args.append(g.reshape(1, K))
        specs.append(pl.BlockSpec((1, K), lambda i, j: (0, 0)))
    args.append(w)
    specs.append(pl.BlockSpec((K, tn), lambda i, j: (0, j)))
    if res is not None:
        args.append(res)
        specs.append(pl.BlockSpec((tm, tn), lambda i, j: (i, j)))
    return pl.pallas_call(
        functools.partial(_matmul_kernel, norm=g is not None, residual=res is not None),
        grid=(M // tm, N // tn),
        in_specs=specs,
        out_specs=pl.BlockSpec((tm, tn), lambda i, j: (i, j)),
        out_shape=jax.ShapeDtypeStruct((M, N), F32),
        scratch_shapes=[pltpu.VMEM((tm, K), BF16)],
        compiler_params=_params("parallel", "arbitrary"),
        name="matmul",
    )(*args)


def _subblock_cumsum(g, sub):
    row = lax.broadcasted_iota(jnp.int32, g.shape, 0) % sub
    s = 1
    while s < sub:
        g = g + jnp.where(row >= s, pltpu.roll(g, s, 0), 0.0)
        s *= 2
    return g


def _gla_kernel(q_ref, f_ref, i_ref, g_ref, lbl_ref, ng_ref, s0_ref, o_ref, so_ref, st_ref, *, C, sub):
    c = pl.program_id(1)
    nsub = C // sub
    mid = sub // 2 - 1

    @pl.when(c == 0)
    def _():
        for h in range(HG_HEADS):
            st_ref[h] = s0_ref[0, h].T

    lbl = lbl_ref[...]
    e = jnp.exp(lbl - jnp.max(lbl, axis=0, keepdims=True))
    lb_all = e[0:1] / jnp.sum(e, axis=0, keepdims=True)
    ng = ng_ref[...]

    row = lax.broadcasted_iota(jnp.int32, (C, C), 0)
    col = lax.broadcasted_iota(jnp.int32, (C, C), 1)
    diag_mask = (row // sub == col // sub) & (row >= col)

    for h in range(HG_HEADS):
        sl = slice(h * HG_DK, (h + 1) * HG_DK)
        lb = lb_all[:, sl]
        q3 = q_ref[:, sl].reshape(nsub, sub, HG_DK)
        f = lb + (1.0 - lb) * _sigmoid(f_ref[:, sl])
        k3 = (1.0 - f).reshape(nsub, sub, HG_DK)
        cs = _subblock_cumsum(jnp.log(f), sub).reshape(nsub, sub, HG_DK)
        v = i_ref[:, sl].astype(BF16)

        T = cs[:, sub - 1:sub, :]
        m = cs[:, mid:mid + 1, :]
        qt = q3 * jnp.exp(cs)
        kh = k3 * jnp.exp(T - cs)
        qd = (q3 * jnp.exp(cs - m)).reshape(C, HG_DK).astype(BF16)
        kd = (k3 * jnp.exp(m - cs)).reshape(C, HG_DK).astype(BF16)
        att = jnp.where(diag_mask,
                        lax.dot_general(qd, kd, (((1,), (1,)), ((), ())), preferred_element_type=F32),
                        0.0)

        PT = [jnp.zeros((1, HG_DK), F32)]
        for I in range(nsub):
            PT.append(PT[-1] + T[I])

        def scaled_k(I):
            parts = []
            for J in range(nsub):
                if J < I:
                    parts.append(kh[J] * jnp.exp(PT[I] - PT[J + 1]))
                else:
                    parts.append(jnp.zeros((sub, HG_DK), F32))
            return jnp.concatenate(parts, axis=0).astype(BF16) if nsub > 1 else parts[0].astype(BF16)

        if nsub > 1:
            rows = [jnp.zeros((sub, C), F32)]
            for I in range(1, nsub):
                rows.append(lax.dot_general(qt[I].astype(BF16), scaled_k(I), (((1,), (1,)), ((), ())),
                                            preferred_element_type=F32))
            att = att + jnp.concatenate(rows, axis=0)

        st = st_ref[h]
        qc = jnp.concatenate([qt[I] * jnp.exp(PT[I]) for I in range(nsub)], axis=0).astype(BF16)
        o = (jnp.dot(att.astype(BF16), v, preferred_element_type=F32)
             + lax.dot_general(qc, st.astype(BF16), (((1,), (1,)), ((), ())), preferred_element_type=F32))

        kc = scaled_k(nsub)
        st_new = (st * jnp.exp(PT[nsub])
                  + lax.dot_general(v, kc, (((0,), (0,)), ((), ())), preferred_element_type=F32))
        st_ref[h] = st_new

        o = o * lax.rsqrt(jnp.mean(o * o, axis=-1, keepdims=True) + EPS) * ng * _silu(g_ref[:, sl])
        o_ref[:, sl] = o

    @pl.when(c == pl.num_programs(1) - 1)
    def _():
        for h in range(HG_HEADS):
            so_ref[0, h] = st_ref[h].T


def _gla(z, lb_logits, norm_g, s0, B, L, C):
    nc = L // C
    sub = min(HG_SUB, C)
    W = HG_HEADS * HG_DK

    def zspec(j):
        return pl.BlockSpec((C, W), lambda b, c, j=j: (b * nc + c, j))

    return pl.pallas_call(
        functools.partial(_gla_kernel, C=C, sub=sub),
        grid=(B, nc),
        in_specs=[zspec(0), zspec(1), zspec(2), zspec(3),
                  pl.BlockSpec((2, W), lambda b, c: (0, 0)),
                  pl.BlockSpec((1, HG_DV), lambda b, c: (0, 0)),
                  pl.BlockSpec((1, HG_HEADS, HG_DK, HG_DV), lambda b, c: (b, 0, 0, 0))],
        out_specs=[pl.BlockSpec((C, W), lambda b, c: (b * nc + c, 0)),
                   pl.BlockSpec((1, HG_HEADS, HG_DK, HG_DV), lambda b, c: (b, 0, 0, 0))],
        out_shape=[jax.ShapeDtypeStruct((B * L, W), F32),
                   jax.ShapeDtypeStruct((B, HG_HEADS, HG_DK, HG_DV), F32)],
        scratch_shapes=[pltpu.VMEM((HG_HEADS, HG_DV, HG_DK), F32)],
        compiler_params=_params("arbitrary", "arbitrary"),
        name="gla",
    )(z, z, z, z, lb_logits, norm_g.reshape(1, HG_DV), s0)


def _bucket_upper_bounds():
    max_exact = RB_BUCKETS // 2
    d = np.arange(1, 4 * RB_MAX_DIST)
    large = max_exact + (np.log(d.astype(np.float32) / max_exact) / math.log(RB_MAX_DIST / max_exact)
                         * (RB_BUCKETS - max_exact)).astype(np.int32)
    bucket = np.where(d < max_exact, d, np.minimum(large, RB_BUCKETS - 1))
    bucket = np.concatenate([[0], bucket])
    return [int(np.max(np.nonzero(bucket == b)[0])) for b in range(RB_BUCKETS - 1)]


def _bias_kernel(rb_ref, own_ref, prev_ref):
    h = pl.program_id(0)
    i = lax.broadcasted_iota(jnp.int32, (MB_BLOCK, MB_BLOCK), 0)
    j = lax.broadcasted_iota(jnp.int32, (MB_BLOCK, MB_BLOCK), 1)
    dmax = _bucket_upper_bounds()

    def table(d):
        val = jnp.full(d.shape, rb_ref[RB_BUCKETS - 1, h], F32)
        for b in range(RB_BUCKETS - 2, -1, -1):
            val = jnp.where(d <= dmax[b], rb_ref[b, h], val)
        return val

    d = i - j
    own_ref[0] = jnp.where(d >= 0, table(d), NEG)
    prev_ref[0] = table(d + MB_BLOCK)


def _bias_tiles(rel_bias):
    shp = jax.ShapeDtypeStruct((MB_HEADS, MB_BLOCK, MB_BLOCK), F32)
    spec = pl.BlockSpec((1, MB_BLOCK, MB_BLOCK), lambda h: (h, 0, 0))
    return pl.pallas_call(
        _bias_kernel,
        grid=(MB_HEADS,),
        in_specs=[pl.BlockSpec(memory_space=pltpu.SMEM)],
        out_specs=[spec, spec],
        out_shape=[shp, shp],
        compiler_params=_params("arbitrary"),
        name="bias_tiles",
    )(rel_bias)


def _top_select(gate, lane):
    sel = jnp.zeros(gate.shape, jnp.bool_)
    lane = lane.astype(F32)
    for _ in range(MB_TOPK):
        m = jnp.max(gate, axis=1, keepdims=True)
        idx = jnp.min(jnp.where(gate == m, lane, float(2 ** 30)), axis=1, keepdims=True)
        hit = (lane == idx) & (m > -jnp.inf)
        sel = sel | hit
        gate = jnp.where(hit, -jnp.inf, gate)
    return sel


def _moba_prep_kernel(k_ref, v_ref, ka_ref, vb_ref, mean_ref):
    n = pl.program_id(1)
    k = k_ref[...]
    lane = lax.broadcasted_iota(jnp.int32, (MB_BLOCK, LANES), 1)
    onehot = jnp.where(lane == n, 1.0, 0.0).astype(BF16)
    ka_ref[0] = jnp.concatenate([k.astype(BF16), onehot], axis=1)
    vb_ref[0] = v_ref[...].astype(BF16)
    mean_ref[0, 0] = jnp.mean(k, axis=0, keepdims=True)


def _moba_prep(z, L, kcol, vcol):
    nb = L // MB_BLOCK
    return pl.pallas_call(
        _moba_prep_kernel,
        grid=(MB_KV_HEADS, nb),
        in_specs=[pl.BlockSpec((MB_BLOCK, MB_HD), lambda g, n: (n, kcol + g)),
                  pl.BlockSpec((MB_BLOCK, MB_HD), lambda g, n: (n, vcol + g))],
        out_specs=[pl.BlockSpec((1, MB_BLOCK, 2 * MB_HD), lambda g, n: (g, n, 0)),
                   pl.BlockSpec((1, MB_BLOCK, MB_HD), lambda g, n: (g, n, 0)),
                   pl.BlockSpec((1, 1, 1, MB_HD), lambda g, n: (g, n, 0, 0))],
        out_shape=[jax.ShapeDtypeStruct((MB_KV_HEADS, L, 2 * MB_HD), BF16),
                   jax.ShapeDtypeStruct((MB_KV_HEADS, L, MB_HD), BF16),
                   jax.ShapeDtypeStruct((MB_KV_HEADS, nb, 1, MB_HD), F32)],
        compiler_params=_params("arbitrary", "arbitrary"),
        name="moba_prep",
    )(z, z)


def _moba_prompt_kernel(q_ref, ka_ref, vb_ref, mean_ref, own_ref, prev_ref, o_ref,
                        qa_ref, m_ref, l_ref, acc_ref):
    i = pl.program_id(1)
    G, B = MB_GROUP, MB_BLOCK
    scale = MB_HD ** -0.5
    lane = lax.broadcasted_iota(jnp.int32, (B, LANES), 1)
    means = mean_ref[0]

    for j in range(G):
        q = q_ref[:, j * MB_HD:(j + 1) * MB_HD]
        gate = lax.dot_general(q, means, (((1,), (1,)), ((), ())), precision=lax.Precision.HIGHEST,
                               preferred_element_type=F32)
        sel = _top_select(jnp.where(lane < i, gate, -jnp.inf), lane)
        b31 = prev_ref[j, :, 0:1]
        mb = jnp.where(sel, jnp.where(lane == i - 1, 0.0, b31), NEG)
        mb = jnp.where(lane == i, 0.0, mb)
        qa_ref[j] = jnp.concatenate([(q * scale).astype(BF16), mb.astype(BF16)], axis=1)

    def attend(n, bias, first):
        kn = ka_ref[0, pl.ds(pl.multiple_of(n * B, B), B), :]
        vn = vb_ref[0, pl.ds(pl.multiple_of(n * B, B), B), :]
        for j in range(G):
            s = lax.dot_general(qa_ref[j], kn, (((1,), (1,)), ((), ())), preferred_element_type=F32)
            if bias is not None:
                s = s + bias[j]
            mx = jnp.max(s, axis=1, keepdims=True)
            if first:
                m_new = mx
                p = jnp.exp(s - m_new)
                l_ref[j] = jnp.sum(p, axis=1, keepdims=True)
                acc_ref[j] = jnp.dot(p.astype(BF16), vn, preferred_element_type=F32)
            else:
                m_prev = m_ref[j]
                m_new = jnp.maximum(m_prev, mx)
                alpha = jnp.exp(m_prev - m_new)
                p = jnp.exp(s - m_new)
                l_ref[j] = alpha * l_ref[j] + jnp.sum(p, axis=1, keepdims=True)
                acc_ref[j] = alpha * acc_ref[j] + jnp.dot(p.astype(BF16), vn, preferred_element_type=F32)
            m_ref[j] = m_new

    attend(i, own_ref, True)

    @pl.when(i >= 1)
    def _():
        attend(i - 1, prev_ref, False)

    def body(n, carry):
        attend(n, None, False)
        return carry

    lax.fori_loop(0, jnp.maximum(i - 1, 0), body, 0)

    for j in range(G):
        o_ref[:, j * MB_HD:(j + 1) * MB_HD] = acc_ref[j] / l_ref[j]


def _moba_prompt(z, L, qcol, ka, vb, means, bias_own, bias_prev):
    nb = L // MB_BLOCK
    G, B = MB_GROUP, MB_BLOCK
    means_p = jnp.pad(means.reshape(MB_KV_HEADS, nb, MB_HD), ((0, 0), (0, LANES - nb), (0, 0)))
    return pl.pallas_call(
        _moba_prompt_kernel,
        grid=(MB_KV_HEADS, nb),
        in_specs=[pl.BlockSpec((B, G * MB_HD), lambda g, i: (i, qcol + g)),
                  pl.BlockSpec((1, L, 2 * MB_HD), lambda g, i: (g, 0, 0)),
                  pl.BlockSpec((1, L, MB_HD), lambda g, i: (g, 0, 0)),
                  pl.BlockSpec((1, LANES, MB_HD), lambda g, i: (g, 0, 0)),
                  pl.BlockSpec((G, B, B), lambda g, i: (g, 0, 0)),
                  pl.BlockSpec((G, B, B), lambda g, i: (g, 0, 0))],
        out_specs=pl.BlockSpec((B, G * MB_HD), lambda g, i: (i, g)),
        out_shape=jax.ShapeDtypeStruct((L, MB_HEADS * MB_HD), F32),
        scratch_shapes=[pltpu.VMEM((G, B, 2 * MB_HD), BF16),
                        pltpu.VMEM((G, B, 1), F32),
                        pltpu.VMEM((G, B, 1), F32),
                        pltpu.VMEM((G, B, MB_HD), F32)],
        compiler_params=_params("arbitrary", "arbitrary"),
        name="moba_prompt",
    )(z, ka, vb, means_p, bias_own, bias_prev)


def _moba_sample_kernel(pt_ref, q_ref, kn_ref, vn_ref, ck_hbm, cv_hbm, own_ref, adj_ref, b31_ref, o_ref,
                        kbuf, vbuf, lg_ref, sem, *, T, n_pages, n_seq):
    b = pl.program_id(0)
    B = MB_BLOCK
    past = n_pages * PAGE_SIZE
    nb = past // B
    R = MB_HEADS * T
    half = R // MB_KV_HEADS
    scale = MB_HD ** -0.5

    def page_copies(seq, slot, j):
        dst = pl.ds(pl.multiple_of(j * PAGE_SIZE, PAGE_SIZE), PAGE_SIZE)
        page = pt_ref[seq, j]
        return (pltpu.make_async_copy(ck_hbm.at[page], kbuf.at[slot, dst, :], sem.at[0, slot]),
                pltpu.make_async_copy(cv_hbm.at[page], vbuf.at[slot, dst, :], sem.at[1, slot]))

    def start_seq(seq, slot):
        def body(j, c):
            ck, cv = page_copies(seq, slot, j)
            ck.start()
            cv.start()
            return c
        lax.fori_loop(0, n_pages, body, 0)

    def wait_seq(seq, slot):
        def body(j, c):
            ck, cv = page_copies(seq, slot, j)
            ck.wait()
            cv.wait()
            return c
        lax.fori_loop(0, n_pages, body, 0)

    slot = b % 2

    @pl.when(b == 0)
    def _():
        start_seq(0, 0)

    @pl.when(b + 1 < n_seq)
    def _():
        start_seq(b + 1, 1 - slot)

    wait_seq(b, slot)

    q = q_ref[...]
    Q = jnp.concatenate([q[:, h * MB_HD:(h + 1) * MB_HD] for h in range(MB_HEADS)], axis=0)
    zero = jnp.zeros((half, MB_HD), F32)
    qpad = jnp.concatenate([jnp.concatenate([Q[:half], zero], axis=1),
                            jnp.concatenate([zero, Q[half:]], axis=1)], axis=0)
    qs = (qpad * scale).astype(BF16)

    mean_rows = [jnp.mean(kbuf[slot, n * B:(n + 1) * B, :], axis=0, keepdims=True) for n in range(nb)]
    means = jnp.concatenate(mean_rows + [jnp.zeros((LANES - nb, 2 * MB_HD), F32)], axis=0)
    gate = lax.dot_general(qpad, means, (((1,), (1,)), ((), ())), precision=lax.Precision.HIGHEST,
                           preferred_element_type=F32)
    lane = lax.broadcasted_iota(jnp.int32, (R, LANES), 1)
    sel = _top_select(jnp.where(lane < nb, gate, -jnp.inf), lane)

    b31 = b31_ref[...]
    adj = adj_ref[...]
    m = jnp.full((R, 1), NEG, F32)
    for n in range(nb):
        kb = kbuf[slot, n * B:(n + 1) * B, :].astype(BF16)
        s = lax.dot_general(qs, kb, (((1,), (1,)), ((), ())), preferred_element_type=F32)
        bias = adj if n == nb - 1 else b31
        s = jnp.where(sel[:, n:n + 1], s + bias, NEG)
        lg_ref[:, n * B:(n + 1) * B] = s
        m = jnp.maximum(m, jnp.max(s, axis=1, keepdims=True))

    s_own = lax.dot_general(qpad * scale, kn_ref[...], (((1,), (1,)), ((), ())),
                            preferred_element_type=F32) + own_ref[...]
    m = jnp.maximum(m, jnp.max(s_own, axis=1, keepdims=True))

    p_own = jnp.exp(s_own - m)
    l = jnp.sum(p_own, axis=1, keepdims=True)
    acc = jnp.dot(p_own, vn_ref[...], preferred_element_type=F32)
    for n in range(nb):
        p = jnp.exp(lg_ref[:, n * B:(n + 1) * B] - m)
        l = l + jnp.sum(p, axis=1, keepdims=True)
        vb = vbuf[slot, n * B:(n + 1) * B, :].astype(BF16)
        acc = acc + jnp.dot(p.astype(BF16), vb, preferred_element_type=F32)
    acc = acc / l
    rows = jnp.concatenate([acc[:half, :MB_HD], acc[half:, MB_HD:]], axis=0)
    o_ref[...] = jnp.concatenate([rows[h * T:(h + 1) * T] for h in range(MB_HEADS)], axis=1)


def _moba_sample(z, page_table, cache_k, cache_v, bias_own, bias_prev, DB, T, qcol, kcol, vcol):
    n_pages = page_table.shape[1]
    past = n_pages * PAGE_SIZE
    R = MB_HEADS * T
    W = MB_HEADS * MB_HD
    KW = MB_KV_HEADS * MB_HD
    ck = cache_k.reshape(cache_k.shape[0], PAGE_SIZE, KW)
    cv = cache_v.reshape(cache_v.shape[0], PAGE_SIZE, KW)
    own = bias_own[:, :T, :T].reshape(R, T)
    adj = bias_prev[:, :T, :].reshape(R, MB_BLOCK)
    b31 = bias_prev[:, :T, 0:1].reshape(R, 1)
    const = lambda b, pt: (0, 0)
    grid_spec = pltpu.PrefetchScalarGridSpec(
        num_scalar_prefetch=1,
        grid=(DB,),
        in_specs=[pl.BlockSpec((T, W), lambda b, pt: (b, qcol)),
                  pl.BlockSpec((T, KW), lambda b, pt: (b, kcol)),
                  pl.BlockSpec((T, KW), lambda b, pt: (b, vcol)),
                  pl.BlockSpec(memory_space=pl.ANY),
                  pl.BlockSpec(memory_space=pl.ANY),
                  pl.BlockSpec((R, T), const),
                  pl.BlockSpec((R, MB_BLOCK), const),
                  pl.BlockSpec((R, 1), const)],
        out_specs=pl.BlockSpec((T, W), lambda b, pt: (b, 0)),
        scratch_shapes=[pltpu.VMEM((2, past, KW), F32),
                        pltpu.VMEM((2, past, KW), F32),
                        pltpu.VMEM((R, past), F32),
                        pltpu.SemaphoreType.DMA((2, 2))],
    )
    return pl.pallas_call(
        functools.partial(_moba_sample_kernel, T=T, n_pages=n_pages, n_seq=DB),
        grid_spec=grid_spec,
        out_shape=jax.ShapeDtypeStruct((DB * T, W), F32),
        compiler_params=_params("arbitrary"),
        name="moba_sample",
    )(page_table, z, z, z, ck, cv, own, adj, b31)


def _merge_kernel(oa_ref, ob_ref, ga_ref, gb_ref, x_ref, wpa_ref, wpb_ref, wo_ref, o_ref):
    a = jnp.dot(oa_ref[...].astype(BF16), wpa_ref[...], preferred_element_type=F32)
    b = jnp.dot(ob_ref[...].astype(BF16), wpb_ref[...], preferred_element_type=F32)
    merged = _sigmoid(ga_ref[...]) * a + _sigmoid(gb_ref[...]) * b
    o_ref[...] = x_ref[...] + jnp.dot(merged.astype(BF16), wo_ref[...], preferred_element_type=F32)


def _merge(oa, ob, z, gcol, x, wpa, wpb, wo, tm=512):
    M, D = x.shape
    tm = min(tm, M)
    row = lambda i: (i, 0)
    const = lambda i: (0, 0)
    return pl.pallas_call(
        _merge_kernel,
        grid=(M // tm,),
        in_specs=[pl.BlockSpec((tm, D), row), pl.BlockSpec((tm, D), row),
                  pl.BlockSpec((tm, D), lambda i: (i, gcol)), pl.BlockSpec((tm, D), lambda i: (i, gcol + 1)),
                  pl.BlockSpec((tm, D), row),
                  pl.BlockSpec((D, D), const), pl.BlockSpec((D, D), const), pl.BlockSpec((D, D), const)],
        out_specs=pl.BlockSpec((tm, D), row),
        out_shape=jax.ShapeDtypeStruct((M, D), F32),
        compiler_params=_params("parallel"),
        name="merge",
    )(oa, ob, z, z, x, wpa, wpb, wo)


def _xattn_kernel(q_ref, mk_ref, mv_ref, o_ref):
    hd = q_ref.shape[1] // MX_HEADS
    scale = hd ** -0.5
    for h in range(MX_HEADS):
        sl = slice(h * hd, (h + 1) * hd)
        q = (q_ref[:, sl] * scale).astype(BF16)
        k = mk_ref[0, :, sl].astype(BF16)
        v = mv_ref[0, :, sl].astype(BF16)
        s = lax.dot_general(q, k, (((1,), (1,)), ((), ())), preferred_element_type=F32)
        p = jnp.exp(s - jnp.max(s, axis=1, keepdims=True))
        o = jnp.dot(p.astype(BF16), v, preferred_element_type=F32)
        o_ref[:, sl] = o / jnp.sum(p, axis=1, keepdims=True)


def _xattn(q, mk, mv, B, L, tm=512):
    D = q.shape[1]
    mem = mk.shape[1]
    tm = min(tm, L)
    nt = L // tm
    return pl.pallas_call(
        _xattn_kernel,
        grid=(B, nt),
        in_specs=[pl.BlockSpec((tm, D), lambda b, t: (b * nt + t, 0)),
                  pl.BlockSpec((1, mem, D), lambda b, t: (b, 0, 0)),
                  pl.BlockSpec((1, mem, D), lambda b, t: (b, 0, 0))],
        out_specs=pl.BlockSpec((tm, D), lambda b, t: (b * nt + t, 0)),
        out_shape=jax.ShapeDtypeStruct((B * L, D), F32),
        compiler_params=_params("parallel", "parallel"),
        name="xattn",
    )(q, mk, mv)


def _ffn_kernel(x_ref, g_ref, wg_ref, wu_ref, wd_ref, fg_ref, o_ref, *, tf):
    x = x_ref[...]
    xn = _rms(x, g_ref[...]).astype(BF16)
    acc = x
    for c in range(wg_ref.shape[1] // tf):
        sl = slice(c * tf, (c + 1) * tf)
        a = jnp.dot(xn, wg_ref[:, sl], preferred_element_type=F32)
        u = jnp.dot(xn, wu_ref[:, sl], preferred_element_type=F32)
        acc = acc + jnp.dot((_silu(a) * u).astype(BF16), wd_ref[sl, :], preferred_element_type=F32)
    o_ref[...] = _rms(acc, fg_ref[...])


def _ffn(x, g, wg, wu, wd, fg, tm=256, tf=256):
    M, D = x.shape
    FF = wg.shape[1]
    tm = min(tm, M)
    row = lambda i: (i, 0)
    const = lambda i: (0, 0)
    return pl.pallas_call(
        functools.partial(_ffn_kernel, tf=tf),
        grid=(M // tm,),
        in_specs=[pl.BlockSpec((tm, D), row), pl.BlockSpec((1, D), const),
                  pl.BlockSpec((D, FF), const), pl.BlockSpec((D, FF), const), pl.BlockSpec((FF, D), const),
                  pl.BlockSpec((1, D), const)],
        out_specs=pl.BlockSpec((tm, D), row),
        out_shape=jax.ShapeDtypeStruct((M, D), F32),
        compiler_params=_params("parallel"),
        name="ffn",
    )(x, g.reshape(1, D), wg, wu, wd, fg.reshape(1, D))


_Z_GATES = 5
_Z_Q = 4
_Z_K128 = 56
_Z_V128 = 58
_Z_K256 = 28
_Z_V256 = 29


def _group(x2, B, L, chunk, s0, moba_fn, W):
    z = _matmul(x2, W["w_in"], g=W["norm_mix_g"], tm=min(1024, B * L), tn=512)
    oa, S = _gla(z, W["lb_logits"], W["hg_norm_g"], s0, B, L, chunk)
    ob = moba_fn(z)
    h = _merge(oa, ob, z, _Z_GATES, x2, W["w_pa"], W["w_pb"], W["w_out"])
    return z, h, S


def kernel(x_prompt, x_sample, cache_k, cache_v, cache_mem_k, cache_mem_v, state_hgrn, page_table, mem_prompt, norm_mix_g, w_in, hg_lb_logits, hg_norm_g, w_proj_a, w_proj_b, w_out, rel_bias, norm_x_g, w_xq, w_mk, w_mv, w_xo, norm_ffn_g, w_gate, w_up, w_down, final_norm_g):
    Bp, Lp, D = x_prompt.shape
    DB, T, _ = x_sample.shape
    l = 0
    win = w_in[l]
    c0 = 4 * HG_HEADS * HG_DK + MB_HEADS * MB_HD
    c1 = c0 + 2 * MB_KV_HEADS * MB_HD
    W = {
        "w_in": jnp.concatenate([win[:, :c0], win[:, c1:], win[:, c0:c1]], axis=1).astype(BF16),
        "norm_mix_g": norm_mix_g[l],
        "lb_logits": hg_lb_logits,
        "hg_norm_g": hg_norm_g[l],
        "w_pa": w_proj_a[l].astype(BF16), "w_pb": w_proj_b[l].astype(BF16), "w_out": w_out[l].astype(BF16),
    }
    wxq, wxo = w_xq[l].astype(BF16), w_xo[l].astype(BF16)
    wmkv = jnp.concatenate([w_mk[l], w_mv[l]], axis=1).astype(BF16)
    wg, wu, wd = w_gate[l].astype(BF16), w_up[l].astype(BF16), w_down[l].astype(BF16)

    bias_own, bias_prev = _bias_tiles(rel_bias)

    xp = x_prompt.reshape(Bp * Lp, D)

    def moba_p(z):
        ka, vb, means = _moba_prep(z, Lp, _Z_K128, _Z_V128)
        return _moba_prompt(z, Lp, 2 * _Z_Q, ka, vb, means, bias_own, bias_prev)

    s0 = jnp.zeros((Bp, HG_HEADS, HG_DK, HG_DV), F32)
    zp, hp, Sp = _group(xp, Bp, Lp, math.gcd(Lp, 64), s0, moba_p, W)
    kw = MB_KV_HEADS * MB_HD
    kp = zp[:, _Z_K256 * kw:(_Z_K256 + 1) * kw].reshape(Bp, Lp, MB_KV_HEADS, MB_HD)
    vp = zp[:, _Z_V256 * kw:(_Z_V256 + 1) * kw].reshape(Bp, Lp, MB_KV_HEADS, MB_HD)

    mem = mem_prompt.reshape(Bp * mem_prompt.shape[1], D)
    mkv = _matmul(mem, wmkv, tm=256, tn=512)
    mkp, mvp = mkv[:, :D], mkv[:, D:]
    qx = _matmul(hp, wxq, g=norm_x_g[l], tm=1024, tn=512)
    ox = _xattn(qx, mkp.reshape(Bp, -1, D), mvp.reshape(Bp, -1, D), Bp, Lp)
    hp = _matmul(ox, wxo, res=hp, tm=1024, tn=512)
    yp = _ffn(hp, norm_ffn_g[l], wg, wu, wd, final_norm_g)

    xs = x_sample.reshape(DB * T, D)

    def moba_s(z):
        return _moba_sample(z, page_table, cache_k[l], cache_v[l], bias_own, bias_prev, DB, T,
                            _Z_Q, _Z_K256, _Z_V256)

    zs, hs, Ss = _group(xs, DB, T, T, state_hgrn[l], moba_s, W)
    ksn = zs[:, _Z_K256 * kw:(_Z_K256 + 1) * kw].reshape(DB, T, MB_KV_HEADS, MB_HD)
    vsn = zs[:, _Z_V256 * kw:(_Z_V256 + 1) * kw].reshape(DB, T, MB_KV_HEADS, MB_HD)
    qxs = _matmul(hs, wxq, g=norm_x_g[l], tn=512)
    oxs = _xattn(qxs, cache_mem_k[l].reshape(DB, -1, D), cache_mem_v[l].reshape(DB, -1, D), DB, T)
    hs = _matmul(oxs, wxo, res=hs, tn=512)
    ys = _ffn(hs, norm_ffn_g[l], wg, wu, wd, final_norm_g)

    mshape = (1, Bp, mem_prompt.shape[1], MX_HEADS, D // MX_HEADS)
    return (yp.reshape(Bp, Lp, D), ys.reshape(DB, T, D),
            Sp[None].astype(state_hgrn.dtype), kp[None], vp[None],
            mkp.reshape(mshape), mvp.reshape(mshape),
            Ss[None].astype(state_hgrn.dtype), ksn[None], vsn[None])
```

```python
import functools
import math

import jax
import jax.numpy as jnp
import numpy as np
from jax import lax
from jax.experimental import pallas as pl
from jax.experimental.pallas import tpu as pltpu

F32 = jnp.float32
BF16 = jnp.bfloat16

EPS = 1e-6
HG_HEADS = 8
HG_DK = 128
HG_DV = 128
HG_SUB = 16
MB_HEADS = 8
MB_KV_HEADS = 2
MB_GROUP = MB_HEADS // MB_KV_HEADS
MB_HD = 128
MB_BLOCK = 256
MB_TOPK = 3
RB_BUCKETS = 32
RB_MAX_DIST = 128
MX_HEADS = 4
PAGE_SIZE = 128
NEG = -1e30
LANES = 128
VMEM_LIMIT = 56 * 1024 * 1024


def _params(*sem):
    return pltpu.CompilerParams(dimension_semantics=sem, vmem_limit_bytes=VMEM_LIMIT)


def _sigmoid(x):
    return 1.0 / (1.0 + jnp.exp(-x))


def _silu(x):
    return x * _sigmoid(x)


def _rms(x, g):
    return x * lax.rsqrt(jnp.mean(x * x, axis=-1, keepdims=True) + EPS) * g


def _matmul_kernel(*refs, norm, residual):
    refs = list(refs)
    x_ref = refs.pop(0)
    g_ref = refs.pop(0) if norm else None
    w_ref = refs.pop(0)
    r_ref = refs.pop(0) if residual else None
    o_ref, xs_ref = refs

    @pl.when(pl.program_id(1) == 0)
    def _():
        x = x_ref[...]
        if norm:
            x = _rms(x, g_ref[...])
        xs_ref[...] = x.astype(BF16)

    acc = jnp.dot(xs_ref[...], w_ref[...], preferred_element_type=F32)
    if residual:
        acc = acc + r_ref[...]
    o_ref[...] = acc


def _matmul(x, w, g=None, res=None, tm=512, tn=512):
    M, K = x.shape
    N = w.shape[1]
    tm = min(tm, M)
    tn = min(tn, N)
    assert M % tm == 0 and N % tn == 0
    args = [x]
    specs = [pl.BlockSpec((tm, K), lambda i, j: (i, 0))]
    if g is not None:
        args.append(g.reshape(1, K))
        specs.append(pl.BlockSpec((1, K), lambda i, j: (0, 0)))
    args.append(w)
    specs.append(pl.BlockSpec((K, tn), lambda i, j: (0, j)))
    if res is not None:
        args.append(res)
        specs.append(pl.BlockSpec((tm, tn), lambda i, j: (i, j)))
    return pl.pallas_call(
        functools.partial(_matmul_kernel, norm=g is not None, residual=res is not None),
        grid=(M // tm, N // tn),
        in_specs=specs,
        out_specs=pl.BlockSpec((tm, tn), lambda i, j: (i, j)),
        out_shape=jax.ShapeDtypeStruct((M, N), F32),
        scratch_shapes=[pltpu.VMEM((tm, K), BF16)],
        compiler_params=_params("parallel", "arbitrary"),
        name="matmul",
    )(*args)


def _subblock_cumsum(g, sub):
    row = lax.broadcasted_iota(jnp.int32, g.shape, 0) % sub
    s = 1
    while s < sub:
        g = g + jnp.where(row >= s, pltpu.roll(g, s, 0), 0.0)
        s *= 2
    return g


def _gla_kernel(q_ref, f_ref, i_ref, g_ref, lbl_ref, ng_ref, s0_ref, o_ref, so_ref, st_ref, *, C, sub):
    c = pl.program_id(1)
    nsub = C // sub
    mid = sub // 2 - 1

    @pl.when(c == 0)
    def _():
        for h in range(HG_HEADS):
            st_ref[h] = s0_ref[0, h].T

    lbl = lbl_ref[...]
    e = jnp.exp(lbl - jnp.max(lbl, axis=0, keepdims=True))
    lb_all = e[0:1] / jnp.sum(e, axis=0, keepdims=True)
    ng = ng_ref[...]

    row = lax.broadcasted_iota(jnp.int32, (C, C), 0)
    col = lax.broadcasted_iota(jnp.int32, (C, C), 1)
    diag_mask = (row // sub == col // sub) & (row >= col)

    for h in range(HG_HEADS):
        sl = slice(h * HG_DK, (h + 1) * HG_DK)
        lb = lb_all[:, sl]
        q3 = q_ref[:, sl].reshape(nsub, sub, HG_DK)
        f = lb + (1.0 - lb) * _sigmoid(f_ref[:, sl])
        k3 = (1.0 - f).reshape(nsub, sub, HG_DK)
        cs = _subblock_cumsum(jnp.log(f), sub).reshape(nsub, sub, HG_DK)
        v = i_ref[:, sl].astype(BF16)

        T = cs[:, sub - 1:sub, :]
        m = cs[:, mid:mid + 1, :]
        qt = q3 * jnp.exp(cs)
        kh = k3 * jnp.exp(T - cs)
        qd = (q3 * jnp.exp(cs - m)).reshape(C, HG_DK).astype(BF16)
        kd = (k3 * jnp.exp(m - cs)).reshape(C, HG_DK).astype(BF16)
        att = jnp.where(diag_mask,
                        lax.dot_general(qd, kd, (((1,), (1,)), ((), ())), preferred_element_type=F32),
                        0.0)

        PT = [jnp.zeros((1, HG_DK), F32)]
        for I in range(nsub):
            PT.append(PT[-1] + T[I])

        def scaled_k(I):
            parts = []
            for J in range(nsub):
                if J < I:
                    parts.append(kh[J] * jnp.exp(PT[I] - PT[J + 1]))
                else:
                    parts.append(jnp.zeros((sub, HG_DK), F32))
            return jnp.concatenate(parts, axis=0).astype(BF16) if nsub > 1 else parts[0].astype(BF16)

        if nsub > 1:
            rows = [jnp.zeros((sub, C), F32)]
            for I in range(1, nsub):
                rows.append(lax.dot_general(qt[I].astype(BF16), scaled_k(I), (((1,), (1,)), ((), ())),
                                            preferred_element_type=F32))
            att = att + jnp.concatenate(rows, axis=0)

        st = st_ref[h]
        qc = jnp.concatenate([qt[I] * jnp.exp(PT[I]) for I in range(nsub)], axis=0).astype(BF16)
        o = (jnp.dot(att.astype(BF16), v, preferred_element_type=F32)
             + lax.dot_general(qc, st.astype(BF16), (((1,), (1,)), ((), ())), preferred_element_type=F32))

        kc = scaled_k(nsub)
        st_new = (st * jnp.exp(PT[nsub])
                  + lax.dot_general(v, kc, (((0,), (0,)), ((), ())), preferred_element_type=F32))
        st_ref[h] = st_new

        o = o * lax.rsqrt(jnp.mean(o * o, axis=-1, keepdims=True) + EPS) * ng * _silu(g_ref[:, sl])
        o_ref[:, sl] = o

    @pl.when(c == pl.num_programs(1) - 1)
    def _():
        for h in range(HG_HEADS):
            so_ref[0, h] = st_ref[h].T


def _gla(z, lb_logits, norm_g, s0, B, L, C):
    nc = L // C
    sub = min(HG_SUB, C)
    W = HG_HEADS * HG_DK

    def zspec(j):
        return pl.BlockSpec((C, W), lambda b, c, j=j: (b * nc + c, j))

    return pl.pallas_call(
        functools.partial(_gla_kernel, C=C, sub=sub),
        grid=(B, nc),
        in_specs=[zspec(0), zspec(1), zspec(2), zspec(3),
                  pl.BlockSpec((2, W), lambda b, c: (0, 0)),
                  pl.BlockSpec((1, HG_DV), lambda b, c: (0, 0)),
                  pl.BlockSpec((1, HG_HEADS, HG_DK, HG_DV), lambda b, c: (b, 0, 0, 0))],
        out_specs=[pl.BlockSpec((C, W), lambda b, c: (b * nc + c, 0)),
                   pl.BlockSpec((1, HG_HEADS, HG_DK, HG_DV), lambda b, c: (b, 0, 0, 0))],
        out_shape=[jax.ShapeDtypeStruct((B * L, W), F32),
                   jax.ShapeDtypeStruct((B, HG_HEADS, HG_DK, HG_DV), F32)],
        scratch_shapes=[pltpu.VMEM((HG_HEADS, HG_DV, HG_DK), F32)],
        compiler_params=_params("arbitrary", "arbitrary"),
        name="gla",
    )(z, z, z, z, lb_logits, norm_g.reshape(1, HG_DV), s0)


def _bucket_upper_bounds():
    max_exact = RB_BUCKETS // 2
    d = np.arange(1, 4 * RB_MAX_DIST)
    large = max_exact + (np.log(d.astype(np.float32) / max_exact) / math.log(RB_MAX_DIST / max_exact)
                         * (RB_BUCKETS - max_exact)).astype(np.int32)
    bucket = np.where(d < max_exact, d, np.minimum(large, RB_BUCKETS - 1))
    bucket = np.concatenate([[0], bucket])
    return [int(np.max(np.nonzero(bucket == b)[0])) for b in range(RB_BUCKETS - 1)]


def _bias_kernel(rb_ref, own_ref, prev_ref):
    h = pl.program_id(0)
    i = lax.broadcasted_iota(jnp.int32, (MB_BLOCK, MB_BLOCK), 0)
    j = lax.broadcasted_iota(jnp.int32, (MB_BLOCK, MB_BLOCK), 1)
    dmax = _bucket_upper_bounds()

    def table(d):
        val = jnp.full(d.shape, rb_ref[RB_BUCKETS - 1, h], F32)
        for b in range(RB_BUCKETS - 2, -1, -1):
            val = jnp.where(d <= dmax[b], rb_ref[b, h], val)
        return val

    d = i - j
    own_ref[0] = jnp.where(d >= 0, table(d), NEG)
    prev_ref[0] = table(d + MB_BLOCK)


def _bias_tiles(rel_bias):
    shp = jax.ShapeDtypeStruct((MB_HEADS, MB_BLOCK, MB_BLOCK), F32)
    spec = pl.BlockSpec((1, MB_BLOCK, MB_BLOCK), lambda h: (h, 0, 0))
    return pl.pallas_call(
        _bias_kernel,
        grid=(MB_HEADS,),
        in_specs=[pl.BlockSpec(memory_space=pltpu.SMEM)],
        out_specs=[spec, spec],
        out_shape=[shp, shp],
        compiler_params=_params("arbitrary"),
        name="bias_tiles",
    )(rel_bias)


def _top_select(gate, lane):
    sel = jnp.zeros(gate.shape, jnp.bool_)
    lane = lane.astype(F32)
    for _ in range(MB_TOPK):
        m = jnp.max(gate, axis=1, keepdims=True)
        idx = jnp.min(jnp.where(gate == m, lane, float(2 ** 30)), axis=1, keepdims=True)
        hit = (lane == idx) & (m > -jnp.inf)
        sel = sel | hit
        gate = jnp.where(hit, -jnp.inf, gate)
    return sel


def _moba_prep_kernel(k_ref, v_ref, ka_ref, vb_ref, mean_ref):
    n = pl.program_id(1)
    k = k_ref[...]
    lane = lax.broadcasted_iota(jnp.int32, (MB_BLOCK, LANES), 1)
    onehot = jnp.where(lane == n, 1.0, 0.0).astype(BF16)
    ka_ref[0] = jnp.concatenate([k.astype(BF16), onehot], axis=1)
    vb_ref[0] = jnp.concatenate([v_ref[...].astype(BF16), jnp.ones((MB_BLOCK, LANES), BF16)], axis=1)
    mean_ref[0, 0] = jnp.mean(k, axis=0, keepdims=True)


def _moba_prep(z, L, kcol, vcol):
    nb = L // MB_BLOCK
    return pl.pallas_call(
        _moba_prep_kernel,
        grid=(MB_KV_HEADS, nb),
        in_specs=[pl.BlockSpec((MB_BLOCK, MB_HD), lambda g, n: (n, kcol + g)),
                  pl.BlockSpec((MB_BLOCK, MB_HD), lambda g, n: (n, vcol + g))],
        out_specs=[pl.BlockSpec((1, MB_BLOCK, 2 * MB_HD), lambda g, n: (g, n, 0)),
                   pl.BlockSpec((1, MB_BLOCK, 2 * MB_HD), lambda g, n: (g, n, 0)),
                   pl.BlockSpec((1, 1, 1, MB_HD), lambda g, n: (g, n, 0, 0))],
        out_shape=[jax.ShapeDtypeStruct((MB_KV_HEADS, L, 2 * MB_HD), BF16),
                   jax.ShapeDtypeStruct((MB_KV_HEADS, L, 2 * MB_HD), BF16),
                   jax.ShapeDtypeStruct((MB_KV_HEADS, nb, 1, MB_HD), F32)],
        compiler_params=_params("arbitrary", "arbitrary"),
        name="moba_prep",
    )(z, z)


def _moba_prompt_kernel(q_ref, ka_ref, va_ref, mean_ref, own_ref, prev_ref, o_ref,
                        qa_ref, m_ref, acc_ref):
    i = pl.program_id(1)
    G, B = MB_GROUP, MB_BLOCK
    R = G * B
    scale = MB_HD ** -0.5
    lane = lax.broadcasted_iota(jnp.int32, (B, LANES), 1)
    means = mean_ref[0]

    for j in range(G):
        q = q_ref[:, j * MB_HD:(j + 1) * MB_HD]
        gate = lax.dot_general(q, means, (((1,), (1,)), ((), ())), precision=lax.Precision.HIGHEST,
                               preferred_element_type=F32)
        sel = _top_select(jnp.where(lane < i, gate, -jnp.inf), lane)
        b31 = prev_ref[j, :, 0:1]
        mb = jnp.where(sel, jnp.where(lane == i - 1, 0.0, b31), NEG)
        mb = jnp.where(lane == i, 0.0, mb)
        qa_ref[j * B:(j + 1) * B, :] = jnp.concatenate([(q * scale).astype(BF16), mb.astype(BF16)], axis=1)

    def attend(n, bias_ref, first):
        rows = pl.ds(pl.multiple_of(n * B, B), B)
        kn = ka_ref[0, rows, :]
        vn = va_ref[0, rows, :]
        s = lax.dot_general(qa_ref[...], kn, (((1,), (1,)), ((), ())), preferred_element_type=F32)
        if bias_ref is not None:
            s = s + bias_ref[...].reshape(R, B)
        mx = jnp.max(s, axis=1, keepdims=True)
        if first:
            m_new = jnp.broadcast_to(mx, (R, LANES))
            p = jnp.exp(s - mx)
            acc_ref[...] = jnp.dot(p.astype(BF16), vn, preferred_element_type=F32)
        else:
            m_prev = m_ref[...]
            m_new = jnp.maximum(m_prev, mx)
            alpha = jnp.exp(m_prev - m_new)
            p = jnp.exp(s - jnp.concatenate([m_new, m_new], axis=1))
            acc_ref[...] = (jnp.concatenate([alpha, alpha], axis=1) * acc_ref[...]
                            + jnp.dot(p.astype(BF16), vn, preferred_element_type=F32))
        m_ref[...] = m_new

    attend(i, own_ref, True)

    @pl.when(i >= 1)
    def _():
        attend(i - 1, prev_ref, False)

    def body(n, carry):
        attend(n, None, False)
        return carry

    lax.fori_loop(0, jnp.maximum(i - 1, 0), body, 0)

    acc = acc_ref[...]
    o = acc[:, :MB_HD] / acc[:, MB_HD:]
    for j in range(G):
        o_ref[:, j * MB_HD:(j + 1) * MB_HD] = o[j * B:(j + 1) * B]


def _moba_prompt(z, L, qcol, ka, vb, means, bias_own, bias_prev):
    nb = L // MB_BLOCK
    G, B = MB_GROUP, MB_BLOCK
    means_p = jnp.pad(means.reshape(MB_KV_HEADS, nb, MB_HD), ((0, 0), (0, LANES - nb), (0, 0)))
    return pl.pallas_call(
        _moba_prompt_kernel,
        grid=(MB_KV_HEADS, nb),
        in_specs=[pl.BlockSpec((B, G * MB_HD), lambda g, i: (i, qcol + g)),
                  pl.BlockSpec((1, L, 2 * MB_HD), lambda g, i: (g, 0, 0), pipeline_mode=pl.Buffered(1)),
                  pl.BlockSpec((1, L, 2 * MB_HD), lambda g, i: (g, 0, 0), pipeline_mode=pl.Buffered(1)),
                  pl.BlockSpec((1, LANES, MB_HD), lambda g, i: (g, 0, 0)),
                  pl.BlockSpec((G, B, B), lambda g, i: (g, 0, 0)),
                  pl.BlockSpec((G, B, B), lambda g, i: (g, 0, 0))],
        out_specs=pl.BlockSpec((B, G * MB_HD), lambda g, i: (i, g)),
        out_shape=jax.ShapeDtypeStruct((L, MB_HEADS * MB_HD), F32),
        scratch_shapes=[pltpu.VMEM((G * B, 2 * MB_HD), BF16),
                        pltpu.VMEM((G * B, LANES), F32),
                        pltpu.VMEM((G * B, 2 * MB_HD), F32)],
        compiler_params=_params("arbitrary", "arbitrary"),
        name="moba_prompt",
    )(z, ka, vb, means_p, bias_own, bias_prev)


def _moba_sample_kernel(pt_ref, q_ref, kn_ref, vn_ref, ck_hbm, cv_hbm, own_ref, adj_ref, b31_ref, o_ref,
                        kbuf, vbuf, lg_ref, sem, *, T, n_pages, n_seq):
    b = pl.program_id(0)
    B = MB_BLOCK
    past = n_pages * PAGE_SIZE
    nb = past // B
    R = MB_HEADS * T
    half = R // MB_KV_HEADS
    scale = MB_HD ** -0.5

    KVH = MB_KV_HEADS
    prow = PAGE_SIZE * KVH

    def page_copies(seq, slot, j):
        dst = pl.ds(pl.multiple_of(j * prow, prow), prow)
        page = pt_ref[seq, j]
        return (pltpu.make_async_copy(ck_hbm.at[page], kbuf.at[slot, dst, :], sem.at[0, slot]),
                pltpu.make_async_copy(cv_hbm.at[page], vbuf.at[slot, dst, :], sem.at[1, slot]))

    def start_seq(seq, slot):
        def body(j, c):
            ck, cv = page_copies(seq, slot, j)
            ck.start()
            cv.start()
            return c
        lax.fori_loop(0, n_pages, body, 0)

    def wait_seq(seq, slot):
        def body(j, c):
            ck, cv = page_copies(seq, slot, j)
            ck.wait()
            cv.wait()
            return c
        lax.fori_loop(0, n_pages, body, 0)

    slot = b % 2

    @pl.when(b == 0)
    def _():
        start_seq(0, 0)

    @pl.when(b + 1 < n_seq)
    def _():
        start_seq(b + 1, 1 - slot)

    wait_seq(b, slot)

    def block(buf, n, g):
        return buf[slot, pl.ds(n * B * KVH + g, B, stride=KVH), :]

    q = q_ref[...]
    Q = jnp.concatenate([q[:, h * MB_HD:(h + 1) * MB_HD] for h in range(MB_HEADS)], axis=0)
    qf = Q * scale
    qs = qf.astype(BF16)
    grow = [slice(g * half, (g + 1) * half) for g in range(KVH)]

    mean_rows = [[] for _ in range(KVH)]
    for n in range(nb):
        for g in range(KVH):
            kf = block(kbuf, n, g)
            mean_rows[g].append(jnp.mean(kf, axis=0, keepdims=True))
            lg_ref[grow[g], n * B:(n + 1) * B] = lax.dot_general(
                qs[grow[g]], kf.astype(BF16), (((1,), (1,)), ((), ())), preferred_element_type=F32)

    gates = []
    for g in range(KVH):
        means = jnp.concatenate(mean_rows[g] + [jnp.zeros((LANES - nb, MB_HD), F32)], axis=0)
        gates.append(lax.dot_general(Q[grow[g]], means, (((1,), (1,)), ((), ())),
                                     precision=lax.Precision.HIGHEST, preferred_element_type=F32))
    gate = jnp.concatenate(gates, axis=0)
    lane = lax.broadcasted_iota(jnp.int32, (R, LANES), 1)
    sel = _top_select(jnp.where(lane < nb, gate, -jnp.inf), lane)
    mb = jnp.where(sel, jnp.where(lane == nb - 1, 0.0, b31_ref[...]), NEG)
    adj = adj_ref[...]

    def logits(n):
        s = lg_ref[:, n * B:(n + 1) * B] + mb[:, n:n + 1]
        return s + adj if n == nb - 1 else s

    kn = kn_ref[...]
    vn = vn_ref[...]
    s_own = jnp.concatenate(
        [lax.dot_general(qf[grow[g]], kn[:, g * MB_HD:(g + 1) * MB_HD], (((1,), (1,)), ((), ())),
                         preferred_element_type=F32) for g in range(KVH)], axis=0) + own_ref[...]

    m = jnp.max(s_own, axis=1, keepdims=True)
    for n in range(nb):
        m = jnp.maximum(m, jnp.max(logits(n), axis=1, keepdims=True))

    p_own = jnp.exp(s_own - m)
    l = jnp.sum(p_own, axis=1, keepdims=True)
    acc = [jnp.dot(p_own[grow[g]], vn[:, g * MB_HD:(g + 1) * MB_HD], preferred_element_type=F32)
           for g in range(KVH)]
    for n in range(nb):
        p = jnp.exp(logits(n) - m)
        l = l + jnp.sum(p, axis=1, keepdims=True)
        pb = p.astype(BF16)
        for g in range(KVH):
            acc[g] = acc[g] + jnp.dot(pb[grow[g]], block(vbuf, n, g).astype(BF16),
                                      preferred_element_type=F32)
    rows = jnp.concatenate(acc, axis=0) / l
    o_ref[...] = jnp.concatenate([rows[h * T:(h + 1) * T] for h in range(MB_HEADS)], axis=1)


def _moba_sample(z, page_table, cache_k, cache_v, bias_own, bias_prev, DB, T, qcol, kcol, vcol):
    n_pages = page_table.shape[1]
    past = n_pages * PAGE_SIZE
    R = MB_HEADS * T
    W = MB_HEADS * MB_HD
    KW = MB_KV_HEADS * MB_HD
    ck = cache_k.reshape(cache_k.shape[0], PAGE_SIZE * MB_KV_HEADS, MB_HD)
    cv = cache_v.reshape(cache_v.shape[0], PAGE_SIZE * MB_KV_HEADS, MB_HD)
    own = bias_own[:, :T, :T].reshape(R, T)
    adj = bias_prev[:, :T, :].reshape(R, MB_BLOCK)
    b31 = bias_prev[:, :T, 0:1].reshape(R, 1)
    const = lambda b, pt: (0, 0)
    grid_spec = pltpu.PrefetchScalarGridSpec(
        num_scalar_prefetch=1,
        grid=(DB,),
        in_specs=[pl.BlockSpec((T, W), lambda b, pt: (b, qcol)),
                  pl.BlockSpec((T, KW), lambda b, pt: (b, kcol)),
                  pl.BlockSpec((T, KW), lambda b, pt: (b, vcol)),
                  pl.BlockSpec(memory_space=pl.ANY),
                  pl.BlockSpec(memory_space=pl.ANY),
                  pl.BlockSpec((R, T), const),
                  pl.BlockSpec((R, MB_BLOCK), const),
                  pl.BlockSpec((R, 1), const)],
        out_specs=pl.BlockSpec((T, W), lambda b, pt: (b, 0)),
        scratch_shapes=[pltpu.VMEM((2, past * MB_KV_HEADS, MB_HD), F32),
                        pltpu.VMEM((2, past * MB_KV_HEADS, MB_HD), F32),
                        pltpu.VMEM((R, past), F32),
                        pltpu.SemaphoreType.DMA((2, 2))],
    )
    return pl.pallas_call(
        functools.partial(_moba_sample_kernel, T=T, n_pages=n_pages, n_seq=DB),
        grid_spec=grid_spec,
        out_shape=jax.ShapeDtypeStruct((DB * T, W), F32),
        compiler_params=_params("arbitrary"),
        name="moba_sample",
    )(page_table, z, z, z, ck, cv, own, adj, b31)


def _merge_kernel(oa_ref, ob_ref, ga_ref, gb_ref, x_ref, wpa_ref, wpb_ref, wo_ref, o_ref):
    a = jnp.dot(oa_ref[...].astype(BF16), wpa_ref[...], preferred_element_type=F32)
    b = jnp.dot(ob_ref[...].astype(BF16), wpb_ref[...], preferred_element_type=F32)
    merged = _sigmoid(ga_ref[...]) * a + _sigmoid(gb_ref[...]) * b
    o_ref[...] = x_ref[...] + jnp.dot(merged.astype(BF16), wo_ref[...], preferred_element_type=F32)


def _merge(oa, ob, z, gcol, x, wpa, wpb, wo, tm=512):
    M, D = x.shape
    tm = min(tm, M)
    row = lambda i: (i, 0)
    const = lambda i: (0, 0)
    return pl.pallas_call(
        _merge_kernel,
        grid=(M // tm,),
        in_specs=[pl.BlockSpec((tm, D), row), pl.BlockSpec((tm, D), row),
                  pl.BlockSpec((tm, D), lambda i: (i, gcol)), pl.BlockSpec((tm, D), lambda i: (i, gcol + 1)),
                  pl.BlockSpec((tm, D), row),
                  pl.BlockSpec((D, D), const), pl.BlockSpec((D, D), const), pl.BlockSpec((D, D), const)],
        out_specs=pl.BlockSpec((tm, D), row),
        out_shape=jax.ShapeDtypeStruct((M, D), F32),
        compiler_params=_params("parallel"),
        name="merge",
    )(oa, ob, z, z, x, wpa, wpb, wo)


def _xattn_kernel(q_ref, mk_ref, mv_ref, o_ref):
    hd = q_ref.shape[1] // MX_HEADS
    scale = hd ** -0.5
    for h in range(MX_HEADS):
        sl = slice(h * hd, (h + 1) * hd)
        q = (q_ref[:, sl] * scale).astype(BF16)
        k = mk_ref[0, :, h, :].astype(BF16)
        v = mv_ref[0, :, h, :].astype(BF16)
        s = lax.dot_general(q, k, (((1,), (1,)), ((), ())), preferred_element_type=F32)
        p = jnp.exp(s - jnp.max(s, axis=1, keepdims=True))
        o = jnp.dot(p.astype(BF16), v, preferred_element_type=F32)
        o_ref[:, sl] = o / jnp.sum(p, axis=1, keepdims=True)


def _xattn(q, mk, mv, B, L, tm=512):
    D = q.shape[1]
    mem, hd = mk.shape[1], mk.shape[3]
    tm = min(tm, L)
    nt = L // tm
    return pl.pallas_call(
        _xattn_kernel,
        grid=(B, nt),
        in_specs=[pl.BlockSpec((tm, D), lambda b, t: (b * nt + t, 0)),
                  pl.BlockSpec((1, mem, MX_HEADS, hd), lambda b, t: (b, 0, 0, 0)),
                  pl.BlockSpec((1, mem, MX_HEADS, hd), lambda b, t: (b, 0, 0, 0))],
        out_specs=pl.BlockSpec((tm, D), lambda b, t: (b * nt + t, 0)),
        out_shape=jax.ShapeDtypeStruct((B * L, D), F32),
        compiler_params=_params("parallel", "parallel"),
        name="xattn",
    )(q, mk, mv)


def _ffn_kernel(x_ref, g_ref, wg_ref, wu_ref, wd_ref, fg_ref, o_ref, *, tf):
    x = x_ref[...]
    xn = _rms(x, g_ref[...]).astype(BF16)
    acc = x
    for c in range(wg_ref.shape[1] // tf):
        sl = slice(c * tf, (c + 1) * tf)
        a = jnp.dot(xn, wg_ref[:, sl], preferred_element_type=F32)
        u = jnp.dot(xn, wu_ref[:, sl], preferred_element_type=F32)
        acc = acc + jnp.dot((_silu(a) * u).astype(BF16), wd_ref[sl, :], preferred_element_type=F32)
    o_ref[...] = _rms(acc, fg_ref[...])


def _ffn(x, g, wg, wu, wd, fg, tm=256, tf=256):
    M, D = x.shape
    FF = wg.shape[1]
    tm = min(tm, M)
    row = lambda i: (i, 0)
    const = lambda i: (0, 0)
    return pl.pallas_call(
        functools.partial(_ffn_kernel, tf=tf),
        grid=(M // tm,),
        in_specs=[pl.BlockSpec((tm, D), row), pl.BlockSpec((1, D), const),
                  pl.BlockSpec((D, FF), const), pl.BlockSpec((D, FF), const), pl.BlockSpec((FF, D), const),
                  pl.BlockSpec((1, D), const)],
        out_specs=pl.BlockSpec((tm, D), row),
        out_shape=jax.ShapeDtypeStruct((M, D), F32),
        compiler_params=_params("parallel"),
        name="ffn",
    )(x, g.reshape(1, D), wg, wu, wd, fg.reshape(1, D))


_Z_GATES = 5
_Z_Q = 4
_Z_K128 = 56
_Z_V128 = 58
_Z_K256 = 28
_Z_V256 = 29


def _group(x2, B, L, chunk, s0, moba_fn, W):
    z = _matmul(x2, W["w_in"], g=W["norm_mix_g"], tm=min(1024, B * L), tn=512)
    oa, S = _gla(z, W["lb_logits"], W["hg_norm_g"], s0, B, L, chunk)
    ob = moba_fn(z)
    h = _merge(oa, ob, z, _Z_GATES, x2, W["w_pa"], W["w_pb"], W["w_out"])
    return z, h, S


def kernel(x_prompt, x_sample, cache_k, cache_v, cache_mem_k, cache_mem_v, state_hgrn, page_table, mem_prompt, norm_mix_g, w_in, hg_lb_logits, hg_norm_g, w_proj_a, w_proj_b, w_out, rel_bias, norm_x_g, w_xq, w_mk, w_mv, w_xo, norm_ffn_g, w_gate, w_up, w_down, final_norm_g):
    Bp, Lp, D = x_prompt.shape
    DB, T, _ = x_sample.shape
    l = 0
    win = w_in[l]
    c0 = 4 * HG_HEADS * HG_DK + MB_HEADS * MB_HD
    c1 = c0 + 2 * MB_KV_HEADS * MB_HD
    W = {
        "w_in": jnp.concatenate([win[:, :c0], win[:, c1:], win[:, c0:c1]], axis=1).astype(BF16),
        "norm_mix_g": norm_mix_g[l],
        "lb_logits": hg_lb_logits,
        "hg_norm_g": hg_norm_g[l],
        "w_pa": w_proj_a[l].astype(BF16), "w_pb": w_proj_b[l].astype(BF16), "w_out": w_out[l].astype(BF16),
    }
    wxq, wxo = w_xq[l].astype(BF16), w_xo[l].astype(BF16)
    wmkv = jnp.concatenate([w_mk[l], w_mv[l]], axis=1).astype(BF16)
    wg, wu, wd = w_gate[l].astype(BF16), w_up[l].astype(BF16), w_down[l].astype(BF16)

    bias_own, bias_prev = _bias_tiles(rel_bias)

    xp = x_prompt.reshape(Bp * Lp, D)

    def moba_p(z):
        ka, vb, means = _moba_prep(z, Lp, _Z_K128, _Z_V128)
        return _moba_prompt(z, Lp, 2 * _Z_Q, ka, vb, means, bias_own, bias_prev)

    s0 = jnp.zeros((Bp, HG_HEADS, HG_DK, HG_DV), F32)
    zp, hp, Sp = _group(xp, Bp, Lp, math.gcd(Lp, 64), s0, moba_p, W)
    kw = MB_KV_HEADS * MB_HD
    kp = zp[:, _Z_K256 * kw:(_Z_K256 + 1) * kw].reshape(Bp, Lp, MB_KV_HEADS, MB_HD)
    vp = zp[:, _Z_V256 * kw:(_Z_V256 + 1) * kw].reshape(Bp, Lp, MB_KV_HEADS, MB_HD)

    mem = mem_prompt.reshape(Bp * mem_prompt.shape[1], D)
    mkv = _matmul(mem, wmkv, tm=256, tn=512)
    mshape = (Bp, mem_prompt.shape[1], MX_HEADS, D // MX_HEADS)
    mkp, mvp = mkv[:, :D].reshape(mshape), mkv[:, D:].reshape(mshape)
    qx = _matmul(hp, wxq, g=norm_x_g[l], tm=1024, tn=512)
    ox = _xattn(qx, mkp, mvp, Bp, Lp)
    hp = _matmul(ox, wxo, res=hp, tm=1024, tn=512)
    yp = _ffn(hp, norm_ffn_g[l], wg, wu, wd, final_norm_g)

    xs = x_sample.reshape(DB * T, D)

    def moba_s(z):
        return _moba_sample(z, page_table, cache_k[l], cache_v[l], bias_own, bias_prev, DB, T,
                            _Z_Q, _Z_K256, _Z_V256)

    zs, hs, Ss = _group(xs, DB, T, T, state_hgrn[l], moba_s, W)
    ksn = zs[:, _Z_K256 * kw:(_Z_K256 + 1) * kw].reshape(DB, T, MB_KV_HEADS, MB_HD)
    vsn = zs[:, _Z_V256 * kw:(_Z_V256 + 1) * kw].reshape(DB, T, MB_KV_HEADS, MB_HD)
    qxs = _matmul(hs, wxq, g=norm_x_g[l], tn=512)
    oxs = _xattn(qxs, cache_mem_k[l], cache_mem_v[l], DB, T)
    hs = _matmul(oxs, wxo, res=hs, tn=512)
    ys = _ffn(hs, norm_ffn_g[l], wg, wu, wd, final_norm_g)

    return (yp.reshape(Bp, Lp, D), ys.reshape(DB, T, D),
            Sp[None].astype(state_hgrn.dtype), kp[None], vp[None],
            mkp[None], mvp[None],
            Ss[None].astype(state_hgrn.dtype), ksn[None], vsn[None])
```

```python
import functools
import math

import jax
import jax.numpy as jnp
import numpy as np
from jax import lax
from jax.experimental import pallas as pl
from jax.experimental.pallas import tpu as pltpu

F32 = jnp.float32
BF16 = jnp.bfloat16

EPS = 1e-6
HG_HEADS = 8
HG_DK = 128
HG_DV = 128
HG_SUB = 16
MB_HEADS = 8
MB_KV_HEADS = 2
MB_GROUP = MB_HEADS // MB_KV_HEADS
MB_HD = 128
MB_BLOCK = 256
MB_TOPK = 3
RB_BUCKETS = 32
RB_MAX_DIST = 128
MX_HEADS = 4
PAGE_SIZE = 128
NEG = -1e30
LOG2E = math.log2(math.e)

_ZA_HQ, _ZA_HI, _ZA_HG, _ZA_MQ, _ZA_GA = 0, 1, 2, 3, 4
_ZF_HF = 0
_ZF_K, _ZF_V = 4, 5
LANES = 128
VMEM_LIMIT = 56 * 1024 * 1024


def _params(*sem):
    return pltpu.CompilerParams(dimension_semantics=sem, vmem_limit_bytes=VMEM_LIMIT)


def _sigmoid(x):
    return 1.0 / (1.0 + jnp.exp(-x))


def _silu(x):
    return x * _sigmoid(x)


def _rms(x, g):
    return x * lax.rsqrt(jnp.mean(x * x, axis=-1, keepdims=True) + EPS) * g


def _matmul_kernel(*refs, norm, residual):
    refs = list(refs)
    x_ref = refs.pop(0)
    g_ref = refs.pop(0) if norm else None
    w_ref = refs.pop(0)
    r_ref = refs.pop(0) if residual else None
    o_ref, xs_ref = refs

    @pl.when(pl.program_id(1) == 0)
    def _():
        x = x_ref[...]
        if norm:
            x = _rms(x, g_ref[...])
        xs_ref[...] = x.astype(BF16)

    acc = jnp.dot(xs_ref[...], w_ref[...], preferred_element_type=F32)
    if residual:
        acc = acc + r_ref[...]
    o_ref[...] = acc.astype(o_ref.dtype)


def _matmul(x, w, g=None, res=None, tm=512, tn=512, out_dtype=F32):
    M, K = x.shape
    N = w.shape[1]
    tm = min(tm, M)
    tn = min(tn, N)
    assert M % tm == 0 and N % tn == 0
    args = [x]
    specs = [pl.BlockSpec((tm, K), lambda i, j: (i, 0))]
    if g is not None:
        args.append(g.reshape(1, K))
        specs.append(pl.BlockSpec((1, K), lambda i, j: (0, 0)))
    args.append(w)
    specs.append(pl.BlockSpec((K, tn), lambda i, j: (0, j)))
    if res is not None:
        args.append(res)
        specs.append(pl.BlockSpec((tm, tn), lambda i, j: (i, j)))
    return pl.pallas_call(
        functools.partial(_matmul_kernel, norm=g is not None, residual=res is not None),
        grid=(M // tm, N // tn),
        in_specs=specs,
        out_specs=pl.BlockSpec((tm, tn), lambda i, j: (i, j)),
        out_shape=jax.ShapeDtypeStruct((M, N), out_dtype),
        scratch_shapes=[pltpu.VMEM((tm, K), BF16)],
        compiler_params=_params("parallel", "arbitrary"),
        name="matmul",
    )(*args)


def _subblock_cumsum(g, sub):
    row = lax.broadcasted_iota(jnp.int32, g.shape, 0) % sub
    s = 1
    while s < sub:
        g = g + jnp.where(row >= s, pltpu.roll(g, s, 0), 0.0)
        s *= 2
    return g


def _gla_kernel(q_ref, f_ref, i_ref, g_ref, lbl_ref, ng_ref, s0_ref, o_ref, so_ref, st_ref, *, C, sub):
    c = pl.program_id(1)
    nsub = C // sub
    mid = sub // 2 - 1

    @pl.when(c == 0)
    def _():
        for h in range(HG_HEADS):
            st_ref[h] = s0_ref[0, h].T

    lbl = lbl_ref[...]
    e = jnp.exp(lbl - jnp.max(lbl, axis=0, keepdims=True))
    lb_all = e[0:1] / jnp.sum(e, axis=0, keepdims=True)
    ng = ng_ref[...]

    row = lax.broadcasted_iota(jnp.int32, (C, C), 0)
    col = lax.broadcasted_iota(jnp.int32, (C, C), 1)
    diag_mask = (row // sub == col // sub) & (row >= col)

    W = HG_HEADS * HG_DK
    q3 = q_ref[...].astype(F32).reshape(nsub, sub, W)
    f = lb_all + (1.0 - lb_all) * _sigmoid(f_ref[...])
    k3 = (1.0 - f).reshape(nsub, sub, W)
    cs = _subblock_cumsum(jnp.log(f), sub).reshape(nsub, sub, W)
    v_all = i_ref[...].astype(BF16)
    silu_g = _silu(g_ref[...].astype(F32))

    T = cs[:, sub - 1:sub, :]
    m = cs[:, mid:mid + 1, :]
    qt = q3 * jnp.exp(cs)
    kh = k3 * jnp.exp(T - cs)
    qd_all = (q3 * jnp.exp(cs - m)).reshape(C, W).astype(BF16)
    kd_all = (k3 * jnp.exp(m - cs)).reshape(C, W).astype(BF16)

    PT = [jnp.zeros((1, W), F32)]
    for I in range(nsub):
        PT.append(PT[-1] + T[I])

    def scaled_k(I):
        parts = []
        for J in range(nsub):
            if J < I:
                parts.append(kh[J] * jnp.exp(PT[I] - PT[J + 1]))
            else:
                parts.append(jnp.zeros((sub, W), F32))
        return jnp.concatenate(parts, axis=0).astype(BF16) if nsub > 1 else parts[0].astype(BF16)

    qt_b = [qt[I].astype(BF16) for I in range(nsub)]
    ks_all = [scaled_k(I) for I in range(1, nsub + 1)]
    qc_all = jnp.concatenate([qt[I] * jnp.exp(PT[I]) for I in range(nsub)], axis=0).astype(BF16)
    decay_all = jnp.exp(PT[nsub])

    for h in range(HG_HEADS):
        sl = slice(h * HG_DK, (h + 1) * HG_DK)
        v = v_all[:, sl]
        att = jnp.where(diag_mask,
                        lax.dot_general(qd_all[:, sl], kd_all[:, sl], (((1,), (1,)), ((), ())),
                                        preferred_element_type=F32),
                        0.0)
        if nsub > 1:
            rows = [jnp.zeros((sub, C), F32)]
            for I in range(1, nsub):
                rows.append(lax.dot_general(qt_b[I][:, sl], ks_all[I - 1][:, sl], (((1,), (1,)), ((), ())),
                                            preferred_element_type=F32))
            att = att + jnp.concatenate(rows, axis=0)

        st = st_ref[h]
        o = (jnp.dot(att.astype(BF16), v, preferred_element_type=F32)
             + lax.dot_general(qc_all[:, sl], st.astype(BF16), (((1,), (1,)), ((), ())),
                               preferred_element_type=F32))
        st_ref[h] = (st * decay_all[:, sl]
                     + lax.dot_general(v, ks_all[-1][:, sl], (((0,), (0,)), ((), ())),
                                       preferred_element_type=F32))

        o = o * lax.rsqrt(jnp.mean(o * o, axis=-1, keepdims=True) + EPS) * ng * silu_g[:, sl]
        o_ref[:, sl] = o.astype(o_ref.dtype)

    @pl.when(c == pl.num_programs(1) - 1)
    def _():
        for h in range(HG_HEADS):
            so_ref[0, h] = st_ref[h].T


def _gla(za, zf, lb_logits, norm_g, s0, B, L, C):
    nc = L // C
    sub = min(HG_SUB, C)
    W = HG_HEADS * HG_DK

    def zspec(j):
        return pl.BlockSpec((C, W), lambda b, c, j=j: (b * nc + c, j))

    return pl.pallas_call(
        functools.partial(_gla_kernel, C=C, sub=sub),
        grid=(B, nc),
        in_specs=[zspec(_ZA_HQ), zspec(_ZF_HF), zspec(_ZA_HI), zspec(_ZA_HG),
                  pl.BlockSpec((2, W), lambda b, c: (0, 0)),
                  pl.BlockSpec((1, HG_DV), lambda b, c: (0, 0)),
                  pl.BlockSpec((1, HG_HEADS, HG_DK, HG_DV), lambda b, c: (b, 0, 0, 0))],
        out_specs=[pl.BlockSpec((C, W), lambda b, c: (b * nc + c, 0)),
                   pl.BlockSpec((1, HG_HEADS, HG_DK, HG_DV), lambda b, c: (b, 0, 0, 0))],
        out_shape=[jax.ShapeDtypeStruct((B * L, W), za.dtype),
                   jax.ShapeDtypeStruct((B, HG_HEADS, HG_DK, HG_DV), F32)],
        scratch_shapes=[pltpu.VMEM((HG_HEADS, HG_DV, HG_DK), F32)],
        compiler_params=_params("arbitrary", "arbitrary"),
        name="gla",
    )(za, zf, za, za, lb_logits, norm_g.reshape(1, HG_DV), s0)


def _bucket_upper_bounds():
    max_exact = RB_BUCKETS // 2
    d = np.arange(1, 4 * RB_MAX_DIST)
    large = max_exact + (np.log(d.astype(np.float32) / max_exact) / math.log(RB_MAX_DIST / max_exact)
                         * (RB_BUCKETS - max_exact)).astype(np.int32)
    bucket = np.where(d < max_exact, d, np.minimum(large, RB_BUCKETS - 1))
    bucket = np.concatenate([[0], bucket])
    return [int(np.max(np.nonzero(bucket == b)[0])) for b in range(RB_BUCKETS - 1)]


def _bias_kernel(rb_ref, own_ref, prev_ref):
    h = pl.program_id(0)
    i = lax.broadcasted_iota(jnp.int32, (MB_BLOCK, MB_BLOCK), 0)
    j = lax.broadcasted_iota(jnp.int32, (MB_BLOCK, MB_BLOCK), 1)
    dmax = _bucket_upper_bounds()

    def table(d):
        val = jnp.full(d.shape, rb_ref[RB_BUCKETS - 1, h], F32)
        for b in range(RB_BUCKETS - 2, -1, -1):
            val = jnp.where(d <= dmax[b], rb_ref[b, h], val)
        return val

    d = i - j
    own_ref[0] = jnp.where(d >= 0, table(d), NEG)
    prev_ref[0] = table(d + MB_BLOCK)


def _bias_tiles(rel_bias):
    shp = jax.ShapeDtypeStruct((MB_HEADS, MB_BLOCK, MB_BLOCK), F32)
    spec = pl.BlockSpec((1, MB_BLOCK, MB_BLOCK), lambda h: (h, 0, 0))
    return pl.pallas_call(
        _bias_kernel,
        grid=(MB_HEADS,),
        in_specs=[pl.BlockSpec(memory_space=pltpu.SMEM)],
        out_specs=[spec, spec],
        out_shape=[shp, shp],
        compiler_params=_params("arbitrary"),
        name="bias_tiles",
    )(rel_bias)


def _top_select(gate, lane, axis=1):
    sel = jnp.zeros(gate.shape, jnp.bool_)
    lane = lane.astype(F32)
    for _ in range(MB_TOPK):
        m = jnp.max(gate, axis=axis, keepdims=True)
        idx = jnp.min(jnp.where(gate == m, lane, float(2 ** 30)), axis=axis, keepdims=True)
        hit = (lane == idx) & (m > -jnp.inf)
        sel = sel | hit
        gate = jnp.where(hit, -jnp.inf, gate)
    return sel


def _moba_prep_kernel(k_ref, v_ref, ka_ref, vb_ref, mean_ref):
    n = pl.program_id(1)
    k = k_ref[...]
    lane = lax.broadcasted_iota(jnp.int32, (MB_BLOCK, LANES), 1)
    onehot = jnp.where(lane == n, 1.0, 0.0).astype(BF16)
    ka_ref[0] = jnp.concatenate([k.astype(BF16), onehot], axis=1)
    vb_ref[0] = jnp.concatenate([v_ref[...].astype(BF16), jnp.ones((MB_BLOCK, LANES), BF16)], axis=1)
    mean_ref[0, 0] = jnp.mean(k, axis=0, keepdims=True)


def _moba_prep(z, L, kcol, vcol):
    nb = L // MB_BLOCK
    return pl.pallas_call(
        _moba_prep_kernel,
        grid=(MB_KV_HEADS, nb),
        in_specs=[pl.BlockSpec((MB_BLOCK, MB_HD), lambda g, n: (n, kcol + g)),
                  pl.BlockSpec((MB_BLOCK, MB_HD), lambda g, n: (n, vcol + g))],
        out_specs=[pl.BlockSpec((1, MB_BLOCK, 2 * MB_HD), lambda g, n: (g, n, 0)),
                   pl.BlockSpec((1, MB_BLOCK, 2 * MB_HD), lambda g, n: (g, n, 0)),
                   pl.BlockSpec((1, 1, 1, MB_HD), lambda g, n: (g, n, 0, 0))],
        out_shape=[jax.ShapeDtypeStruct((MB_KV_HEADS, L, 2 * MB_HD), BF16),
                   jax.ShapeDtypeStruct((MB_KV_HEADS, L, 2 * MB_HD), BF16),
                   jax.ShapeDtypeStruct((MB_KV_HEADS, nb, 1, MB_HD), F32)],
        compiler_params=_params("arbitrary", "arbitrary"),
        name="moba_prep",
    )(z, z)


def _moba_prompt_kernel(q_ref, ka_ref, va_ref, mean_ref, own_ref, prev_ref, o_ref,
                        qa_ref, m_ref, acc_ref, s_ref):
    i = pl.program_id(1)
    G, B = MB_GROUP, MB_BLOCK
    R = G * B
    scale = MB_HD ** -0.5
    blk = lax.broadcasted_iota(jnp.int32, (LANES, B), 0)
    means = mean_ref[0]

    for j in range(G):
        q = q_ref[:, j * MB_HD:(j + 1) * MB_HD].astype(F32)
        gate = lax.dot_general(means, q, (((1,), (1,)), ((), ())), precision=lax.Precision.HIGHEST,
                               preferred_element_type=F32)
        sel = _top_select(jnp.where(blk < i, gate, -jnp.inf), blk, axis=0)
        b31 = prev_ref[j, 0:1, 0:1]
        mb = jnp.where(sel, jnp.where(blk == i - 1, 0.0, b31), NEG)
        mb = jnp.where(blk == i, 0.0, mb).T
        qa_ref[j * B:(j + 1) * B, :] = jnp.concatenate(
            [(q * (scale * LOG2E)).astype(BF16), (mb * LOG2E).astype(BF16)], axis=1)

    def block_rows(n):
        return pl.ds(pl.multiple_of(n * B, B), B)

    def qk(n):
        return lax.dot_general(qa_ref[...], ka_ref[0, block_rows(n), :], (((1,), (1,)), ((), ())),
                               preferred_element_type=F32)

    def softmax_pv(s, n, first):
        vn = va_ref[0, block_rows(n), :]
        mx = jnp.max(s, axis=1, keepdims=True)
        if first:
            m_new = jnp.broadcast_to(mx, (R, LANES))
            p = jnp.exp2(s - mx)
            acc_ref[...] = jnp.dot(p.astype(BF16), vn, preferred_element_type=F32)
        else:
            m_prev = m_ref[...]
            m_new = jnp.maximum(m_prev, mx)
            alpha = jnp.exp2(m_prev - m_new)
            p = jnp.exp2(s - jnp.concatenate([m_new, m_new], axis=1))
            acc_ref[...] = (jnp.concatenate([alpha, alpha], axis=1) * acc_ref[...]
                            + jnp.dot(p.astype(BF16), vn, preferred_element_type=F32))
        m_ref[...] = m_new

    softmax_pv(qk(i) + own_ref[...].reshape(R, B) * LOG2E, i, True)

    @pl.when(i >= 1)
    def _():
        softmax_pv(qk(i - 1) + prev_ref[...].reshape(R, B) * LOG2E, i - 1, False)

    @pl.when(i >= 2)
    def _():
        s_ref[...] = qk(0)
        last = i - 2

        def pair(t, carry):
            n = 2 * t
            s_a = s_ref[...]
            s_b = qk(n + 1)
            softmax_pv(s_a, n, False)
            s_c = qk(jnp.minimum(n + 2, last))
            softmax_pv(s_b, n + 1, False)
            s_ref[...] = s_c
            return carry

        lax.fori_loop(0, (i - 1) // 2, pair, 0)

        @pl.when((i - 1) % 2 == 1)
        def _():
            softmax_pv(s_ref[...], last, False)

    acc = acc_ref[...]
    o = acc[:, :MB_HD] / acc[:, MB_HD:]
    for j in range(G):
        o_ref[:, j * MB_HD:(j + 1) * MB_HD] = o[j * B:(j + 1) * B].astype(o_ref.dtype)


def _moba_prompt(z, L, qcol, ka, vb, means, bias_own, bias_prev):
    nb = L // MB_BLOCK
    G, B = MB_GROUP, MB_BLOCK
    means_p = jnp.pad(means.reshape(MB_KV_HEADS, nb, MB_HD), ((0, 0), (0, LANES - nb), (0, 0)))
    return pl.pallas_call(
        _moba_prompt_kernel,
        grid=(MB_KV_HEADS, nb),
        in_specs=[pl.BlockSpec((B, G * MB_HD), lambda g, i: (i, qcol + g)),
                  pl.BlockSpec((1, L, 2 * MB_HD), lambda g, i: (g, 0, 0), pipeline_mode=pl.Buffered(1)),
                  pl.BlockSpec((1, L, 2 * MB_HD), lambda g, i: (g, 0, 0), pipeline_mode=pl.Buffered(1)),
                  pl.BlockSpec((1, LANES, MB_HD), lambda g, i: (g, 0, 0)),
                  pl.BlockSpec((G, B, B), lambda g, i: (g, 0, 0)),
                  pl.BlockSpec((G, B, B), lambda g, i: (g, 0, 0))],
        out_specs=pl.BlockSpec((B, G * MB_HD), lambda g, i: (i, g)),
        out_shape=jax.ShapeDtypeStruct((L, MB_HEADS * MB_HD), z.dtype),
        scratch_shapes=[pltpu.VMEM((G * B, 2 * MB_HD), BF16),
                        pltpu.VMEM((G * B, LANES), F32),
                        pltpu.VMEM((G * B, 2 * MB_HD), F32),
                        pltpu.VMEM((G * B, B), F32)],
        compiler_params=_params("arbitrary", "arbitrary"),
        name="moba_prompt",
    )(z, ka, vb, means_p, bias_own, bias_prev)


def _moba_sample_kernel(pt_ref, q_ref, kn_ref, vn_ref, ck_hbm, cv_hbm, own_ref, adj_ref, b31_ref, o_ref,
                        kbuf, vbuf, lg_ref, sem, *, T, n_pages, n_seq):
    b = pl.program_id(0)
    B = MB_BLOCK
    past = n_pages * PAGE_SIZE
    nb = past // B
    R = MB_HEADS * T
    half = R // MB_KV_HEADS
    scale = MB_HD ** -0.5

    KVH = MB_KV_HEADS
    prow = PAGE_SIZE * KVH

    def page_copies(seq, slot, j):
        dst = pl.ds(pl.multiple_of(j * prow, prow), prow)
        page = pt_ref[seq, j]
        return (pltpu.make_async_copy(ck_hbm.at[page], kbuf.at[slot, dst, :], sem.at[0, slot]),
                pltpu.make_async_copy(cv_hbm.at[page], vbuf.at[slot, dst, :], sem.at[1, slot]))

    def start_seq(seq, slot):
        def body(j, c):
            ck, cv = page_copies(seq, slot, j)
            ck.start()
            cv.start()
            return c
        lax.fori_loop(0, n_pages, body, 0)

    def wait_seq(seq, slot):
        def body(j, c):
            ck, cv = page_copies(seq, slot, j)
            ck.wait()
            cv.wait()
            return c
        lax.fori_loop(0, n_pages, body, 0)

    slot = b % 2

    @pl.when(b == 0)
    def _():
        start_seq(0, 0)

    @pl.when(b + 1 < n_seq)
    def _():
        start_seq(b + 1, 1 - slot)

    wait_seq(b, slot)

    def block(buf, n, g):
        return buf[slot, pl.ds(n * B * KVH + g, B, stride=KVH), :]

    q = q_ref[...]
    Q = jnp.concatenate([q[:, h * MB_HD:(h + 1) * MB_HD] for h in range(MB_HEADS)], axis=0)
    qf = Q * scale
    qs = qf.astype(BF16)
    grow = [slice(g * half, (g + 1) * half) for g in range(KVH)]

    mean_rows = [[] for _ in range(KVH)]
    for n in range(nb):
        for g in range(KVH):
            kf = block(kbuf, n, g)
            mean_rows[g].append(jnp.mean(kf, axis=0, keepdims=True))
            lg_ref[grow[g], n * B:(n + 1) * B] = lax.dot_general(
                qs[grow[g]], kf.astype(BF16), (((1,), (1,)), ((), ())), preferred_element_type=F32)

    gates = []
    for g in range(KVH):
        means = jnp.concatenate(mean_rows[g] + [jnp.zeros((LANES - nb, MB_HD), F32)], axis=0)
        gates.append(lax.dot_general(Q[grow[g]], means, (((1,), (1,)), ((), ())),
                                     precision=lax.Precision.HIGHEST, preferred_element_type=F32))
    gate = jnp.concatenate(gates, axis=0)
    lane = lax.broadcasted_iota(jnp.int32, (R, LANES), 1)
    sel = _top_select(jnp.where(lane < nb, gate, -jnp.inf), lane)
    mb = jnp.where(sel, jnp.where(lane == nb - 1, 0.0, b31_ref[...]), NEG)
    adj = adj_ref[...]

    def logits(n):
        s = lg_ref[:, n * B:(n + 1) * B] + mb[:, n:n + 1]
        return s + adj if n == nb - 1 else s

    kn = kn_ref[...]
    vn = vn_ref[...]
    s_own = jnp.concatenate(
        [lax.dot_general(qf[grow[g]], kn[:, g * MB_HD:(g + 1) * MB_HD], (((1,), (1,)), ((), ())),
                         preferred_element_type=F32) for g in range(KVH)], axis=0) + own_ref[...]

    m = jnp.max(s_own, axis=1, keepdims=True)
    for n in range(nb):
        m = jnp.maximum(m, jnp.max(logits(n), axis=1, keepdims=True))

    p_own = jnp.exp(s_own - m)
    l = jnp.sum(p_own, axis=1, keepdims=True)
    acc = [jnp.dot(p_own[grow[g]], vn[:, g * MB_HD:(g + 1) * MB_HD], preferred_element_type=F32)
           for g in range(KVH)]
    for n in range(nb):
        p = jnp.exp(logits(n) - m)
        l = l + jnp.sum(p, axis=1, keepdims=True)
        pb = p.astype(BF16)
        for g in range(KVH):
            acc[g] = acc[g] + jnp.dot(pb[grow[g]], block(vbuf, n, g).astype(BF16),
                                      preferred_element_type=F32)
    rows = jnp.concatenate(acc, axis=0) / l
    o_ref[...] = jnp.concatenate([rows[h * T:(h + 1) * T] for h in range(MB_HEADS)], axis=1)


def _moba_sample(z, zf, page_table, cache_k, cache_v, bias_own, bias_prev, DB, T, qcol, kcol, vcol):
    n_pages = page_table.shape[1]
    past = n_pages * PAGE_SIZE
    R = MB_HEADS * T
    W = MB_HEADS * MB_HD
    KW = MB_KV_HEADS * MB_HD
    ck = cache_k.reshape(cache_k.shape[0], PAGE_SIZE * MB_KV_HEADS, MB_HD)
    cv = cache_v.reshape(cache_v.shape[0], PAGE_SIZE * MB_KV_HEADS, MB_HD)
    own = bias_own[:, :T, :T].reshape(R, T)
    adj = bias_prev[:, :T, :].reshape(R, MB_BLOCK)
    b31 = bias_prev[:, :T, 0:1].reshape(R, 1)
    const = lambda b, pt: (0, 0)
    grid_spec = pltpu.PrefetchScalarGridSpec(
        num_scalar_prefetch=1,
        grid=(DB,),
        in_specs=[pl.BlockSpec((T, W), lambda b, pt: (b, qcol)),
                  pl.BlockSpec((T, KW), lambda b, pt: (b, kcol)),
                  pl.BlockSpec((T, KW), lambda b, pt: (b, vcol)),
                  pl.BlockSpec(memory_space=pl.ANY),
                  pl.BlockSpec(memory_space=pl.ANY),
                  pl.BlockSpec((R, T), const),
                  pl.BlockSpec((R, MB_BLOCK), const),
                  pl.BlockSpec((R, 1), const)],
        out_specs=pl.BlockSpec((T, W), lambda b, pt: (b, 0)),
        scratch_shapes=[pltpu.VMEM((2, past * MB_KV_HEADS, MB_HD), F32),
                        pltpu.VMEM((2, past * MB_KV_HEADS, MB_HD), F32),
                        pltpu.VMEM((R, past), F32),
                        pltpu.SemaphoreType.DMA((2, 2))],
    )
    return pl.pallas_call(
        functools.partial(_moba_sample_kernel, T=T, n_pages=n_pages, n_seq=DB),
        grid_spec=grid_spec,
        out_shape=jax.ShapeDtypeStruct((DB * T, W), F32),
        compiler_params=_params("arbitrary"),
        name="moba_sample",
    )(page_table, z, zf, zf, ck, cv, own, adj, b31)


def _merge_kernel(oa_ref, ob_ref, ga_ref, gb_ref, x_ref, wpa_ref, wpb_ref, wo_ref, o_ref):
    a = jnp.dot(oa_ref[...].astype(BF16), wpa_ref[...], preferred_element_type=F32)
    b = jnp.dot(ob_ref[...].astype(BF16), wpb_ref[...], preferred_element_type=F32)
    merged = _sigmoid(ga_ref[...].astype(F32)) * a + _sigmoid(gb_ref[...].astype(F32)) * b
    o_ref[...] = x_ref[...] + jnp.dot(merged.astype(BF16), wo_ref[...], preferred_element_type=F32)


def _merge(oa, ob, z, gcol, x, wpa, wpb, wo, tm=512):
    M, D = x.shape
    tm = min(tm, M)
    row = lambda i: (i, 0)
    const = lambda i: (0, 0)
    return pl.pallas_call(
        _merge_kernel,
        grid=(M // tm,),
        in_specs=[pl.BlockSpec((tm, D), row), pl.BlockSpec((tm, D), row),
                  pl.BlockSpec((tm, D), lambda i: (i, gcol)), pl.BlockSpec((tm, D), lambda i: (i, gcol + 1)),
                  pl.BlockSpec((tm, D), row),
                  pl.BlockSpec((D, D), const), pl.BlockSpec((D, D), const), pl.BlockSpec((D, D), const)],
        out_specs=pl.BlockSpec((tm, D), row),
        out_shape=jax.ShapeDtypeStruct((M, D), F32),
        compiler_params=_params("parallel"),
        name="merge",
    )(oa, ob, z, z, x, wpa, wpb, wo)


def _xattn_kernel(q_ref, mk_ref, mv_ref, o_ref):
    hd = q_ref.shape[1] // MX_HEADS
    scale = hd ** -0.5
    for h in range(MX_HEADS):
        sl = slice(h * hd, (h + 1) * hd)
        q = (q_ref[:, sl] * scale).astype(BF16)
        k = mk_ref[0, :, h, :].astype(BF16)
        v = mv_ref[0, :, h, :].astype(BF16)
        s = lax.dot_general(q, k, (((1,), (1,)), ((), ())), preferred_element_type=F32)
        p = jnp.exp(s - jnp.max(s, axis=1, keepdims=True))
        o = jnp.dot(p.astype(BF16), v, preferred_element_type=F32)
        o_ref[:, sl] = o / jnp.sum(p, axis=1, keepdims=True)


def _xattn(q, mk, mv, B, L, tm=512):
    D = q.shape[1]
    mem, hd = mk.shape[1], mk.shape[3]
    tm = min(tm, L)
    nt = L // tm
    return pl.pallas_call(
        _xattn_kernel,
        grid=(B, nt),
        in_specs=[pl.BlockSpec((tm, D), lambda b, t: (b * nt + t, 0)),
                  pl.BlockSpec((1, mem, MX_HEADS, hd), lambda b, t: (b, 0, 0, 0)),
                  pl.BlockSpec((1, mem, MX_HEADS, hd), lambda b, t: (b, 0, 0, 0))],
        out_specs=pl.BlockSpec((tm, D), lambda b, t: (b * nt + t, 0)),
        out_shape=jax.ShapeDtypeStruct((B * L, D), F32),
        compiler_params=_params("parallel", "parallel"),
        name="xattn",
    )(q, mk, mv)


def _xblock_kernel(x_ref, g_ref, wq_ref, k_ref, v_ref, wo_ref, o_ref):
    x = x_ref[...]
    q = jnp.dot(_rms(x, g_ref[...]).astype(BF16), wq_ref[...], preferred_element_type=F32)
    hd = k_ref.shape[2]
    scale = hd ** -0.5
    outs = []
    for h in range(MX_HEADS):
        qh = (q[:, h * hd:(h + 1) * hd] * scale).astype(BF16)
        s = lax.dot_general(qh, k_ref[h], (((1,), (1,)), ((), ())), preferred_element_type=F32)
        p = jnp.exp(s - jnp.max(s, axis=1, keepdims=True))
        o = jnp.dot(p.astype(BF16), v_ref[h], preferred_element_type=F32)
        outs.append((o / jnp.sum(p, axis=1, keepdims=True)).astype(BF16))
    o_ref[...] = x + jnp.dot(jnp.concatenate(outs, axis=1), wo_ref[...], preferred_element_type=F32)


def _xblock(x, g, wq, mk, mv, wo, tm=512):
    M, D = x.shape
    tm = min(tm, M)
    row = lambda i: (i, 0)
    const2 = lambda i: (0, 0)
    const3 = lambda i: (0, 0, 0)
    return pl.pallas_call(
        _xblock_kernel,
        grid=(M // tm,),
        in_specs=[pl.BlockSpec((tm, D), row), pl.BlockSpec((1, D), const2), pl.BlockSpec((D, D), const2),
                  pl.BlockSpec(mk.shape, const3), pl.BlockSpec(mv.shape, const3), pl.BlockSpec((D, D), const2)],
        out_specs=pl.BlockSpec((tm, D), row),
        out_shape=jax.ShapeDtypeStruct((M, D), F32),
        compiler_params=_params("parallel"),
        name="xblock",
    )(x, g.reshape(1, D), wq, mk, mv, wo)


def _ffn_kernel(x_ref, g_ref, wg_ref, wu_ref, wd_ref, fg_ref, o_ref, *, tf):
    x = x_ref[...]
    xn = _rms(x, g_ref[...]).astype(BF16)
    acc = x
    for c in range(wg_ref.shape[1] // tf):
        sl = slice(c * tf, (c + 1) * tf)
        a = jnp.dot(xn, wg_ref[:, sl], preferred_element_type=F32)
        u = jnp.dot(xn, wu_ref[:, sl], preferred_element_type=F32)
        acc = acc + jnp.dot((_silu(a) * u).astype(BF16), wd_ref[sl, :], preferred_element_type=F32)
    o_ref[...] = _rms(acc, fg_ref[...])


def _ffn(x, g, wg, wu, wd, fg, tm=1024, tf=256):
    M, D = x.shape
    FF = wg.shape[1]
    tm = min(tm, M)
    row = lambda i: (i, 0)
    const = lambda i: (0, 0)
    once = pl.Buffered(1)
    return pl.pallas_call(
        functools.partial(_ffn_kernel, tf=tf),
        grid=(M // tm,),
        in_specs=[pl.BlockSpec((tm, D), row), pl.BlockSpec((1, D), const),
                  pl.BlockSpec((D, FF), const, pipeline_mode=once),
                  pl.BlockSpec((D, FF), const, pipeline_mode=once),
                  pl.BlockSpec((FF, D), const, pipeline_mode=once),
                  pl.BlockSpec((1, D), const)],
        out_specs=pl.BlockSpec((tm, D), row),
        out_shape=jax.ShapeDtypeStruct((M, D), F32),
        compiler_params=_params("parallel"),
        name="ffn",
    )(x, g.reshape(1, D), wg, wu, wd, fg.reshape(1, D))


def _mixer(x2, B, L, chunk, s0, moba_fn, W, za_dtype):
    tm = min(2048, B * L)
    za = _matmul(x2, W["w_in_a"], g=W["norm_mix_g"], tm=tm, tn=512, out_dtype=za_dtype)
    zf = _matmul(x2, W["w_in_f"], g=W["norm_mix_g"], tm=tm, tn=512)
    oa, S = _gla(za, zf, W["lb_logits"], W["hg_norm_g"], s0, B, L, chunk)
    ob = moba_fn(za, zf)
    h = _merge(oa, ob, za, _ZA_GA, x2, W["w_pa"], W["w_pb"], W["w_out"])
    return h, S, zf


def kernel(x_prompt, x_sample, cache_k, cache_v, cache_mem_k, cache_mem_v, state_hgrn, page_table, mem_prompt, norm_mix_g, w_in, hg_lb_logits, hg_norm_g, w_proj_a, w_proj_b, w_out, rel_bias, norm_x_g, w_xq, w_mk, w_mv, w_xo, norm_ffn_g, w_gate, w_up, w_down, final_norm_g):
    Bp, Lp, D = x_prompt.shape
    DB, T, _ = x_sample.shape
    l = 0
    win = w_in[l]
    hw = HG_HEADS * HG_DK
    hq, hf, hi, hg = (win[:, j * hw:(j + 1) * hw] for j in range(4))
    c0 = 4 * hw + MB_HEADS * MB_HD
    c1 = c0 + 2 * MB_KV_HEADS * MB_HD
    W = {
        "w_in_a": jnp.concatenate([hq, hi, hg, win[:, 4 * hw:c0], win[:, c1:]], axis=1).astype(BF16),
        "w_in_f": jnp.concatenate([hf, win[:, c0:c1]], axis=1).astype(BF16),
        "norm_mix_g": norm_mix_g[l],
        "lb_logits": hg_lb_logits,
        "hg_norm_g": hg_norm_g[l],
        "w_pa": w_proj_a[l].astype(BF16), "w_pb": w_proj_b[l].astype(BF16), "w_out": w_out[l].astype(BF16),
    }
    wxq, wxo = w_xq[l].astype(BF16), w_xo[l].astype(BF16)
    wmkv = jnp.concatenate([w_mk[l], w_mv[l]], axis=1).astype(BF16)
    wg, wu, wd = w_gate[l].astype(BF16), w_up[l].astype(BF16), w_down[l].astype(BF16)

    bias_own, bias_prev = _bias_tiles(rel_bias)

    xp = x_prompt.reshape(Bp * Lp, D)

    assert Bp == 1, "the prompt kernels take one sequence"
    kw = MB_KV_HEADS * MB_HD

    def new_kv(zf, B, L):
        return (zf[:, _ZF_K * kw:(_ZF_K + 1) * kw].reshape(B, L, MB_KV_HEADS, MB_HD),
                zf[:, _ZF_V * kw:(_ZF_V + 1) * kw].reshape(B, L, MB_KV_HEADS, MB_HD))

    def moba_p(za, zf):
        ka, va, means = _moba_prep(zf, Lp, 2 * _ZF_K, 2 * _ZF_V)
        return _moba_prompt(za, Lp, 2 * _ZA_MQ, ka, va, means, bias_own, bias_prev)

    s0 = jnp.zeros((Bp, HG_HEADS, HG_DK, HG_DV), F32)
    hp, Sp, zfp = _mixer(xp, Bp, Lp, math.gcd(Lp, 64), s0, moba_p, W, BF16)
    kp, vp = new_kv(zfp, Bp, Lp)

    mem = mem_prompt.reshape(Bp * mem_prompt.shape[1], D)
    mkv = _matmul(mem, wmkv, tm=256, tn=512)
    mshape = (Bp, mem_prompt.shape[1], MX_HEADS, D // MX_HEADS)
    mkp, mvp = mkv[:, :D].reshape(mshape), mkv[:, D:].reshape(mshape)
    hp = _xblock(hp, norm_x_g[l], wxq, mkp[0].swapaxes(0, 1).astype(BF16), mvp[0].swapaxes(0, 1).astype(BF16),
                 wxo)
    yp = _ffn(hp, norm_ffn_g[l], wg, wu, wd, final_norm_g)

    xs = x_sample.reshape(DB * T, D)

    def moba_s(za, zf):
        return _moba_sample(za, zf, page_table, cache_k[l], cache_v[l], bias_own, bias_prev, DB, T,
                            _ZA_MQ, _ZF_K, _ZF_V)

    hs, Ss, zfs = _mixer(xs, DB, T, T, state_hgrn[l], moba_s, W, F32)
    ksn, vsn = new_kv(zfs, DB, T)
    qxs = _matmul(hs, wxq, g=norm_x_g[l], tn=512)
    oxs = _xattn(qxs, cache_mem_k[l], cache_mem_v[l], DB, T)
    hs = _matmul(oxs, wxo, res=hs, tn=512)
    ys = _ffn(hs, norm_ffn_g[l], wg, wu, wd, final_norm_g)

    return (yp.reshape(Bp, Lp, D), ys.reshape(DB, T, D),
            Sp[None].astype(state_hgrn.dtype), kp[None], vp[None],
            mkp[None], mvp[None],
            Ss[None].astype(state_hgrn.dtype), ksn[None], vsn[None])
```

```python
import functools
import math

import jax
import jax.numpy as jnp
import numpy as np
from jax import lax
from jax.experimental import pallas as pl
from jax.experimental.pallas import tpu as pltpu

F32 = jnp.float32
BF16 = jnp.bfloat16

EPS = 1e-6
HG_HEADS = 8
HG_DK = 128
HG_DV = 128
HG_SUB = 16
MB_HEADS = 8
MB_KV_HEADS = 2
MB_GROUP = MB_HEADS // MB_KV_HEADS
MB_HD = 128
MB_BLOCK = 256
MB_TOPK = 3
MB_UNROLL = 4
RB_BUCKETS = 32
RB_MAX_DIST = 128
MX_HEADS = 4
PAGE_SIZE = 128
NEG = -1e30
LOG2E = math.log2(math.e)

_ZA_HQ, _ZA_HI, _ZA_HG, _ZA_MQ, _ZA_GA = 0, 1, 2, 3, 4
_ZF_HF = 0
_ZF_K, _ZF_V = 4, 5
LANES = 128
VMEM_LIMIT = 56 * 1024 * 1024


def _params(*sem):
    return pltpu.CompilerParams(dimension_semantics=sem, vmem_limit_bytes=VMEM_LIMIT)


def _sigmoid(x):
    return 1.0 / (1.0 + jnp.exp(-x))


def _silu(x):
    return x * _sigmoid(x)


def _rms(x, g):
    return x * lax.rsqrt(jnp.mean(x * x, axis=-1, keepdims=True) + EPS) * g


def _matmul_kernel(*refs, norm, residual):
    refs = list(refs)
    x_ref = refs.pop(0)
    g_ref = refs.pop(0) if norm else None
    w_ref = refs.pop(0)
    r_ref = refs.pop(0) if residual else None
    o_ref, xs_ref = refs

    @pl.when(pl.program_id(1) == 0)
    def _():
        x = x_ref[...]
        if norm:
            x = _rms(x, g_ref[...])
        xs_ref[...] = x.astype(BF16)

    acc = jnp.dot(xs_ref[...], w_ref[...], preferred_element_type=F32)
    if residual:
        acc = acc + r_ref[...]
    o_ref[...] = acc.astype(o_ref.dtype)


def _matmul(x, w, g=None, res=None, tm=512, tn=512, out_dtype=F32):
    M, K = x.shape
    N = w.shape[1]
    tm = min(tm, M)
    tn = min(tn, N)
    assert M % tm == 0 and N % tn == 0
    args = [x]
    specs = [pl.BlockSpec((tm, K), lambda i, j: (i, 0))]
    if g is not None:
        args.append(g.reshape(1, K))
        specs.append(pl.BlockSpec((1, K), lambda i, j: (0, 0)))
    args.append(w)
    specs.append(pl.BlockSpec((K, tn), lambda i, j: (0, j)))
    if res is not None:
        args.append(res)
        specs.append(pl.BlockSpec((tm, tn), lambda i, j: (i, j)))
    return pl.pallas_call(
        functools.partial(_matmul_kernel, norm=g is not None, residual=res is not None),
        grid=(M // tm, N // tn),
        in_specs=specs,
        out_specs=pl.BlockSpec((tm, tn), lambda i, j: (i, j)),
        out_shape=jax.ShapeDtypeStruct((M, N), out_dtype),
        scratch_shapes=[pltpu.VMEM((tm, K), BF16)],
        compiler_params=_params("parallel", "arbitrary"),
        name="matmul",
    )(*args)


def _subblock_cumsum(g, sub):
    row = lax.broadcasted_iota(jnp.int32, g.shape, 0) % sub
    s = 1
    while s < sub:
        g = g + jnp.where(row >= s, pltpu.roll(g, s, 0), 0.0)
        s *= 2
    return g


def _gla_kernel(q_ref, f_ref, i_ref, g_ref, lbl_ref, ng_ref, s0_ref, o_ref, so_ref, st_ref, *, C, sub):
    c = pl.program_id(1)
    nsub = C // sub
    mid = sub // 2 - 1

    @pl.when(c == 0)
    def _():
        for h in range(HG_HEADS):
            st_ref[h] = s0_ref[0, h].T

    lbl = lbl_ref[...]
    e = jnp.exp(lbl - jnp.max(lbl, axis=0, keepdims=True))
    lb_all = e[0:1] / jnp.sum(e, axis=0, keepdims=True)
    ng = ng_ref[...]

    row = lax.broadcasted_iota(jnp.int32, (C, C), 0)
    col = lax.broadcasted_iota(jnp.int32, (C, C), 1)
    diag_mask = (row // sub == col // sub) & (row >= col)

    W = HG_HEADS * HG_DK
    q3 = q_ref[...].astype(F32).reshape(nsub, sub, W)
    f = lb_all + (1.0 - lb_all) * _sigmoid(f_ref[...])
    k3 = (1.0 - f).reshape(nsub, sub, W)
    cs = _subblock_cumsum(jnp.log(f), sub).reshape(nsub, sub, W)
    v_all = i_ref[...].astype(BF16)
    silu_g = _silu(g_ref[...].astype(F32))

    T = cs[:, sub - 1:sub, :]
    m = cs[:, mid:mid + 1, :]
    qd = q3 * jnp.exp(cs - m)
    kd = k3 * jnp.exp(m - cs)
    qt = qd * jnp.exp(m)
    kh = kd * jnp.exp(T - m)
    qd_all = qd.reshape(C, W).astype(BF16)
    kd_all = kd.reshape(C, W).astype(BF16)

    PT = [jnp.zeros((1, W), F32)]
    for I in range(nsub):
        PT.append(PT[-1] + T[I])

    def scaled_k(I):
        parts = []
        for J in range(nsub):
            if J < I:
                parts.append(kh[J] * jnp.exp(PT[I] - PT[J + 1]))
            else:
                parts.append(jnp.zeros((sub, W), F32))
        return jnp.concatenate(parts, axis=0).astype(BF16) if nsub > 1 else parts[0].astype(BF16)

    qt_b = [qt[I].astype(BF16) for I in range(nsub)]
    ks_all = [scaled_k(I) for I in range(1, nsub + 1)]
    qc_all = jnp.concatenate([qt[I] * jnp.exp(PT[I]) for I in range(nsub)], axis=0).astype(BF16)
    decay_all = jnp.exp(PT[nsub])

    sls = [slice(h * HG_DK, (h + 1) * HG_DK) for h in range(HG_HEADS)]
    nt = (((1,), (1,)), ((), ()))
    atts = []
    for sl in sls:
        att = jnp.where(diag_mask,
                        lax.dot_general(qd_all[:, sl], kd_all[:, sl], nt, preferred_element_type=F32), 0.0)
        if nsub > 1:
            rows = [jnp.zeros((sub, C), F32)]
            for I in range(1, nsub):
                rows.append(lax.dot_general(qt_b[I][:, sl], ks_all[I - 1][:, sl], nt,
                                            preferred_element_type=F32))
            att = att + jnp.concatenate(rows, axis=0)
        atts.append(att.astype(BF16))

    sts = [st_ref[h] for h in range(HG_HEADS)]
    outs = [jnp.dot(atts[h], v_all[:, sl], preferred_element_type=F32)
            + lax.dot_general(qc_all[:, sl], sts[h].astype(BF16), nt, preferred_element_type=F32)
            for h, sl in enumerate(sls)]
    for h, sl in enumerate(sls):
        st_ref[h] = (sts[h] * decay_all[:, sl]
                     + lax.dot_general(v_all[:, sl], ks_all[-1][:, sl], (((0,), (0,)), ((), ())),
                                       preferred_element_type=F32))
    for h, sl in enumerate(sls):
        o = outs[h]
        o = o * lax.rsqrt(jnp.mean(o * o, axis=-1, keepdims=True) + EPS) * ng * silu_g[:, sl]
        o_ref[:, sl] = o.astype(o_ref.dtype)

    @pl.when(c == pl.num_programs(1) - 1)
    def _():
        for h in range(HG_HEADS):
            so_ref[0, h] = st_ref[h].T


def _gla(za, zf, lb_logits, norm_g, s0, B, L, C):
    nc = L // C
    sub = min(HG_SUB, C)
    W = HG_HEADS * HG_DK

    def zspec(j):
        return pl.BlockSpec((C, W), lambda b, c, j=j: (b * nc + c, j))

    return pl.pallas_call(
        functools.partial(_gla_kernel, C=C, sub=sub),
        grid=(B, nc),
        in_specs=[zspec(_ZA_HQ), zspec(_ZF_HF), zspec(_ZA_HI), zspec(_ZA_HG),
                  pl.BlockSpec((2, W), lambda b, c: (0, 0)),
                  pl.BlockSpec((1, HG_DV), lambda b, c: (0, 0)),
                  pl.BlockSpec((1, HG_HEADS, HG_DK, HG_DV), lambda b, c: (b, 0, 0, 0))],
        out_specs=[pl.BlockSpec((C, W), lambda b, c: (b * nc + c, 0)),
                   pl.BlockSpec((1, HG_HEADS, HG_DK, HG_DV), lambda b, c: (b, 0, 0, 0))],
        out_shape=[jax.ShapeDtypeStruct((B * L, W), za.dtype),
                   jax.ShapeDtypeStruct((B, HG_HEADS, HG_DK, HG_DV), F32)],
        scratch_shapes=[pltpu.VMEM((HG_HEADS, HG_DV, HG_DK), F32)],
        compiler_params=_params("arbitrary", "arbitrary"),
        name="gla",
    )(za, zf, za, za, lb_logits, norm_g.reshape(1, HG_DV), s0)


def _bucket_upper_bounds():
    max_exact = RB_BUCKETS // 2
    d = np.arange(1, 4 * RB_MAX_DIST)
    large = max_exact + (np.log(d.astype(np.float32) / max_exact) / math.log(RB_MAX_DIST / max_exact)
                         * (RB_BUCKETS - max_exact)).astype(np.int32)
    bucket = np.where(d < max_exact, d, np.minimum(large, RB_BUCKETS - 1))
    bucket = np.concatenate([[0], bucket])
    return [int(np.max(np.nonzero(bucket == b)[0])) for b in range(RB_BUCKETS - 1)]


def _bias_kernel(rb_ref, own_ref, prev_ref):
    h = pl.program_id(0)
    i = lax.broadcasted_iota(jnp.int32, (MB_BLOCK, MB_BLOCK), 0)
    j = lax.broadcasted_iota(jnp.int32, (MB_BLOCK, MB_BLOCK), 1)
    dmax = _bucket_upper_bounds()

    def table(d):
        val = jnp.full(d.shape, rb_ref[RB_BUCKETS - 1, h], F32)
        for b in range(RB_BUCKETS - 2, -1, -1):
            val = jnp.where(d <= dmax[b], rb_ref[b, h], val)
        return val

    d = i - j
    own_ref[0] = jnp.where(d >= 0, table(d), NEG)
    prev_ref[0] = table(d + MB_BLOCK)


def _bias_tiles(rel_bias):
    shp = jax.ShapeDtypeStruct((MB_HEADS, MB_BLOCK, MB_BLOCK), F32)
    spec = pl.BlockSpec((1, MB_BLOCK, MB_BLOCK), lambda h: (h, 0, 0))
    return pl.pallas_call(
        _bias_kernel,
        grid=(MB_HEADS,),
        in_specs=[pl.BlockSpec(memory_space=pltpu.SMEM)],
        out_specs=[spec, spec],
        out_shape=[shp, shp],
        compiler_params=_params("arbitrary"),
        name="bias_tiles",
    )(rel_bias)


def _top_select(gate, lane, axis=1):
    sel = jnp.zeros(gate.shape, jnp.bool_)
    lane = lane.astype(F32)
    for _ in range(MB_TOPK):
        m = jnp.max(gate, axis=axis, keepdims=True)
        idx = jnp.min(jnp.where(gate == m, lane, float(2 ** 30)), axis=axis, keepdims=True)
        hit = (lane == idx) & (m > -jnp.inf)
        sel = sel | hit
        gate = jnp.where(hit, -jnp.inf, gate)
    return sel


def _moba_prep_kernel(k_ref, v_ref, ka_ref, vb_ref, mean_ref):
    n = pl.program_id(1)
    k = k_ref[...]
    lane = lax.broadcasted_iota(jnp.int32, (MB_BLOCK, LANES), 1)
    onehot = jnp.where(lane == n, 1.0, 0.0).astype(BF16)
    ka_ref[0] = jnp.concatenate([k.astype(BF16), onehot], axis=1)
    vb_ref[0] = jnp.concatenate([v_ref[...].astype(BF16), jnp.ones((MB_BLOCK, LANES), BF16)], axis=1)
    mean_ref[0, 0] = jnp.mean(k, axis=0, keepdims=True)


def _moba_prep(z, L, kcol, vcol):
    nb = L // MB_BLOCK
    return pl.pallas_call(
        _moba_prep_kernel,
        grid=(MB_KV_HEADS, nb),
        in_specs=[pl.BlockSpec((MB_BLOCK, MB_HD), lambda g, n: (n, kcol + g)),
                  pl.BlockSpec((MB_BLOCK, MB_HD), lambda g, n: (n, vcol + g))],
        out_specs=[pl.BlockSpec((1, MB_BLOCK, 2 * MB_HD), lambda g, n: (g, n, 0)),
                   pl.BlockSpec((1, MB_BLOCK, 2 * MB_HD), lambda g, n: (g, n, 0)),
                   pl.BlockSpec((1, 1, 1, MB_HD), lambda g, n: (g, n, 0, 0))],
        out_shape=[jax.ShapeDtypeStruct((MB_KV_HEADS, L, 2 * MB_HD), BF16),
                   jax.ShapeDtypeStruct((MB_KV_HEADS, L, 2 * MB_HD), BF16),
                   jax.ShapeDtypeStruct((MB_KV_HEADS, nb, 1, MB_HD), F32)],
        compiler_params=_params("arbitrary", "arbitrary"),
        name="moba_prep",
    )(z, z)


def _moba_prompt_kernel(q_ref, ka_ref, va_ref, mean_ref, own_ref, prev_ref, o_ref,
                        qa_ref, m_ref, acc_ref, s_ref):
    i = pl.program_id(1)
    G, B = MB_GROUP, MB_BLOCK
    R = G * B
    scale = MB_HD ** -0.5
    blk = lax.broadcasted_iota(jnp.int32, (LANES, B), 0)
    means = mean_ref[0]

    for j in range(G):
        q = q_ref[:, j * MB_HD:(j + 1) * MB_HD].astype(F32)
        gate = lax.dot_general(means, q, (((1,), (1,)), ((), ())), precision=lax.Precision.HIGHEST,
                               preferred_element_type=F32)
        sel = _top_select(jnp.where(blk < i, gate, -jnp.inf), blk, axis=0)
        b31 = prev_ref[j, 0:1, 0:1]
        mb = jnp.where(sel, jnp.where(blk == i - 1, 0.0, b31), NEG)
        mb = jnp.where(blk == i, 0.0, mb).T
        qa_ref[j * B:(j + 1) * B, :] = jnp.concatenate(
            [(q * (scale * LOG2E)).astype(BF16), (mb * LOG2E).astype(BF16)], axis=1)

    def block_rows(n):
        return pl.ds(pl.multiple_of(n * B, B), B)

    def qk(n):
        return lax.dot_general(qa_ref[...], ka_ref[0, block_rows(n), :], (((1,), (1,)), ((), ())),
                               preferred_element_type=F32)

    def softmax_pv(s, n, first):
        vn = va_ref[0, block_rows(n), :]
        mx = jnp.max(s, axis=1, keepdims=True)
        if first:
            m_new = jnp.broadcast_to(mx, (R, LANES))
            p = jnp.exp2(s - mx)
            acc_ref[...] = jnp.dot(p.astype(BF16), vn, preferred_element_type=F32)
        else:
            m_prev = m_ref[...]
            m_new = jnp.maximum(m_prev, mx)
            alpha = jnp.exp2(m_prev - m_new)
            p = jnp.exp2(s - jnp.concatenate([m_new, m_new], axis=1))
            acc_ref[...] = (jnp.concatenate([alpha, alpha], axis=1) * acc_ref[...]
                            + jnp.dot(p.astype(BF16), vn, preferred_element_type=F32))
        m_ref[...] = m_new

    softmax_pv(qk(i) + own_ref[...].reshape(R, B) * LOG2E, i, True)

    @pl.when(i >= 1)
    def _():
        softmax_pv(qk(i - 1) + prev_ref[...].reshape(R, B) * LOG2E, i - 1, False)

    @pl.when(i >= 2)
    def _():
        s_ref[...] = qk(0)
        last = i - 2

        def run(n0, count):
            s_cur = s_ref[...]
            for u in range(count):
                s_next = qk(jnp.minimum(n0 + u + 1, last))
                softmax_pv(s_cur, n0 + u, False)
                s_cur = s_next
            s_ref[...] = s_cur

        def unrolled(t, carry):
            run(MB_UNROLL * t, MB_UNROLL)
            return carry

        def single(n, carry):
            run(n, 1)
            return carry

        n_main = (i - 1) // MB_UNROLL
        lax.fori_loop(0, n_main, unrolled, 0)
        lax.fori_loop(n_main * MB_UNROLL, i - 1, single, 0)

    acc = acc_ref[...]
    o = acc[:, :MB_HD] / acc[:, MB_HD:]
    for j in range(G):
        o_ref[:, j * MB_HD:(j + 1) * MB_HD] = o[j * B:(j + 1) * B].astype(o_ref.dtype)


def _moba_prompt(z, L, qcol, ka, vb, means, bias_own, bias_prev):
    nb = L // MB_BLOCK
    G, B = MB_GROUP, MB_BLOCK
    means_p = jnp.pad(means.reshape(MB_KV_HEADS, nb, MB_HD), ((0, 0), (0, LANES - nb), (0, 0)))
    return pl.pallas_call(
        _moba_prompt_kernel,
        grid=(MB_KV_HEADS, nb),
        in_specs=[pl.BlockSpec((B, G * MB_HD), lambda g, i: (i, qcol + g)),
                  pl.BlockSpec((1, L, 2 * MB_HD), lambda g, i: (g, 0, 0), pipeline_mode=pl.Buffered(1)),
                  pl.BlockSpec((1, L, 2 * MB_HD), lambda g, i: (g, 0, 0), pipeline_mode=pl.Buffered(1)),
                  pl.BlockSpec((1, LANES, MB_HD), lambda g, i: (g, 0, 0)),
                  pl.BlockSpec((G, B, B), lambda g, i: (g, 0, 0)),
                  pl.BlockSpec((G, B, B), lambda g, i: (g, 0, 0))],
        out_specs=pl.BlockSpec((B, G * MB_HD), lambda g, i: (i, g)),
        out_shape=jax.ShapeDtypeStruct((L, MB_HEADS * MB_HD), z.dtype),
        scratch_shapes=[pltpu.VMEM((G * B, 2 * MB_HD), BF16),
                        pltpu.VMEM((G * B, LANES), F32),
                        pltpu.VMEM((G * B, 2 * MB_HD), F32),
                        pltpu.VMEM((G * B, B), F32)],
        compiler_params=_params("arbitrary", "arbitrary"),
        name="moba_prompt",
    )(z, ka, vb, means_p, bias_own, bias_prev)


def _moba_sample_kernel(pt_ref, q_ref, kn_ref, vn_ref, ck_hbm, cv_hbm, own_ref, adj_ref, b31_ref, o_ref,
                        kbuf, vbuf, lg_ref, sem, *, T, n_pages, n_seq):
    b = pl.program_id(0)
    B = MB_BLOCK
    past = n_pages * PAGE_SIZE
    nb = past // B
    R = MB_HEADS * T
    half = R // MB_KV_HEADS
    scale = MB_HD ** -0.5

    KVH = MB_KV_HEADS
    prow = PAGE_SIZE * KVH

    def page_copies(seq, slot, j):
        dst = pl.ds(pl.multiple_of(j * prow, prow), prow)
        page = pt_ref[seq, j]
        return (pltpu.make_async_copy(ck_hbm.at[page], kbuf.at[slot, dst, :], sem.at[0, slot]),
                pltpu.make_async_copy(cv_hbm.at[page], vbuf.at[slot, dst, :], sem.at[1, slot]))

    def start_seq(seq, slot):
        def body(j, c):
            ck, cv = page_copies(seq, slot, j)
            ck.start()
            cv.start()
            return c
        lax.fori_loop(0, n_pages, body, 0)

    def wait_seq(seq, slot):
        def body(j, c):
            ck, cv = page_copies(seq, slot, j)
            ck.wait()
            cv.wait()
            return c
        lax.fori_loop(0, n_pages, body, 0)

    slot = b % 2

    @pl.when(b == 0)
    def _():
        start_seq(0, 0)

    @pl.when(b + 1 < n_seq)
    def _():
        start_seq(b + 1, 1 - slot)

    wait_seq(b, slot)

    def block(buf, n, g):
        return buf[slot, pl.ds(n * B * KVH + g, B, stride=KVH), :]

    q = q_ref[...]
    Q = jnp.concatenate([q[:, h * MB_HD:(h + 1) * MB_HD] for h in range(MB_HEADS)], axis=0)
    qf = Q * scale
    qs = qf.astype(BF16)
    grow = [slice(g * half, (g + 1) * half) for g in range(KVH)]

    mean_rows = [[] for _ in range(KVH)]
    for n in range(nb):
        for g in range(KVH):
            kf = block(kbuf, n, g)
            mean_rows[g].append(jnp.mean(kf, axis=0, keepdims=True))
            lg_ref[grow[g], n * B:(n + 1) * B] = lax.dot_general(
                qs[grow[g]], kf.astype(BF16), (((1,), (1,)), ((), ())), preferred_element_type=F32)

    gates = []
    for g in range(KVH):
        means = jnp.concatenate(mean_rows[g] + [jnp.zeros((LANES - nb, MB_HD), F32)], axis=0)
        gates.append(lax.dot_general(Q[grow[g]], means, (((1,), (1,)), ((), ())),
                                     precision=lax.Precision.HIGHEST, preferred_element_type=F32))
    gate = jnp.concatenate(gates, axis=0)
    lane = lax.broadcasted_iota(jnp.int32, (R, LANES), 1)
    sel = _top_select(jnp.where(lane < nb, gate, -jnp.inf), lane)
    mb = jnp.where(sel, jnp.where(lane == nb - 1, 0.0, b31_ref[...]), NEG)
    adj = adj_ref[...]

    def logits(n):
        s = lg_ref[:, n * B:(n + 1) * B] + mb[:, n:n + 1]
        return s + adj if n == nb - 1 else s

    kn = kn_ref[...]
    vn = vn_ref[...]
    s_own = jnp.concatenate(
        [lax.dot_general(qf[grow[g]], kn[:, g * MB_HD:(g + 1) * MB_HD], (((1,), (1,)), ((), ())),
                         preferred_element_type=F32) for g in range(KVH)], axis=0) + own_ref[...]

    m = jnp.max(s_own, axis=1, keepdims=True)
    for n in range(nb):
        m = jnp.maximum(m, jnp.max(logits(n), axis=1, keepdims=True))

    p_own = jnp.exp(s_own - m)
    l = jnp.sum(p_own, axis=1, keepdims=True)
    acc = [jnp.dot(p_own[grow[g]], vn[:, g * MB_HD:(g + 1) * MB_HD], preferred_element_type=F32)
           for g in range(KVH)]
    for n in range(nb):
        p = jnp.exp(logits(n) - m)
        l = l + jnp.sum(p, axis=1, keepdims=True)
        pb = p.astype(BF16)
        for g in range(KVH):
            acc[g] = acc[g] + jnp.dot(pb[grow[g]], block(vbuf, n, g).astype(BF16),
                                      preferred_element_type=F32)
    rows = jnp.concatenate(acc, axis=0) / l
    o_ref[...] = jnp.concatenate([rows[h * T:(h + 1) * T] for h in range(MB_HEADS)], axis=1)


def _moba_sample(z, zf, page_table, cache_k, cache_v, bias_own, bias_prev, DB, T, qcol, kcol, vcol):
    n_pages = page_table.shape[1]
    past = n_pages * PAGE_SIZE
    R = MB_HEADS * T
    W = MB_HEADS * MB_HD
    KW = MB_KV_HEADS * MB_HD
    ck = cache_k.reshape(cache_k.shape[0], PAGE_SIZE * MB_KV_HEADS, MB_HD)
    cv = cache_v.reshape(cache_v.shape[0], PAGE_SIZE * MB_KV_HEADS, MB_HD)
    own = bias_own[:, :T, :T].reshape(R, T)
    adj = bias_prev[:, :T, :].reshape(R, MB_BLOCK)
    b31 = bias_prev[:, :T, 0:1].reshape(R, 1)
    const = lambda b, pt: (0, 0)
    grid_spec = pltpu.PrefetchScalarGridSpec(
        num_scalar_prefetch=1,
        grid=(DB,),
        in_specs=[pl.BlockSpec((T, W), lambda b, pt: (b, qcol)),
                  pl.BlockSpec((T, KW), lambda b, pt: (b, kcol)),
                  pl.BlockSpec((T, KW), lambda b, pt: (b, vcol)),
                  pl.BlockSpec(memory_space=pl.ANY),
                  pl.BlockSpec(memory_space=pl.ANY),
                  pl.BlockSpec((R, T), const),
                  pl.BlockSpec((R, MB_BLOCK), const),
                  pl.BlockSpec((R, 1), const)],
        out_specs=pl.BlockSpec((T, W), lambda b, pt: (b, 0)),
        scratch_shapes=[pltpu.VMEM((2, past * MB_KV_HEADS, MB_HD), F32),
                        pltpu.VMEM((2, past * MB_KV_HEADS, MB_HD), F32),
                        pltpu.VMEM((R, past), F32),
                        pltpu.SemaphoreType.DMA((2, 2))],
    )
    return pl.pallas_call(
        functools.partial(_moba_sample_kernel, T=T, n_pages=n_pages, n_seq=DB),
        grid_spec=grid_spec,
        out_shape=jax.ShapeDtypeStruct((DB * T, W), F32),
        compiler_params=_params("arbitrary"),
        name="moba_sample",
    )(page_table, z, zf, zf, ck, cv, own, adj, b31)


def _merge_kernel(oa_ref, ob_ref, ga_ref, gb_ref, x_ref, wpa_ref, wpb_ref, wo_ref, o_ref):
    a = jnp.dot(oa_ref[...].astype(BF16), wpa_ref[...], preferred_element_type=F32)
    b = jnp.dot(ob_ref[...].astype(BF16), wpb_ref[...], preferred_element_type=F32)
    merged = _sigmoid(ga_ref[...].astype(F32)) * a + _sigmoid(gb_ref[...].astype(F32)) * b
    o_ref[...] = x_ref[...] + jnp.dot(merged.astype(BF16), wo_ref[...], preferred_element_type=F32)


def _merge(oa, ob, z, gcol, x, wpa, wpb, wo, tm=512):
    M, D = x.shape
    tm = min(tm, M)
    row = lambda i: (i, 0)
    const = lambda i: (0, 0)
    return pl.pallas_call(
        _merge_kernel,
        grid=(M // tm,),
        in_specs=[pl.BlockSpec((tm, D), row), pl.BlockSpec((tm, D), row),
                  pl.BlockSpec((tm, D), lambda i: (i, gcol)), pl.BlockSpec((tm, D), lambda i: (i, gcol + 1)),
                  pl.BlockSpec((tm, D), row),
                  pl.BlockSpec((D, D), const), pl.BlockSpec((D, D), const), pl.BlockSpec((D, D), const)],
        out_specs=pl.BlockSpec((tm, D), row),
        out_shape=jax.ShapeDtypeStruct((M, D), F32),
        compiler_params=_params("parallel"),
        name="merge",
    )(oa, ob, z, z, x, wpa, wpb, wo)


def _xattn_kernel(q_ref, mk_hbm, mv_hbm, o_ref, kbuf, vbuf, sem, *, n_seq):
    b = pl.program_id(0)
    hd = q_ref.shape[1] // MX_HEADS
    scale = hd ** -0.5
    slot = b % 2

    def head_copies(seq, slot):
        cps = []
        for h in range(MX_HEADS):
            cps.append(pltpu.make_async_copy(mk_hbm.at[seq, :, h, :], kbuf.at[slot, h], sem.at[0, slot]))
            cps.append(pltpu.make_async_copy(mv_hbm.at[seq, :, h, :], vbuf.at[slot, h], sem.at[1, slot]))
        return cps

    @pl.when(b == 0)
    def _():
        for cp in head_copies(0, 0):
            cp.start()

    @pl.when(b + 1 < n_seq)
    def _():
        for cp in head_copies(b + 1, 1 - slot):
            cp.start()

    for cp in head_copies(b, slot):
        cp.wait()

    q = (q_ref[...] * scale).astype(BF16)
    heads = range(MX_HEADS)
    s = [lax.dot_general(q[:, h * hd:(h + 1) * hd], kbuf[slot, h].astype(BF16), (((1,), (1,)), ((), ())),
                         preferred_element_type=F32) for h in heads]
    p = [jnp.exp(s[h] - jnp.max(s[h], axis=1, keepdims=True)) for h in heads]
    o = [jnp.dot(p[h].astype(BF16), vbuf[slot, h].astype(BF16), preferred_element_type=F32) for h in heads]
    o_ref[...] = jnp.concatenate([o[h] / jnp.sum(p[h], axis=1, keepdims=True) for h in heads], axis=1)


def _xattn(q, mk, mv, B, L):
    D = q.shape[1]
    mem, hd = mk.shape[1], mk.shape[3]
    return pl.pallas_call(
        functools.partial(_xattn_kernel, n_seq=B),
        grid=(B,),
        in_specs=[pl.BlockSpec((L, D), lambda b: (b, 0)),
                  pl.BlockSpec(memory_space=pl.ANY),
                  pl.BlockSpec(memory_space=pl.ANY)],
        out_specs=pl.BlockSpec((L, D), lambda b: (b, 0)),
        out_shape=jax.ShapeDtypeStruct((B * L, D), F32),
        scratch_shapes=[pltpu.VMEM((2, MX_HEADS, mem, hd), F32),
                        pltpu.VMEM((2, MX_HEADS, mem, hd), F32),
                        pltpu.SemaphoreType.DMA((2, 2))],
        compiler_params=_params("arbitrary"),
        name="xattn",
    )(q, mk, mv)


def _xblock_kernel(x_ref, g_ref, wq_ref, k_ref, v_ref, wo_ref, o_ref):
    x = x_ref[...]
    q = jnp.dot(_rms(x, g_ref[...]).astype(BF16), wq_ref[...], preferred_element_type=F32)
    hd = k_ref.shape[2]
    scale = hd ** -0.5
    outs = []
    for h in range(MX_HEADS):
        qh = (q[:, h * hd:(h + 1) * hd] * scale).astype(BF16)
        s = lax.dot_general(qh, k_ref[h], (((1,), (1,)), ((), ())), preferred_element_type=F32)
        p = jnp.exp(s - jnp.max(s, axis=1, keepdims=True))
        o = jnp.dot(p.astype(BF16), v_ref[h], preferred_element_type=F32)
        outs.append((o / jnp.sum(p, axis=1, keepdims=True)).astype(BF16))
    o_ref[...] = x + jnp.dot(jnp.concatenate(outs, axis=1), wo_ref[...], preferred_element_type=F32)


def _xblock(x, g, wq, mk, mv, wo, tm=512):
    M, D = x.shape
    tm = min(tm, M)
    row = lambda i: (i, 0)
    const2 = lambda i: (0, 0)
    const3 = lambda i: (0, 0, 0)
    return pl.pallas_call(
        _xblock_kernel,
        grid=(M // tm,),
        in_specs=[pl.BlockSpec((tm, D), row), pl.BlockSpec((1, D), const2), pl.BlockSpec((D, D), const2),
                  pl.BlockSpec(mk.shape, const3), pl.BlockSpec(mv.shape, const3), pl.BlockSpec((D, D), const2)],
        out_specs=pl.BlockSpec((tm, D), row),
        out_shape=jax.ShapeDtypeStruct((M, D), F32),
        compiler_params=_params("parallel"),
        name="xblock",
    )(x, g.reshape(1, D), wq, mk, mv, wo)


def _ffn_kernel(x_ref, g_ref, wg_ref, wu_ref, wd_ref, fg_ref, o_ref, *, tf):
    x = x_ref[...]
    xn = _rms(x, g_ref[...]).astype(BF16)
    acc = x
    for c in range(wg_ref.shape[1] // tf):
        sl = slice(c * tf, (c + 1) * tf)
        a = jnp.dot(xn, wg_ref[:, sl], preferred_element_type=F32)
        u = jnp.dot(xn, wu_ref[:, sl], preferred_element_type=F32)
        acc = acc + jnp.dot((_silu(a) * u).astype(BF16), wd_ref[sl, :], preferred_element_type=F32)
    o_ref[...] = _rms(acc, fg_ref[...])


def _ffn(x, g, wg, wu, wd, fg, tm=1024, tf=256):
    M, D = x.shape
    FF = wg.shape[1]
    tm = min(tm, M)
    row = lambda i: (i, 0)
    const = lambda i: (0, 0)
    once = pl.Buffered(1)
    return pl.pallas_call(
        functools.partial(_ffn_kernel, tf=tf),
        grid=(M // tm,),
        in_specs=[pl.BlockSpec((tm, D), row), pl.BlockSpec((1, D), const),
                  pl.BlockSpec((D, FF), const, pipeline_mode=once),
                  pl.BlockSpec((D, FF), const, pipeline_mode=once),
                  pl.BlockSpec((FF, D), const, pipeline_mode=once),
                  pl.BlockSpec((1, D), const)],
        out_specs=pl.BlockSpec((tm, D), row),
        out_shape=jax.ShapeDtypeStruct((M, D), F32),
        compiler_params=_params("parallel"),
        name="ffn",
    )(x, g.reshape(1, D), wg, wu, wd, fg.reshape(1, D))


def _mixer(x2, B, L, chunk, s0, moba_fn, W, za_dtype):
    tm = min(2048, B * L)
    za = _matmul(x2, W["w_in_a"], g=W["norm_mix_g"], tm=tm, tn=512, out_dtype=za_dtype)
    zf = _matmul(x2, W["w_in_f"], g=W["norm_mix_g"], tm=tm, tn=512)
    oa, S = _gla(za, zf, W["lb_logits"], W["hg_norm_g"], s0, B, L, chunk)
    ob = moba_fn(za, zf)
    h = _merge(oa, ob, za, _ZA_GA, x2, W["w_pa"], W["w_pb"], W["w_out"])
    return h, S, zf


def kernel(x_prompt, x_sample, cache_k, cache_v, cache_mem_k, cache_mem_v, state_hgrn, page_table, mem_prompt, norm_mix_g, w_in, hg_lb_logits, hg_norm_g, w_proj_a, w_proj_b, w_out, rel_bias, norm_x_g, w_xq, w_mk, w_mv, w_xo, norm_ffn_g, w_gate, w_up, w_down, final_norm_g):
    Bp, Lp, D = x_prompt.shape
    DB, T, _ = x_sample.shape
    l = 0
    win = w_in[l]
    hw = HG_HEADS * HG_DK
    hq, hf, hi, hg = (win[:, j * hw:(j + 1) * hw] for j in range(4))
    c0 = 4 * hw + MB_HEADS * MB_HD
    c1 = c0 + 2 * MB_KV_HEADS * MB_HD
    W = {
        "w_in_a": jnp.concatenate([hq, hi, hg, win[:, 4 * hw:c0], win[:, c1:]], axis=1).astype(BF16),
        "w_in_f": jnp.concatenate([hf, win[:, c0:c1]], axis=1).astype(BF16),
        "norm_mix_g": norm_mix_g[l],
        "lb_logits": hg_lb_logits,
        "hg_norm_g": hg_norm_g[l],
        "w_pa": w_proj_a[l].astype(BF16), "w_pb": w_proj_b[l].astype(BF16), "w_out": w_out[l].astype(BF16),
    }
    wxq, wxo = w_xq[l].astype(BF16), w_xo[l].astype(BF16)
    wmkv = jnp.concatenate([w_mk[l], w_mv[l]], axis=1).astype(BF16)
    wg, wu, wd = w_gate[l].astype(BF16), w_up[l].astype(BF16), w_down[l].astype(BF16)

    bias_own, bias_prev = _bias_tiles(rel_bias)

    xp = x_prompt.reshape(Bp * Lp, D)

    assert Bp == 1, "the prompt kernels take one sequence"
    kw = MB_KV_HEADS * MB_HD

    def new_kv(zf, B, L):
        return (zf[:, _ZF_K * kw:(_ZF_K + 1) * kw].reshape(B, L, MB_KV_HEADS, MB_HD),
                zf[:, _ZF_V * kw:(_ZF_V + 1) * kw].reshape(B, L, MB_KV_HEADS, MB_HD))

    def moba_p(za, zf):
        ka, va, means = _moba_prep(zf, Lp, 2 * _ZF_K, 2 * _ZF_V)
        return _moba_prompt(za, Lp, 2 * _ZA_MQ, ka, va, means, bias_own, bias_prev)

    s0 = jnp.zeros((Bp, HG_HEADS, HG_DK, HG_DV), F32)
    hp, Sp, zfp = _mixer(xp, Bp, Lp, math.gcd(Lp, 64), s0, moba_p, W, BF16)
    kp, vp = new_kv(zfp, Bp, Lp)

    mem = mem_prompt.reshape(Bp * mem_prompt.shape[1], D)
    mkv = _matmul(mem, wmkv, tm=256, tn=512)
    mshape = (Bp, mem_prompt.shape[1], MX_HEADS, D // MX_HEADS)
    mkp, mvp = mkv[:, :D].reshape(mshape), mkv[:, D:].reshape(mshape)
    hp = _xblock(hp, norm_x_g[l], wxq, mkp[0].swapaxes(0, 1).astype(BF16), mvp[0].swapaxes(0, 1).astype(BF16),
                 wxo)
    yp = _ffn(hp, norm_ffn_g[l], wg, wu, wd, final_norm_g)

    xs = x_sample.reshape(DB * T, D)

    def moba_s(za, zf):
        return _moba_sample(za, zf, page_table, cache_k[l], cache_v[l], bias_own, bias_prev, DB, T,
                            _ZA_MQ, _ZF_K, _ZF_V)

    hs, Ss, zfs = _mixer(xs, DB, T, T, state_hgrn[l], moba_s, W, F32)
    ksn, vsn = new_kv(zfs, DB, T)
    qxs = _matmul(hs, wxq, g=norm_x_g[l], tn=512)
    oxs = _xattn(qxs, cache_mem_k[l], cache_mem_v[l], DB, T)
    hs = _matmul(oxs, wxo, res=hs, tn=512)
    ys = _ffn(hs, norm_ffn_g[l], wg, wu, wd, final_norm_g)

    return (yp.reshape(Bp, Lp, D), ys.reshape(DB, T, D),
            Sp[None].astype(state_hgrn.dtype), kp[None], vp[None],
            mkp[None], mvp[None],
            Ss[None].astype(state_hgrn.dtype), ksn[None], vsn[None])
```

```python
import functools
import math

import jax
import jax.numpy as jnp
import numpy as np
from jax import lax
from jax.experimental import pallas as pl
from jax.experimental.pallas import tpu as pltpu

F32 = jnp.float32
BF16 = jnp.bfloat16

EPS = 1e-6
HG_HEADS = 8
HG_DK = 128
HG_DV = 128
HG_SUB = 16
MB_HEADS = 8
MB_KV_HEADS = 2
MB_GROUP = MB_HEADS // MB_KV_HEADS
MB_HD = 128
MB_BLOCK = 256
MB_TOPK = 3
MB_UNROLL = 4
RB_BUCKETS = 32
RB_MAX_DIST = 128
MX_HEADS = 4
PAGE_SIZE = 128
NEG = -1e30
LOG2E = math.log2(math.e)

_ZA_HQ, _ZA_HI, _ZA_HG, _ZA_MQ, _ZA_GA = 0, 1, 2, 3, 4
_ZF_HF = 0
_ZF_K, _ZF_V = 4, 5
LANES = 128
VMEM_LIMIT = 56 * 1024 * 1024


def _params(*sem):
    return pltpu.CompilerParams(dimension_semantics=sem, vmem_limit_bytes=VMEM_LIMIT)


def _sigmoid(x):
    return 1.0 / (1.0 + jnp.exp(-x))


def _silu(x):
    return x * _sigmoid(x)


def _rms(x, g):
    return x * lax.rsqrt(jnp.mean(x * x, axis=-1, keepdims=True) + EPS) * g


def _matmul_kernel(*refs, norm, residual):
    refs = list(refs)
    x_ref = refs.pop(0)
    g_ref = refs.pop(0) if norm else None
    w_ref = refs.pop(0)
    r_ref = refs.pop(0) if residual else None
    o_ref, xs_ref = refs

    @pl.when(pl.program_id(1) == 0)
    def _():
        x = x_ref[...]
        if norm:
            x = _rms(x, g_ref[...])
        xs_ref[...] = x.astype(BF16)

    acc = jnp.dot(xs_ref[...], w_ref[...], preferred_element_type=F32)
    if residual:
        acc = acc + r_ref[...]
    o_ref[...] = acc.astype(o_ref.dtype)


def _matmul(x, w, g=None, res=None, tm=512, tn=512, out_dtype=F32):
    M, K = x.shape
    N = w.shape[1]
    tm = min(tm, M)
    tn = min(tn, N)
    assert M % tm == 0 and N % tn == 0
    args = [x]
    specs = [pl.BlockSpec((tm, K), lambda i, j: (i, 0))]
    if g is not None:
        args.append(g.reshape(1, K))
        specs.append(pl.BlockSpec((1, K), lambda i, j: (0, 0)))
    args.append(w)
    specs.append(pl.BlockSpec((K, tn), lambda i, j: (0, j)))
    if res is not None:
        args.append(res)
        specs.append(pl.BlockSpec((tm, tn), lambda i, j: (i, j)))
    return pl.pallas_call(
        functools.partial(_matmul_kernel, norm=g is not None, residual=res is not None),
        grid=(M // tm, N // tn),
        in_specs=specs,
        out_specs=pl.BlockSpec((tm, tn), lambda i, j: (i, j)),
        out_shape=jax.ShapeDtypeStruct((M, N), out_dtype),
        scratch_shapes=[pltpu.VMEM((tm, K), BF16)],
        compiler_params=_params("parallel", "arbitrary"),
        name="matmul",
    )(*args)


def _proj_in_kernel(x_ref, g_ref, w_ref, oa_ref, of_ref, xs_ref, *, na):
    j = pl.program_id(1)

    @pl.when(j == 0)
    def _():
        xs_ref[...] = _rms(x_ref[...], g_ref[...]).astype(BF16)

    acc = jnp.dot(xs_ref[...], w_ref[...], preferred_element_type=F32)

    @pl.when(j < na)
    def _():
        oa_ref[...] = acc.astype(oa_ref.dtype)

    @pl.when(j >= na)
    def _():
        of_ref[...] = acc


def _proj_in(x, g, w, a_cols, a_dtype, tm, tn=512):
    M, K = x.shape
    N = w.shape[1]
    tm = min(tm, M)
    na, nf = a_cols // tn, (N - a_cols) // tn
    assert M % tm == 0 and na * tn == a_cols and (na + nf) * tn == N
    return pl.pallas_call(
        functools.partial(_proj_in_kernel, na=na),
        grid=(M // tm, na + nf),
        in_specs=[pl.BlockSpec((tm, K), lambda i, j: (i, 0)),
                  pl.BlockSpec((1, K), lambda i, j: (0, 0)),
                  pl.BlockSpec((K, tn), lambda i, j: (0, j))],
        out_specs=[pl.BlockSpec((tm, tn), lambda i, j: (i, jnp.minimum(j, na - 1))),
                   pl.BlockSpec((tm, tn), lambda i, j: (i, jnp.maximum(j - na, 0)))],
        out_shape=[jax.ShapeDtypeStruct((M, a_cols), a_dtype),
                   jax.ShapeDtypeStruct((M, N - a_cols), F32)],
        scratch_shapes=[pltpu.VMEM((tm, K), BF16)],
        compiler_params=_params("parallel", "arbitrary"),
        name="proj_in",
    )(x, g.reshape(1, K), w)


def _subblock_cumsum(g, sub):
    row = lax.broadcasted_iota(jnp.int32, g.shape, 0) % sub
    s = 1
    while s < sub:
        g = g + jnp.where(row >= s, pltpu.roll(g, s, 0), 0.0)
        s *= 2
    return g


def _gla_kernel(q_ref, f_ref, i_ref, g_ref, lbl_ref, ng_ref, s0_ref, o_ref, so_ref, st_ref, *, C, sub):
    c = pl.program_id(1)
    nsub = C // sub
    mid = sub // 2 - 1

    @pl.when(c == 0)
    def _():
        for h in range(HG_HEADS):
            st_ref[h] = s0_ref[0, h].T

    lbl = lbl_ref[...]
    e = jnp.exp(lbl - jnp.max(lbl, axis=0, keepdims=True))
    lb_all = e[0:1] / jnp.sum(e, axis=0, keepdims=True)
    ng = ng_ref[...]

    row = lax.broadcasted_iota(jnp.int32, (C, C), 0)
    col = lax.broadcasted_iota(jnp.int32, (C, C), 1)
    diag_mask = (row // sub == col // sub) & (row >= col)

    W = HG_HEADS * HG_DK
    q3 = q_ref[...].astype(F32).reshape(nsub, sub, W)
    f = lb_all + (1.0 - lb_all) * _sigmoid(f_ref[...])
    k3 = (1.0 - f).reshape(nsub, sub, W)
    cs = _subblock_cumsum(jnp.log(f), sub).reshape(nsub, sub, W)
    v_all = i_ref[...].astype(BF16)
    silu_g = _silu(g_ref[...].astype(F32))

    T = cs[:, sub - 1:sub, :]
    m = cs[:, mid:mid + 1, :]
    qd = q3 * jnp.exp(cs - m)
    kd = k3 * jnp.exp(m - cs)
    qt = qd * jnp.exp(m)
    kh = kd * jnp.exp(T - m)
    qd_all = qd.reshape(C, W).astype(BF16)
    kd_all = kd.reshape(C, W).astype(BF16)

    PT = [jnp.zeros((1, W), F32)]
    for I in range(nsub):
        PT.append(PT[-1] + T[I])

    def scaled_k(I):
        parts = []
        for J in range(nsub):
            if J < I:
                parts.append(kh[J] * jnp.exp(PT[I] - PT[J + 1]))
            else:
                parts.append(jnp.zeros((sub, W), F32))
        return jnp.concatenate(parts, axis=0).astype(BF16) if nsub > 1 else parts[0].astype(BF16)

    qt_b = [qt[I].astype(BF16) for I in range(nsub)]
    ks_all = [scaled_k(I) for I in range(1, nsub + 1)]
    qc_all = jnp.concatenate([qt[I] * jnp.exp(PT[I]) for I in range(nsub)], axis=0).astype(BF16)
    decay_all = jnp.exp(PT[nsub])

    sls = [slice(h * HG_DK, (h + 1) * HG_DK) for h in range(HG_HEADS)]
    nt = (((1,), (1,)), ((), ()))
    atts = []
    for sl in sls:
        att = jnp.where(diag_mask,
                        lax.dot_general(qd_all[:, sl], kd_all[:, sl], nt, preferred_element_type=F32), 0.0)
        if nsub > 1:
            rows = [jnp.zeros((sub, C), F32)]
            for I in range(1, nsub):
                rows.append(lax.dot_general(qt_b[I][:, sl], ks_all[I - 1][:, sl], nt,
                                            preferred_element_type=F32))
            att = att + jnp.concatenate(rows, axis=0)
        atts.append(att.astype(BF16))

    sts = [st_ref[h] for h in range(HG_HEADS)]
    outs = [jnp.dot(atts[h], v_all[:, sl], preferred_element_type=F32)
            + lax.dot_general(qc_all[:, sl], sts[h].astype(BF16), nt, preferred_element_type=F32)
            for h, sl in enumerate(sls)]
    for h, sl in enumerate(sls):
        st_ref[h] = (sts[h] * decay_all[:, sl]
                     + lax.dot_general(v_all[:, sl], ks_all[-1][:, sl], (((0,), (0,)), ((), ())),
                                       preferred_element_type=F32))
    for h, sl in enumerate(sls):
        o = outs[h]
        o = o * lax.rsqrt(jnp.mean(o * o, axis=-1, keepdims=True) + EPS) * ng * silu_g[:, sl]
        o_ref[:, sl] = o.astype(o_ref.dtype)

    @pl.when(c == pl.num_programs(1) - 1)
    def _():
        for h in range(HG_HEADS):
            so_ref[0, h] = st_ref[h].T


def _gla(za, zf, lb_logits, norm_g, s0, B, L, C):
    nc = L // C
    sub = min(HG_SUB, C)
    W = HG_HEADS * HG_DK

    def zspec(j):
        return pl.BlockSpec((C, W), lambda b, c, j=j: (b * nc + c, j))

    return pl.pallas_call(
        functools.partial(_gla_kernel, C=C, sub=sub),
        grid=(B, nc),
        in_specs=[zspec(_ZA_HQ), zspec(_ZF_HF), zspec(_ZA_HI), zspec(_ZA_HG),
                  pl.BlockSpec((2, W), lambda b, c: (0, 0)),
                  pl.BlockSpec((1, HG_DV), lambda b, c: (0, 0)),
                  pl.BlockSpec((1, HG_HEADS, HG_DK, HG_DV), lambda b, c: (b, 0, 0, 0))],
        out_specs=[pl.BlockSpec((C, W), lambda b, c: (b * nc + c, 0)),
                   pl.BlockSpec((1, HG_HEADS, HG_DK, HG_DV), lambda b, c: (b, 0, 0, 0))],
        out_shape=[jax.ShapeDtypeStruct((B * L, W), za.dtype),
                   jax.ShapeDtypeStruct((B, HG_HEADS, HG_DK, HG_DV), F32)],
        scratch_shapes=[pltpu.VMEM((HG_HEADS, HG_DV, HG_DK), F32)],
        compiler_params=_params("arbitrary", "arbitrary"),
        name="gla",
    )(za, zf, za, za, lb_logits, norm_g.reshape(1, HG_DV), s0)


def _bucket_upper_bounds():
    max_exact = RB_BUCKETS // 2
    d = np.arange(1, 4 * RB_MAX_DIST)
    large = max_exact + (np.log(d.astype(np.float32) / max_exact) / math.log(RB_MAX_DIST / max_exact)
                         * (RB_BUCKETS - max_exact)).astype(np.int32)
    bucket = np.where(d < max_exact, d, np.minimum(large, RB_BUCKETS - 1))
    bucket = np.concatenate([[0], bucket])
    return [int(np.max(np.nonzero(bucket == b)[0])) for b in range(RB_BUCKETS - 1)]


def _bias_kernel(rb_ref, own_ref, prev_ref):
    h = pl.program_id(0)
    i = lax.broadcasted_iota(jnp.int32, (MB_BLOCK, MB_BLOCK), 0)
    j = lax.broadcasted_iota(jnp.int32, (MB_BLOCK, MB_BLOCK), 1)
    dmax = _bucket_upper_bounds()

    def table(d):
        val = jnp.full(d.shape, rb_ref[RB_BUCKETS - 1, h], F32)
        for b in range(RB_BUCKETS - 2, -1, -1):
            val = jnp.where(d <= dmax[b], rb_ref[b, h], val)
        return val

    d = i - j
    own_ref[0] = jnp.where(d >= 0, table(d), NEG)
    prev_ref[0] = table(d + MB_BLOCK)


def _bias_tiles(rel_bias):
    shp = jax.ShapeDtypeStruct((MB_HEADS, MB_BLOCK, MB_BLOCK), F32)
    spec = pl.BlockSpec((1, MB_BLOCK, MB_BLOCK), lambda h: (h, 0, 0))
    return pl.pallas_call(
        _bias_kernel,
        grid=(MB_HEADS,),
        in_specs=[pl.BlockSpec(memory_space=pltpu.SMEM)],
        out_specs=[spec, spec],
        out_shape=[shp, shp],
        compiler_params=_params("arbitrary"),
        name="bias_tiles",
    )(rel_bias)


def _top_select(gate, lane, axis=1):
    sel = jnp.zeros(gate.shape, jnp.bool_)
    lane = lane.astype(F32)
    for _ in range(MB_TOPK):
        m = jnp.max(gate, axis=axis, keepdims=True)
        idx = jnp.min(jnp.where(gate == m, lane, float(2 ** 30)), axis=axis, keepdims=True)
        hit = (lane == idx) & (m > -jnp.inf)
        sel = sel | hit
        gate = jnp.where(hit, -jnp.inf, gate)
    return sel


def _moba_prep_kernel(k_ref, v_ref, ka_ref, vb_ref, mean_ref, *, per_step):
    rows = per_step * MB_BLOCK
    k = k_ref[...]
    block = (lax.broadcasted_iota(jnp.int32, (rows, LANES), 0) // MB_BLOCK
             + pl.program_id(1) * per_step)
    lane = lax.broadcasted_iota(jnp.int32, (rows, LANES), 1)
    onehot = jnp.where(lane == block, 1.0, 0.0).astype(BF16)
    ka_ref[0] = jnp.concatenate([k.astype(BF16), onehot], axis=1)
    vb_ref[0] = jnp.concatenate([v_ref[...].astype(BF16), jnp.ones((rows, LANES), BF16)], axis=1)
    mean_ref[0] = jnp.mean(k.reshape(per_step, MB_BLOCK, MB_HD), axis=1)


def _moba_prep(z, L, kcol, vcol):
    nb = L // MB_BLOCK
    per_step = math.gcd(nb, 8)
    rows = per_step * MB_BLOCK
    return pl.pallas_call(
        functools.partial(_moba_prep_kernel, per_step=per_step),
        grid=(MB_KV_HEADS, nb // per_step),
        in_specs=[pl.BlockSpec((rows, MB_HD), lambda g, n: (n, kcol + g)),
                  pl.BlockSpec((rows, MB_HD), lambda g, n: (n, vcol + g))],
        out_specs=[pl.BlockSpec((1, rows, 2 * MB_HD), lambda g, n: (g, n, 0)),
                   pl.BlockSpec((1, rows, 2 * MB_HD), lambda g, n: (g, n, 0)),
                   pl.BlockSpec((1, per_step, MB_HD), lambda g, n: (g, n, 0))],
        out_shape=[jax.ShapeDtypeStruct((MB_KV_HEADS, L, 2 * MB_HD), BF16),
                   jax.ShapeDtypeStruct((MB_KV_HEADS, L, 2 * MB_HD), BF16),
                   jax.ShapeDtypeStruct((MB_KV_HEADS, nb, MB_HD), F32)],
        compiler_params=_params("arbitrary", "arbitrary"),
        name="moba_prep",
    )(z, z)


def _moba_prompt_kernel(q_ref, ka_ref, va_ref, mean_ref, own_ref, prev_ref, o_ref,
                        qa_ref, m_ref, acc_ref, s_ref):
    i = pl.program_id(1)
    G, B = MB_GROUP, MB_BLOCK
    R = G * B
    scale = MB_HD ** -0.5
    blk = lax.broadcasted_iota(jnp.int32, (LANES, B), 0)
    means = mean_ref[0]

    for j in range(G):
        q = q_ref[:, j * MB_HD:(j + 1) * MB_HD].astype(F32)
        gate = lax.dot_general(means, q, (((1,), (1,)), ((), ())), precision=lax.Precision.HIGHEST,
                               preferred_element_type=F32)
        sel = _top_select(jnp.where(blk < i, gate, -jnp.inf), blk, axis=0)
        b31 = prev_ref[j, 0:1, 0:1]
        mb = jnp.where(sel, jnp.where(blk == i - 1, 0.0, b31), NEG)
        mb = jnp.where(blk == i, 0.0, mb).T
        qa_ref[j * B:(j + 1) * B, :] = jnp.concatenate(
            [(q * (scale * LOG2E)).astype(BF16), (mb * LOG2E).astype(BF16)], axis=1)

    def block_rows(n):
        return pl.ds(pl.multiple_of(n * B, B), B)

    def qk(n):
        return lax.dot_general(qa_ref[...], ka_ref[0, block_rows(n), :], (((1,), (1,)), ((), ())),
                               preferred_element_type=F32)

    def softmax_pv(s, n, first):
        vn = va_ref[0, block_rows(n), :]
        mx = jnp.max(s, axis=1, keepdims=True)
        if first:
            m_new = jnp.broadcast_to(mx, (R, LANES))
            p = jnp.exp2(s - mx)
            acc_ref[...] = jnp.dot(p.astype(BF16), vn, preferred_element_type=F32)
        else:
            m_prev = m_ref[...]
            m_new = jnp.maximum(m_prev, mx)
            alpha = jnp.exp2(m_prev - m_new)
            p = jnp.exp2(s - jnp.concatenate([m_new, m_new], axis=1))
            acc_ref[...] = (jnp.concatenate([alpha, alpha], axis=1) * acc_ref[...]
                            + jnp.dot(p.astype(BF16), vn, preferred_element_type=F32))
        m_ref[...] = m_new

    def own_logits():
        return qk(i) + own_ref[...].reshape(R, B) * LOG2E

    def prev_logits():
        return qk(i - 1) + prev_ref[...].reshape(R, B) * LOG2E

    @pl.when(i == 0)
    def _():
        softmax_pv(own_logits(), i, True)

    @pl.when(i == 1)
    def _():
        softmax_pv(own_logits(), i, True)
        softmax_pv(prev_logits(), i - 1, False)

    @pl.when(i >= 2)
    def _():
        s_own = own_logits()
        s_prev = prev_logits()
        softmax_pv(s_own, i, True)
        s_ref[...] = qk(0)
        softmax_pv(s_prev, i - 1, False)
        last = i - 2

        def run(n0, count):
            s_cur = s_ref[...]
            for u in range(count):
                s_next = qk(jnp.minimum(n0 + u + 1, last))
                softmax_pv(s_cur, n0 + u, False)
                s_cur = s_next
            s_ref[...] = s_cur

        def unrolled(t, carry):
            run(MB_UNROLL * t, MB_UNROLL)
            return carry

        def single(n, carry):
            run(n, 1)
            return carry

        n_main = (i - 1) // MB_UNROLL
        lax.fori_loop(0, n_main, unrolled, 0)
        lax.fori_loop(n_main * MB_UNROLL, i - 1, single, 0)

    acc = acc_ref[...]
    o = acc[:, :MB_HD] / acc[:, MB_HD:]
    for j in range(G):
        o_ref[:, j * MB_HD:(j + 1) * MB_HD] = o[j * B:(j + 1) * B].astype(o_ref.dtype)


def _moba_prompt(z, L, qcol, ka, vb, means, bias_own, bias_prev):
    nb = L // MB_BLOCK
    G, B = MB_GROUP, MB_BLOCK
    means_p = jnp.pad(means.reshape(MB_KV_HEADS, nb, MB_HD), ((0, 0), (0, LANES - nb), (0, 0)))
    return pl.pallas_call(
        _moba_prompt_kernel,
        grid=(MB_KV_HEADS, nb),
        in_specs=[pl.BlockSpec((B, G * MB_HD), lambda g, i: (i, qcol + g)),
                  pl.BlockSpec((1, L, 2 * MB_HD), lambda g, i: (g, 0, 0), pipeline_mode=pl.Buffered(1)),
                  pl.BlockSpec((1, L, 2 * MB_HD), lambda g, i: (g, 0, 0), pipeline_mode=pl.Buffered(1)),
                  pl.BlockSpec((1, LANES, MB_HD), lambda g, i: (g, 0, 0)),
                  pl.BlockSpec((G, B, B), lambda g, i: (g, 0, 0)),
                  pl.BlockSpec((G, B, B), lambda g, i: (g, 0, 0))],
        out_specs=pl.BlockSpec((B, G * MB_HD), lambda g, i: (i, g)),
        out_shape=jax.ShapeDtypeStruct((L, MB_HEADS * MB_HD), z.dtype),
        scratch_shapes=[pltpu.VMEM((G * B, 2 * MB_HD), BF16),
                        pltpu.VMEM((G * B, LANES), F32),
                        pltpu.VMEM((G * B, 2 * MB_HD), F32),
                        pltpu.VMEM((G * B, B), F32)],
        compiler_params=_params("arbitrary", "arbitrary"),
        name="moba_prompt",
    )(z, ka, vb, means_p, bias_own, bias_prev)


def _moba_sample_kernel(pt_ref, q_ref, kn_ref, vn_ref, ck_hbm, cv_hbm, own_ref, adj_ref, b31_ref, o_ref,
                        kbuf, vbuf, lg_ref, sem, *, T, n_pages, n_seq):
    b = pl.program_id(0)
    B = MB_BLOCK
    past = n_pages * PAGE_SIZE
    nb = past // B
    R = MB_HEADS * T
    half = R // MB_KV_HEADS
    scale = MB_HD ** -0.5

    KVH = MB_KV_HEADS
    prow = PAGE_SIZE * KVH

    def page_copies(seq, slot, j):
        dst = pl.ds(pl.multiple_of(j * prow, prow), prow)
        page = pt_ref[seq, j]
        return (pltpu.make_async_copy(ck_hbm.at[page], kbuf.at[slot, dst, :], sem.at[0, slot]),
                pltpu.make_async_copy(cv_hbm.at[page], vbuf.at[slot, dst, :], sem.at[1, slot]))

    def start_seq(seq, slot):
        def body(j, c):
            ck, cv = page_copies(seq, slot, j)
            ck.start()
            cv.start()
            return c
        lax.fori_loop(0, n_pages, body, 0)

    def wait_seq(seq, slot):
        def body(j, c):
            ck, cv = page_copies(seq, slot, j)
            ck.wait()
            cv.wait()
            return c
        lax.fori_loop(0, n_pages, body, 0)

    slot = b % 2

    @pl.when(b == 0)
    def _():
        start_seq(0, 0)

    @pl.when(b + 1 < n_seq)
    def _():
        start_seq(b + 1, 1 - slot)

    wait_seq(b, slot)

    def block(buf, n, g):
        return buf[slot, pl.ds(n * B * KVH + g, B, stride=KVH), :]

    q = q_ref[...]
    Q = jnp.concatenate([q[:, h * MB_HD:(h + 1) * MB_HD] for h in range(MB_HEADS)], axis=0)
    qf = Q * scale
    qs = qf.astype(BF16)
    grow = [slice(g * half, (g + 1) * half) for g in range(KVH)]

    mean_rows = [[] for _ in range(KVH)]
    for n in range(nb):
        for g in range(KVH):
            kf = block(kbuf, n, g)
            mean_rows[g].append(jnp.mean(kf, axis=0, keepdims=True))
            lg_ref[grow[g], n * B:(n + 1) * B] = lax.dot_general(
                qs[grow[g]], kf.astype(BF16), (((1,), (1,)), ((), ())), preferred_element_type=F32)

    gates = []
    for g in range(KVH):
        means = jnp.concatenate(mean_rows[g] + [jnp.zeros((LANES - nb, MB_HD), F32)], axis=0)
        gates.append(lax.dot_general(Q[grow[g]], means, (((1,), (1,)), ((), ())),
                                     precision=lax.Precision.HIGHEST, preferred_element_type=F32))
    gate = jnp.concatenate(gates, axis=0)
    lane = lax.broadcasted_iota(jnp.int32, (R, LANES), 1)
    sel = _top_select(jnp.where(lane < nb, gate, -jnp.inf), lane)
    mb = jnp.where(sel, jnp.where(lane == nb - 1, 0.0, b31_ref[...]), NEG)
    adj = adj_ref[...]

    def logits(n):
        s = lg_ref[:, n * B:(n + 1) * B] + mb[:, n:n + 1]
        return s + adj if n == nb - 1 else s

    kn = kn_ref[...]
    vn = vn_ref[...]
    s_own = jnp.concatenate(
        [lax.dot_general(qf[grow[g]], kn[:, g * MB_HD:(g + 1) * MB_HD], (((1,), (1,)), ((), ())),
                         preferred_element_type=F32) for g in range(KVH)], axis=0) + own_ref[...]

    m = jnp.max(s_own, axis=1, keepdims=True)
    for n in range(nb):
        m = jnp.maximum(m, jnp.max(logits(n), axis=1, keepdims=True))

    p_own = jnp.exp(s_own - m)
    l = jnp.sum(p_own, axis=1, keepdims=True)
    acc = [jnp.dot(p_own[grow[g]], vn[:, g * MB_HD:(g + 1) * MB_HD], preferred_element_type=F32)
           for g in range(KVH)]
    for n in range(nb):
        p = jnp.exp(logits(n) - m)
        l = l + jnp.sum(p, axis=1, keepdims=True)
        pb = p.astype(BF16)
        for g in range(KVH):
            acc[g] = acc[g] + jnp.dot(pb[grow[g]], block(vbuf, n, g).astype(BF16),
                                      preferred_element_type=F32)
    rows = jnp.concatenate(acc, axis=0) / l
    o_ref[...] = jnp.concatenate([rows[h * T:(h + 1) * T] for h in range(MB_HEADS)], axis=1)


def _moba_sample(z, zf, page_table, cache_k, cache_v, bias_own, bias_prev, DB, T, qcol, kcol, vcol):
    n_pages = page_table.shape[1]
    past = n_pages * PAGE_SIZE
    R = MB_HEADS * T
    W = MB_HEADS * MB_HD
    KW = MB_KV_HEADS * MB_HD
    ck = cache_k.reshape(cache_k.shape[0], PAGE_SIZE * MB_KV_HEADS, MB_HD)
    cv = cache_v.reshape(cache_v.shape[0], PAGE_SIZE * MB_KV_HEADS, MB_HD)
    own = bias_own[:, :T, :T].reshape(R, T)
    adj = bias_prev[:, :T, :].reshape(R, MB_BLOCK)
    b31 = bias_prev[:, :T, 0:1].reshape(R, 1)
    const = lambda b, pt: (0, 0)
    grid_spec = pltpu.PrefetchScalarGridSpec(
        num_scalar_prefetch=1,
        grid=(DB,),
        in_specs=[pl.BlockSpec((T, W), lambda b, pt: (b, qcol)),
                  pl.BlockSpec((T, KW), lambda b, pt: (b, kcol)),
                  pl.BlockSpec((T, KW), lambda b, pt: (b, vcol)),
                  pl.BlockSpec(memory_space=pl.ANY),
                  pl.BlockSpec(memory_space=pl.ANY),
                  pl.BlockSpec((R, T), const),
                  pl.BlockSpec((R, MB_BLOCK), const),
                  pl.BlockSpec((R, 1), const)],
        out_specs=pl.BlockSpec((T, W), lambda b, pt: (b, 0)),
        scratch_shapes=[pltpu.VMEM((2, past * MB_KV_HEADS, MB_HD), F32),
                        pltpu.VMEM((2, past * MB_KV_HEADS, MB_HD), F32),
                        pltpu.VMEM((R, past), F32),
                        pltpu.SemaphoreType.DMA((2, 2))],
    )
    return pl.pallas_call(
        functools.partial(_moba_sample_kernel, T=T, n_pages=n_pages, n_seq=DB),
        grid_spec=grid_spec,
        out_shape=jax.ShapeDtypeStruct((DB * T, W), F32),
        compiler_params=_params("arbitrary"),
        name="moba_sample",
    )(page_table, z, zf, zf, ck, cv, own, adj, b31)


def _merge_kernel(oa_ref, ob_ref, ga_ref, gb_ref, x_ref, wpa_ref, wpb_ref, wo_ref, o_ref):
    a = jnp.dot(oa_ref[...].astype(BF16), wpa_ref[...], preferred_element_type=F32)
    b = jnp.dot(ob_ref[...].astype(BF16), wpb_ref[...], preferred_element_type=F32)
    merged = _sigmoid(ga_ref[...].astype(F32)) * a + _sigmoid(gb_ref[...].astype(F32)) * b
    o_ref[...] = x_ref[...] + jnp.dot(merged.astype(BF16), wo_ref[...], preferred_element_type=F32)


def _merge(oa, ob, z, gcol, x, wpa, wpb, wo, tm=512):
    M, D = x.shape
    tm = min(tm, M)
    row = lambda i: (i, 0)
    const = lambda i: (0, 0)
    return pl.pallas_call(
        _merge_kernel,
        grid=(M // tm,),
        in_specs=[pl.BlockSpec((tm, D), row), pl.BlockSpec((tm, D), row),
                  pl.BlockSpec((tm, D), lambda i: (i, gcol)), pl.BlockSpec((tm, D), lambda i: (i, gcol + 1)),
                  pl.BlockSpec((tm, D), row),
                  pl.BlockSpec((D, D), const), pl.BlockSpec((D, D), const), pl.BlockSpec((D, D), const)],
        out_specs=pl.BlockSpec((tm, D), row),
        out_shape=jax.ShapeDtypeStruct((M, D), F32),
        compiler_params=_params("parallel"),
        name="merge",
    )(oa, ob, z, z, x, wpa, wpb, wo)


def _xattn_kernel(q_ref, mk_hbm, mv_hbm, o_ref, kbuf, vbuf, sem, *, n_steps, per_step, L):
    b = pl.program_id(0)
    hd = q_ref.shape[1] // MX_HEADS
    scale = hd ** -0.5
    slot = b % 2

    def head_copies(step, slot):
        cps = []
        for j in range(per_step):
            seq = step * per_step + j
            for h in range(MX_HEADS):
                cps.append(pltpu.make_async_copy(mk_hbm.at[seq, :, h, :], kbuf.at[slot, j, h], sem.at[0, slot]))
                cps.append(pltpu.make_async_copy(mv_hbm.at[seq, :, h, :], vbuf.at[slot, j, h], sem.at[1, slot]))
        return cps

    @pl.when(b == 0)
    def _():
        for cp in head_copies(0, 0):
            cp.start()

    @pl.when(b + 1 < n_steps)
    def _():
        for cp in head_copies(b + 1, 1 - slot):
            cp.start()

    for cp in head_copies(b, slot):
        cp.wait()

    q = (q_ref[...] * scale).astype(BF16)
    pairs = [(j, h) for j in range(per_step) for h in range(MX_HEADS)]
    s = [lax.dot_general(q[j * L:(j + 1) * L, h * hd:(h + 1) * hd], kbuf[slot, j, h].astype(BF16),
                         (((1,), (1,)), ((), ())), preferred_element_type=F32) for j, h in pairs]
    p = [jnp.exp(x - jnp.max(x, axis=1, keepdims=True)) for x in s]
    o = [jnp.dot(p[c].astype(BF16), vbuf[slot, j, h].astype(BF16), preferred_element_type=F32)
         / jnp.sum(p[c], axis=1, keepdims=True) for c, (j, h) in enumerate(pairs)]
    o_ref[...] = jnp.concatenate(
        [jnp.concatenate(o[j * MX_HEADS:(j + 1) * MX_HEADS], axis=1) for j in range(per_step)], axis=0)


def _xattn(q, mk, mv, B, L, per_step=4):
    D = q.shape[1]
    mem, hd = mk.shape[1], mk.shape[3]
    per_step = math.gcd(per_step, B)
    n_steps = B // per_step
    rows = per_step * L
    return pl.pallas_call(
        functools.partial(_xattn_kernel, n_steps=n_steps, per_step=per_step, L=L),
        grid=(n_steps,),
        in_specs=[pl.BlockSpec((rows, D), lambda b: (b, 0)),
                  pl.BlockSpec(memory_space=pl.ANY),
                  pl.BlockSpec(memory_space=pl.ANY)],
        out_specs=pl.BlockSpec((rows, D), lambda b: (b, 0)),
        out_shape=jax.ShapeDtypeStruct((B * L, D), F32),
        scratch_shapes=[pltpu.VMEM((2, per_step, MX_HEADS, mem, hd), F32),
                        pltpu.VMEM((2, per_step, MX_HEADS, mem, hd), F32),
                        pltpu.SemaphoreType.DMA((2, 2))],
        compiler_params=_params("arbitrary"),
        name="xattn",
    )(q, mk, mv)


def _xblock_kernel(x_ref, g_ref, wq_ref, k_ref, v_ref, wo_ref, o_ref):
    x = x_ref[...]
    q = jnp.dot(_rms(x, g_ref[...]).astype(BF16), wq_ref[...], preferred_element_type=F32)
    hd = k_ref.shape[2]
    scale = hd ** -0.5
    outs = []
    for h in range(MX_HEADS):
        qh = (q[:, h * hd:(h + 1) * hd] * scale).astype(BF16)
        s = lax.dot_general(qh, k_ref[h], (((1,), (1,)), ((), ())), preferred_element_type=F32)
        p = jnp.exp(s - jnp.max(s, axis=1, keepdims=True))
        o = jnp.dot(p.astype(BF16), v_ref[h], preferred_element_type=F32)
        outs.append((o / jnp.sum(p, axis=1, keepdims=True)).astype(BF16))
    o_ref[...] = x + jnp.dot(jnp.concatenate(outs, axis=1), wo_ref[...], preferred_element_type=F32)


def _xblock(x, g, wq, mk, mv, wo, tm=512):
    M, D = x.shape
    tm = min(tm, M)
    row = lambda i: (i, 0)
    const2 = lambda i: (0, 0)
    const3 = lambda i: (0, 0, 0)
    return pl.pallas_call(
        _xblock_kernel,
        grid=(M // tm,),
        in_specs=[pl.BlockSpec((tm, D), row), pl.BlockSpec((1, D), const2), pl.BlockSpec((D, D), const2),
                  pl.BlockSpec(mk.shape, const3), pl.BlockSpec(mv.shape, const3), pl.BlockSpec((D, D), const2)],
        out_specs=pl.BlockSpec((tm, D), row),
        out_shape=jax.ShapeDtypeStruct((M, D), F32),
        compiler_params=_params("parallel"),
        name="xblock",
    )(x, g.reshape(1, D), wq, mk, mv, wo)


def _ffn_kernel(x_ref, g_ref, wg_ref, wu_ref, wd_ref, fg_ref, o_ref, *, tf):
    x = x_ref[...]
    xn = _rms(x, g_ref[...]).astype(BF16)
    acc = x
    for c in range(wg_ref.shape[1] // tf):
        sl = slice(c * tf, (c + 1) * tf)
        a = jnp.dot(xn, wg_ref[:, sl], preferred_element_type=F32)
        u = jnp.dot(xn, wu_ref[:, sl], preferred_element_type=F32)
        acc = acc + jnp.dot((_silu(a) * u).astype(BF16), wd_ref[sl, :], preferred_element_type=F32)
    o_ref[...] = _rms(acc, fg_ref[...])


def _ffn(x, g, wg, wu, wd, fg, tm=1024, tf=256):
    M, D = x.shape
    FF = wg.shape[1]
    tm = min(tm, M)
    row = lambda i: (i, 0)
    const = lambda i: (0, 0)
    once = pl.Buffered(1)
    return pl.pallas_call(
        functools.partial(_ffn_kernel, tf=tf),
        grid=(M // tm,),
        in_specs=[pl.BlockSpec((tm, D), row), pl.BlockSpec((1, D), const),
                  pl.BlockSpec((D, FF), const, pipeline_mode=once),
                  pl.BlockSpec((D, FF), const, pipeline_mode=once),
                  pl.BlockSpec((FF, D), const, pipeline_mode=once),
                  pl.BlockSpec((1, D), const)],
        out_specs=pl.BlockSpec((tm, D), row),
        out_shape=jax.ShapeDtypeStruct((M, D), F32),
        compiler_params=_params("parallel"),
        name="ffn",
    )(x, g.reshape(1, D), wg, wu, wd, fg.reshape(1, D))


def _mixer(x2, B, L, chunk, s0, moba_fn, W, za_dtype):
    za, zf = _proj_in(x2, W["norm_mix_g"], W["w_in"], W["a_cols"], za_dtype, tm=2048)
    oa, S = _gla(za, zf, W["lb_logits"], W["hg_norm_g"], s0, B, L, chunk)
    ob = moba_fn(za, zf)
    h = _merge(oa, ob, za, _ZA_GA, x2, W["w_pa"], W["w_pb"], W["w_out"])
    return h, S, zf


def kernel(x_prompt, x_sample, cache_k, cache_v, cache_mem_k, cache_mem_v, state_hgrn, page_table, mem_prompt, norm_mix_g, w_in, hg_lb_logits, hg_norm_g, w_proj_a, w_proj_b, w_out, rel_bias, norm_x_g, w_xq, w_mk, w_mv, w_xo, norm_ffn_g, w_gate, w_up, w_down, final_norm_g):
    Bp, Lp, D = x_prompt.shape
    DB, T, _ = x_sample.shape
    l = 0
    win = w_in[l]
    hw = HG_HEADS * HG_DK
    hq, hf, hi, hg = (win[:, j * hw:(j + 1) * hw] for j in range(4))
    c0 = 4 * hw + MB_HEADS * MB_HD
    c1 = c0 + 2 * MB_KV_HEADS * MB_HD
    W = {
        "w_in": jnp.concatenate([hq, hi, hg, win[:, 4 * hw:c0], win[:, c1:], hf, win[:, c0:c1]],
                                axis=1).astype(BF16),
        "a_cols": win.shape[1] - hw - (c1 - c0),
        "norm_mix_g": norm_mix_g[l],
        "lb_logits": hg_lb_logits,
        "hg_norm_g": hg_norm_g[l],
        "w_pa": w_proj_a[l].astype(BF16), "w_pb": w_proj_b[l].astype(BF16), "w_out": w_out[l].astype(BF16),
    }
    wxq, wxo = w_xq[l].astype(BF16), w_xo[l].astype(BF16)
    wmkv = jnp.concatenate([w_mk[l], w_mv[l]], axis=1).astype(BF16)
    wg, wu, wd = w_gate[l].astype(BF16), w_up[l].astype(BF16), w_down[l].astype(BF16)

    bias_own, bias_prev = _bias_tiles(rel_bias)

    xp = x_prompt.reshape(Bp * Lp, D)

    assert Bp == 1, "the prompt kernels take one sequence"
    kw = MB_KV_HEADS * MB_HD

    def new_kv(zf, B, L):
        return (zf[:, _ZF_K * kw:(_ZF_K + 1) * kw].reshape(B, L, MB_KV_HEADS, MB_HD),
                zf[:, _ZF_V * kw:(_ZF_V + 1) * kw].reshape(B, L, MB_KV_HEADS, MB_HD))

    def moba_p(za, zf):
        ka, va, means = _moba_prep(zf, Lp, 2 * _ZF_K, 2 * _ZF_V)
        return _moba_prompt(za, Lp, 2 * _ZA_MQ, ka, va, means, bias_own, bias_prev)

    s0 = jnp.zeros((Bp, HG_HEADS, HG_DK, HG_DV), F32)
    hp, Sp, zfp = _mixer(xp, Bp, Lp, math.gcd(Lp, 64), s0, moba_p, W, BF16)
    kp, vp = new_kv(zfp, Bp, Lp)

    mem = mem_prompt.reshape(Bp * mem_prompt.shape[1], D)
    mkv = _matmul(mem, wmkv, tm=256, tn=512)
    mshape = (Bp, mem_prompt.shape[1], MX_HEADS, D // MX_HEADS)
    mkp, mvp = mkv[:, :D].reshape(mshape), mkv[:, D:].reshape(mshape)
    hp = _xblock(hp, norm_x_g[l], wxq, mkp[0].swapaxes(0, 1).astype(BF16), mvp[0].swapaxes(0, 1).astype(BF16),
                 wxo)
    yp = _ffn(hp, norm_ffn_g[l], wg, wu, wd, final_norm_g)

    xs = x_sample.reshape(DB * T, D)

    def moba_s(za, zf):
        return _moba_sample(za, zf, page_table, cache_k[l], cache_v[l], bias_own, bias_prev, DB, T,
                            _ZA_MQ, _ZF_K, _ZF_V)

    hs, Ss, zfs = _mixer(xs, DB, T, T, state_hgrn[l], moba_s, W, F32)
    ksn, vsn = new_kv(zfs, DB, T)
    qxs = _matmul(hs, wxq, g=norm_x_g[l], tn=512)
    oxs = _xattn(qxs, cache_mem_k[l], cache_mem_v[l], DB, T)
    hs = _matmul(oxs, wxo, res=hs, tn=512)
    ys = _ffn(hs, norm_ffn_g[l], wg, wu, wd, final_norm_g)

    return (yp.reshape(Bp, Lp, D), ys.reshape(DB, T, D),
            Sp[None].astype(state_hgrn.dtype), kp[None], vp[None],
            mkp[None], mvp[None],
            Ss[None].astype(state_hgrn.dtype), ksn[None], vsn[None])
```

```python
import functools
import math

import jax
import jax.numpy as jnp
import numpy as np
from jax import lax
from jax.experimental import pallas as pl
from jax.experimental.pallas import tpu as pltpu

F32 = jnp.float32
BF16 = jnp.bfloat16

EPS = 1e-6
HG_HEADS = 8
HG_DK = 128
HG_DV = 128
HG_SUB = 16
MB_HEADS = 8
MB_KV_HEADS = 2
MB_GROUP = MB_HEADS // MB_KV_HEADS
MB_HD = 128
MB_BLOCK = 256
MB_TOPK = 3
MB_UNROLL = 4
RB_BUCKETS = 32
RB_MAX_DIST = 128
MX_HEADS = 4
PAGE_SIZE = 128
NEG = -1e30
LOG2E = math.log2(math.e)

_ZA_HQ, _ZA_HI, _ZA_HG, _ZA_MQ, _ZA_GA = 0, 1, 2, 3, 4
_ZF_HF = 0
_ZF_K, _ZF_V = 4, 5
LANES = 128
VMEM_LIMIT = 56 * 1024 * 1024


def _params(*sem):
    return pltpu.CompilerParams(dimension_semantics=sem, vmem_limit_bytes=VMEM_LIMIT)


def _sigmoid(x):
    return 1.0 / (1.0 + jnp.exp(-x))


def _silu(x):
    return x * _sigmoid(x)


def _rms(x, g):
    return x * lax.rsqrt(jnp.mean(x * x, axis=-1, keepdims=True) + EPS) * g


def _matmul_kernel(*refs, norm, residual):
    refs = list(refs)
    x_ref = refs.pop(0)
    g_ref = refs.pop(0) if norm else None
    w_ref = refs.pop(0)
    r_ref = refs.pop(0) if residual else None
    o_ref, xs_ref = refs

    @pl.when(pl.program_id(1) == 0)
    def _():
        x = x_ref[...]
        if norm:
            x = _rms(x, g_ref[...])
        xs_ref[...] = x.astype(BF16)

    acc = jnp.dot(xs_ref[...], w_ref[...], preferred_element_type=F32)
    if residual:
        acc = acc + r_ref[...]
    o_ref[...] = acc.astype(o_ref.dtype)


def _matmul(x, w, g=None, res=None, tm=512, tn=512, out_dtype=F32):
    M, K = x.shape
    N = w.shape[1]
    tm = min(tm, M)
    tn = min(tn, N)
    assert M % tm == 0 and N % tn == 0
    args = [x]
    specs = [pl.BlockSpec((tm, K), lambda i, j: (i, 0))]
    if g is not None:
        args.append(g.reshape(1, K))
        specs.append(pl.BlockSpec((1, K), lambda i, j: (0, 0)))
    args.append(w)
    specs.append(pl.BlockSpec((K, tn), lambda i, j: (0, j)))
    if res is not None:
        args.append(res)
        specs.append(pl.BlockSpec((tm, tn), lambda i, j: (i, j)))
    return pl.pallas_call(
        functools.partial(_matmul_kernel, norm=g is not None, residual=res is not None),
        grid=(M // tm, N // tn),
        in_specs=specs,
        out_specs=pl.BlockSpec((tm, tn), lambda i, j: (i, j)),
        out_shape=jax.ShapeDtypeStruct((M, N), out_dtype),
        scratch_shapes=[pltpu.VMEM((tm, K), BF16)],
        compiler_params=_params("parallel", "arbitrary"),
        name="matmul",
    )(*args)


def _proj_in_kernel(x_ref, g_ref, w_ref, oa_ref, of_ref, xs_ref, *, na):
    j = pl.program_id(1)

    @pl.when(j == 0)
    def _():
        xs_ref[...] = _rms(x_ref[...], g_ref[...]).astype(BF16)

    acc = jnp.dot(xs_ref[...], w_ref[...], preferred_element_type=F32)

    @pl.when(j < na)
    def _():
        oa_ref[...] = acc.astype(oa_ref.dtype)

    @pl.when(j >= na)
    def _():
        of_ref[...] = acc


def _proj_in(x, g, w, a_cols, a_dtype, tm, tn=512):
    M, K = x.shape
    N = w.shape[1]
    tm = min(tm, M)
    na, nf = a_cols // tn, (N - a_cols) // tn
    assert M % tm == 0 and na * tn == a_cols and (na + nf) * tn == N
    return pl.pallas_call(
        functools.partial(_proj_in_kernel, na=na),
        grid=(M // tm, na + nf),
        in_specs=[pl.BlockSpec((tm, K), lambda i, j: (i, 0)),
                  pl.BlockSpec((1, K), lambda i, j: (0, 0)),
                  pl.BlockSpec((K, tn), lambda i, j: (0, j))],
        out_specs=[pl.BlockSpec((tm, tn), lambda i, j: (i, jnp.minimum(j, na - 1))),
                   pl.BlockSpec((tm, tn), lambda i, j: (i, jnp.maximum(j - na, 0)))],
        out_shape=[jax.ShapeDtypeStruct((M, a_cols), a_dtype),
                   jax.ShapeDtypeStruct((M, N - a_cols), F32)],
        scratch_shapes=[pltpu.VMEM((tm, K), BF16)],
        compiler_params=_params("parallel", "arbitrary"),
        name="proj_in",
    )(x, g.reshape(1, K), w)


def _subblock_cumsum(g, sub):
    row = lax.broadcasted_iota(jnp.int32, g.shape, 0) % sub
    s = 1
    while s < sub:
        g = g + jnp.where(row >= s, pltpu.roll(g, s, 0), 0.0)
        s *= 2
    return g


def _gla_kernel(q_ref, f_ref, i_ref, g_ref, lbl_ref, ng_ref, s0_ref, o_ref, so_ref, st_ref, *, C, sub):
    c = pl.program_id(1)
    nsub = C // sub
    mid = sub // 2 - 1

    @pl.when(c == 0)
    def _():
        st_ref[...] = s0_ref[0]

    lbl = lbl_ref[...]
    e = jnp.exp(lbl - jnp.max(lbl, axis=0, keepdims=True))
    lb_all = e[0:1] / jnp.sum(e, axis=0, keepdims=True)
    ng = ng_ref[...]

    row = lax.broadcasted_iota(jnp.int32, (C, C), 0)
    col = lax.broadcasted_iota(jnp.int32, (C, C), 1)
    diag_mask = (row // sub == col // sub) & (row >= col)

    W = HG_HEADS * HG_DK
    q3 = q_ref[...].astype(F32).reshape(nsub, sub, W)
    f = lb_all + (1.0 - lb_all) * _sigmoid(f_ref[...])
    k3 = (1.0 - f).reshape(nsub, sub, W)
    cs = _subblock_cumsum(jnp.log(f), sub).reshape(nsub, sub, W)
    v_all = i_ref[...].astype(BF16)
    silu_g = _silu(g_ref[...].astype(F32))

    T = cs[:, sub - 1:sub, :]
    m = cs[:, mid:mid + 1, :]
    qd = q3 * jnp.exp(cs - m)
    kd = k3 * jnp.exp(m - cs)
    qt = qd * jnp.exp(m)
    kh = kd * jnp.exp(T - m)
    qd_all = qd.reshape(C, W).astype(BF16)
    kd_all = kd.reshape(C, W).astype(BF16)

    PT = [jnp.zeros((1, W), F32)]
    for I in range(nsub):
        PT.append(PT[-1] + T[I])

    def scaled_k(I):
        parts = []
        for J in range(nsub):
            if J < I:
                parts.append(kh[J] * jnp.exp(PT[I] - PT[J + 1]))
            else:
                parts.append(jnp.zeros((sub, W), F32))
        return jnp.concatenate(parts, axis=0).astype(BF16) if nsub > 1 else parts[0].astype(BF16)

    qt_b = [qt[I].astype(BF16) for I in range(nsub)]
    ks_all = [scaled_k(I) for I in range(1, nsub + 1)]
    qc_all = jnp.concatenate([qt[I] * jnp.exp(PT[I]) for I in range(nsub)], axis=0).astype(BF16)
    decay_all = jnp.exp(PT[nsub])

    sls = [slice(h * HG_DK, (h + 1) * HG_DK) for h in range(HG_HEADS)]
    nt = (((1,), (1,)), ((), ()))
    atts = []
    for sl in sls:
        att = jnp.where(diag_mask,
                        lax.dot_general(qd_all[:, sl], kd_all[:, sl], nt, preferred_element_type=F32), 0.0)
        if nsub > 1:
            rows = [jnp.zeros((sub, C), F32)]
            for I in range(1, nsub):
                rows.append(lax.dot_general(qt_b[I][:, sl], ks_all[I - 1][:, sl], nt,
                                            preferred_element_type=F32))
            att = att + jnp.concatenate(rows, axis=0)
        atts.append(att.astype(BF16))

    sts = [st_ref[h] for h in range(HG_HEADS)]
    outs = [jnp.dot(atts[h], v_all[:, sl], preferred_element_type=F32)
            + jnp.dot(qc_all[:, sl], sts[h].astype(BF16), preferred_element_type=F32)
            for h, sl in enumerate(sls)]
    for h, sl in enumerate(sls):
        dcol = jnp.transpose(jnp.broadcast_to(decay_all[:, sl], (8, HG_DK)))[:, 0:1]
        st_ref[h] = (sts[h] * dcol
                     + lax.dot_general(ks_all[-1][:, sl], v_all[:, sl], (((0,), (0,)), ((), ())),
                                       preferred_element_type=F32))
    for h, sl in enumerate(sls):
        o = outs[h]
        o = o * lax.rsqrt(jnp.mean(o * o, axis=-1, keepdims=True) + EPS) * ng * silu_g[:, sl]
        o_ref[:, sl] = o.astype(o_ref.dtype)

    @pl.when(c == pl.num_programs(1) - 1)
    def _():
        so_ref[0] = st_ref[...]


def _gla(za, zf, lb_logits, norm_g, s0, B, L, C):
    nc = L // C
    sub = min(HG_SUB, C)
    W = HG_HEADS * HG_DK

    def zspec(j):
        return pl.BlockSpec((C, W), lambda b, c, j=j: (b * nc + c, j))

    return pl.pallas_call(
        functools.partial(_gla_kernel, C=C, sub=sub),
        grid=(B, nc),
        in_specs=[zspec(_ZA_HQ), zspec(_ZF_HF), zspec(_ZA_HI), zspec(_ZA_HG),
                  pl.BlockSpec((2, W), lambda b, c: (0, 0)),
                  pl.BlockSpec((1, HG_DV), lambda b, c: (0, 0)),
                  pl.BlockSpec((1, HG_HEADS, HG_DK, HG_DV), lambda b, c: (b, 0, 0, 0))],
        out_specs=[pl.BlockSpec((C, W), lambda b, c: (b * nc + c, 0)),
                   pl.BlockSpec((1, HG_HEADS, HG_DK, HG_DV), lambda b, c: (b, 0, 0, 0))],
        out_shape=[jax.ShapeDtypeStruct((B * L, W), za.dtype),
                   jax.ShapeDtypeStruct((B, HG_HEADS, HG_DK, HG_DV), F32)],
        scratch_shapes=[pltpu.VMEM((HG_HEADS, HG_DK, HG_DV), F32)],
        compiler_params=_params("arbitrary", "arbitrary"),
        name="gla",
    )(za, zf, za, za, lb_logits, norm_g.reshape(1, HG_DV), s0)


def _bucket_upper_bounds():
    max_exact = RB_BUCKETS // 2
    d = np.arange(1, 4 * RB_MAX_DIST)
    large = max_exact + (np.log(d.astype(np.float32) / max_exact) / math.log(RB_MAX_DIST / max_exact)
                         * (RB_BUCKETS - max_exact)).astype(np.int32)
    bucket = np.where(d < max_exact, d, np.minimum(large, RB_BUCKETS - 1))
    bucket = np.concatenate([[0], bucket])
    return [int(np.max(np.nonzero(bucket == b)[0])) for b in range(RB_BUCKETS - 1)]


def _bias_kernel(rb_ref, own_ref, prev_ref):
    h = pl.program_id(0)
    i = lax.broadcasted_iota(jnp.int32, (MB_BLOCK, MB_BLOCK), 0)
    j = lax.broadcasted_iota(jnp.int32, (MB_BLOCK, MB_BLOCK), 1)
    dmax = _bucket_upper_bounds()

    def table(d):
        val = jnp.full(d.shape, rb_ref[RB_BUCKETS - 1, h], F32)
        for b in range(RB_BUCKETS - 2, -1, -1):
            val = jnp.where(d <= dmax[b], rb_ref[b, h], val)
        return val

    d = i - j
    own_ref[0] = jnp.where(d >= 0, table(d), NEG)
    prev_ref[0] = table(d + MB_BLOCK)


def _bias_tiles(rel_bias):
    shp = jax.ShapeDtypeStruct((MB_HEADS, MB_BLOCK, MB_BLOCK), F32)
    spec = pl.BlockSpec((1, MB_BLOCK, MB_BLOCK), lambda h: (h, 0, 0))
    return pl.pallas_call(
        _bias_kernel,
        grid=(MB_HEADS,),
        in_specs=[pl.BlockSpec(memory_space=pltpu.SMEM)],
        out_specs=[spec, spec],
        out_shape=[shp, shp],
        compiler_params=_params("arbitrary"),
        name="bias_tiles",
    )(rel_bias)


def _top_select(gate, lane, axis=1):
    sel = jnp.zeros(gate.shape, jnp.bool_)
    lane = lane.astype(F32)
    for _ in range(MB_TOPK):
        m = jnp.max(gate, axis=axis, keepdims=True)
        idx = jnp.min(jnp.where(gate == m, lane, float(2 ** 30)), axis=axis, keepdims=True)
        hit = (lane == idx) & (m > -jnp.inf)
        sel = sel | hit
        gate = jnp.where(hit, -jnp.inf, gate)
    return sel


def _moba_prep_kernel(k_ref, v_ref, ka_ref, vb_ref, mean_ref, *, per_step):
    rows = per_step * MB_BLOCK
    k = k_ref[...]
    block = (lax.broadcasted_iota(jnp.int32, (rows, LANES), 0) // MB_BLOCK
             + pl.program_id(1) * per_step)
    lane = lax.broadcasted_iota(jnp.int32, (rows, LANES), 1)
    onehot = jnp.where(lane == block, 1.0, 0.0).astype(BF16)
    ka_ref[0] = jnp.concatenate([k.astype(BF16), onehot], axis=1)
    vb_ref[0] = jnp.concatenate([v_ref[...].astype(BF16), jnp.ones((rows, LANES), BF16)], axis=1)
    mean_ref[0] = jnp.mean(k.reshape(per_step, MB_BLOCK, MB_HD), axis=1)


def _moba_prep(z, L, kcol, vcol):
    nb = L // MB_BLOCK
    per_step = math.gcd(nb, 8)
    rows = per_step * MB_BLOCK
    return pl.pallas_call(
        functools.partial(_moba_prep_kernel, per_step=per_step),
        grid=(MB_KV_HEADS, nb // per_step),
        in_specs=[pl.BlockSpec((rows, MB_HD), lambda g, n: (n, kcol + g)),
                  pl.BlockSpec((rows, MB_HD), lambda g, n: (n, vcol + g))],
        out_specs=[pl.BlockSpec((1, rows, 2 * MB_HD), lambda g, n: (g, n, 0)),
                   pl.BlockSpec((1, rows, 2 * MB_HD), lambda g, n: (g, n, 0)),
                   pl.BlockSpec((1, per_step, MB_HD), lambda g, n: (g, n, 0))],
        out_shape=[jax.ShapeDtypeStruct((MB_KV_HEADS, L, 2 * MB_HD), BF16),
                   jax.ShapeDtypeStruct((MB_KV_HEADS, L, 2 * MB_HD), BF16),
                   jax.ShapeDtypeStruct((MB_KV_HEADS, nb, MB_HD), F32)],
        compiler_params=_params("arbitrary", "arbitrary"),
        name="moba_prep",
    )(z, z)


def _moba_prompt_kernel(q_ref, ka_ref, va_ref, mean_ref, own_ref, prev_ref, o_ref,
                        qa_ref, m_ref, acc_ref, s_ref):
    i = pl.program_id(1)
    G, B = MB_GROUP, MB_BLOCK
    R = G * B
    scale = MB_HD ** -0.5
    blk = lax.broadcasted_iota(jnp.int32, (LANES, B), 0)
    means = mean_ref[0]

    for j in range(G):
        q = q_ref[:, j * MB_HD:(j + 1) * MB_HD].astype(F32)
        gate = lax.dot_general(means, q, (((1,), (1,)), ((), ())), precision=lax.Precision.HIGHEST,
                               preferred_element_type=F32)
        sel = _top_select(jnp.where(blk < i, gate, -jnp.inf), blk, axis=0)
        b31 = prev_ref[j, 0:1, 0:1]
        mb = jnp.where(sel, jnp.where(blk == i - 1, 0.0, b31), NEG)
        mb = jnp.where(blk == i, 0.0, mb).T
        qa_ref[j * B:(j + 1) * B, :] = jnp.concatenate(
            [(q * (scale * LOG2E)).astype(BF16), (mb * LOG2E).astype(BF16)], axis=1)

    def block_rows(n):
        return pl.ds(pl.multiple_of(n * B, B), B)

    def qk(n):
        return lax.dot_general(qa_ref[...], ka_ref[0, block_rows(n), :], (((1,), (1,)), ((), ())),
                               preferred_element_type=F32)

    def softmax_pv(s, n, first):
        vn = va_ref[0, block_rows(n), :]
        mx = jnp.max(s, axis=1, keepdims=True)
        if first:
            m_new = jnp.broadcast_to(mx, (R, LANES))
            p = jnp.exp2(s - mx)
            acc_ref[...] = jnp.dot(p.astype(BF16), vn, preferred_element_type=F32)
        else:
            m_prev = m_ref[...]
            m_new = jnp.maximum(m_prev, mx)
            alpha = jnp.exp2(m_prev - m_new)
            p = jnp.exp2(s - jnp.concatenate([m_new, m_new], axis=1))
            acc_ref[...] = (jnp.concatenate([alpha, alpha], axis=1) * acc_ref[...]
                            + jnp.dot(p.astype(BF16), vn, preferred_element_type=F32))
        m_ref[...] = m_new

    def own_logits():
        return qk(i) + own_ref[...].reshape(R, B) * LOG2E

    def prev_logits():
        return qk(i - 1) + prev_ref[...].reshape(R, B) * LOG2E

    @pl.when(i == 0)
    def _():
        softmax_pv(own_logits(), i, True)

    @pl.when(i == 1)
    def _():
        softmax_pv(own_logits(), i, True)
        softmax_pv(prev_logits(), i - 1, False)

    @pl.when(i >= 2)
    def _():
        s_own = own_logits()
        s_prev = prev_logits()
        softmax_pv(s_own, i, True)
        s_ref[...] = qk(0)
        softmax_pv(s_prev, i - 1, False)
        last = i - 2

        def run(n0, count):
            s_cur = s_ref[...]
            for u in range(count):
                s_next = qk(jnp.minimum(n0 + u + 1, last))
                softmax_pv(s_cur, n0 + u, False)
                s_cur = s_next
            s_ref[...] = s_cur

        def unrolled(t, carry):
            run(MB_UNROLL * t, MB_UNROLL)
            return carry

        def single(n, carry):
            run(n, 1)
            return carry

        n_main = (i - 1) // MB_UNROLL
        lax.fori_loop(0, n_main, unrolled, 0)
        lax.fori_loop(n_main * MB_UNROLL, i - 1, single, 0)

    acc = acc_ref[...]
    o = acc[:, :MB_HD] / acc[:, MB_HD:]
    for j in range(G):
        o_ref[:, j * MB_HD:(j + 1) * MB_HD] = o[j * B:(j + 1) * B].astype(o_ref.dtype)


def _moba_prompt(z, L, qcol, ka, vb, means, bias_own, bias_prev):
    nb = L // MB_BLOCK
    G, B = MB_GROUP, MB_BLOCK
    means_p = jnp.pad(means.reshape(MB_KV_HEADS, nb, MB_HD), ((0, 0), (0, LANES - nb), (0, 0)))
    return pl.pallas_call(
        _moba_prompt_kernel,
        grid=(MB_KV_HEADS, nb),
        in_specs=[pl.BlockSpec((B, G * MB_HD), lambda g, i: (i, qcol + g)),
                  pl.BlockSpec((1, L, 2 * MB_HD), lambda g, i: (g, 0, 0), pipeline_mode=pl.Buffered(1)),
                  pl.BlockSpec((1, L, 2 * MB_HD), lambda g, i: (g, 0, 0), pipeline_mode=pl.Buffered(1)),
                  pl.BlockSpec((1, LANES, MB_HD), lambda g, i: (g, 0, 0)),
                  pl.BlockSpec((G, B, B), lambda g, i: (g, 0, 0)),
                  pl.BlockSpec((G, B, B), lambda g, i: (g, 0, 0))],
        out_specs=pl.BlockSpec((B, G * MB_HD), lambda g, i: (i, g)),
        out_shape=jax.ShapeDtypeStruct((L, MB_HEADS * MB_HD), z.dtype),
        scratch_shapes=[pltpu.VMEM((G * B, 2 * MB_HD), BF16),
                        pltpu.VMEM((G * B, LANES), F32),
                        pltpu.VMEM((G * B, 2 * MB_HD), F32),
                        pltpu.VMEM((G * B, B), F32)],
        compiler_params=_params("arbitrary", "arbitrary"),
        name="moba_prompt",
    )(z, ka, vb, means_p, bias_own, bias_prev)


def _moba_sample_kernel(pt_ref, q_ref, kn_ref, vn_ref, ck_hbm, cv_hbm, own_ref, adj_ref, b31_ref, o_ref,
                        kbuf, vbuf, lg_ref, sem, *, T, n_pages, n_seq):
    b = pl.program_id(0)
    B = MB_BLOCK
    past = n_pages * PAGE_SIZE
    nb = past // B
    R = MB_HEADS * T
    half = R // MB_KV_HEADS
    scale = MB_HD ** -0.5

    KVH = MB_KV_HEADS
    prow = PAGE_SIZE * KVH

    def page_copies(seq, slot, j):
        dst = pl.ds(pl.multiple_of(j * prow, prow), prow)
        page = pt_ref[seq, j]
        return (pltpu.make_async_copy(ck_hbm.at[page], kbuf.at[slot, dst, :], sem.at[0, slot]),
                pltpu.make_async_copy(cv_hbm.at[page], vbuf.at[slot, dst, :], sem.at[1, slot]))

    def start_seq(seq, slot):
        def body(j, c):
            ck, cv = page_copies(seq, slot, j)
            ck.start()
            cv.start()
            return c
        lax.fori_loop(0, n_pages, body, 0)

    def wait_seq(seq, slot):
        def body(j, c):
            ck, cv = page_copies(seq, slot, j)
            ck.wait()
            cv.wait()
            return c
        lax.fori_loop(0, n_pages, body, 0)

    slot = b % 2

    @pl.when(b == 0)
    def _():
        start_seq(0, 0)

    @pl.when(b + 1 < n_seq)
    def _():
        start_seq(b + 1, 1 - slot)

    wait_seq(b, slot)

    def block(buf, n, g):
        return buf[slot, pl.ds(n * B * KVH + g, B, stride=KVH), :]

    q = q_ref[...]
    Q = jnp.concatenate([q[:, h * MB_HD:(h + 1) * MB_HD] for h in range(MB_HEADS)], axis=0)
    qf = Q * scale
    qs = qf.astype(BF16)
    grow = [slice(g * half, (g + 1) * half) for g in range(KVH)]

    mean_rows = [[] for _ in range(KVH)]
    for n in range(nb):
        for g in range(KVH):
            kf = block(kbuf, n, g)
            mean_rows[g].append(jnp.mean(kf, axis=0, keepdims=True))
            lg_ref[grow[g], n * B:(n + 1) * B] = lax.dot_general(
                qs[grow[g]], kf.astype(BF16), (((1,), (1,)), ((), ())), preferred_element_type=F32)

    gates = []
    for g in range(KVH):
        means = jnp.concatenate(mean_rows[g] + [jnp.zeros((LANES - nb, MB_HD), F32)], axis=0)
        gates.append(lax.dot_general(Q[grow[g]], means, (((1,), (1,)), ((), ())),
                                     precision=lax.Precision.HIGHEST, preferred_element_type=F32))
    gate = jnp.concatenate(gates, axis=0)
    lane = lax.broadcasted_iota(jnp.int32, (R, LANES), 1)
    sel = _top_select(jnp.where(lane < nb, gate, -jnp.inf), lane)
    mb = jnp.where(sel, jnp.where(lane == nb - 1, 0.0, b31_ref[...]), NEG)
    adj = adj_ref[...]

    def logits(n):
        s = lg_ref[:, n * B:(n + 1) * B] + mb[:, n:n + 1]
        return s + adj if n == nb - 1 else s

    kn = kn_ref[...]
    vn = vn_ref[...]
    s_own = jnp.concatenate(
        [lax.dot_general(qf[grow[g]], kn[:, g * MB_HD:(g + 1) * MB_HD], (((1,), (1,)), ((), ())),
                         preferred_element_type=F32) for g in range(KVH)], axis=0) + own_ref[...]

    m = jnp.max(s_own, axis=1, keepdims=True)
    for n in range(nb):
        m = jnp.maximum(m, jnp.max(logits(n), axis=1, keepdims=True))

    p_own = jnp.exp(s_own - m)
    l = jnp.sum(p_own, axis=1, keepdims=True)
    acc = [jnp.dot(p_own[grow[g]], vn[:, g * MB_HD:(g + 1) * MB_HD], preferred_element_type=F32)
           for g in range(KVH)]
    for n in range(nb):
        p = jnp.exp(logits(n) - m)
        l = l + jnp.sum(p, axis=1, keepdims=True)
        pb = p.astype(BF16)
        for g in range(KVH):
            acc[g] = acc[g] + jnp.dot(pb[grow[g]], block(vbuf, n, g).astype(BF16),
                                      preferred_element_type=F32)
    rows = jnp.concatenate(acc, axis=0) / l
    o_ref[...] = jnp.concatenate([rows[h * T:(h + 1) * T] for h in range(MB_HEADS)], axis=1)


def _moba_sample(z, zf, page_table, cache_k, cache_v, bias_own, bias_prev, DB, T, qcol, kcol, vcol):
    n_pages = page_table.shape[1]
    past = n_pages * PAGE_SIZE
    R = MB_HEADS * T
    W = MB_HEADS * MB_HD
    KW = MB_KV_HEADS * MB_HD
    ck = cache_k.reshape(cache_k.shape[0], PAGE_SIZE * MB_KV_HEADS, MB_HD)
    cv = cache_v.reshape(cache_v.shape[0], PAGE_SIZE * MB_KV_HEADS, MB_HD)
    own = bias_own[:, :T, :T].reshape(R, T)
    adj = bias_prev[:, :T, :].reshape(R, MB_BLOCK)
    b31 = bias_prev[:, :T, 0:1].reshape(R, 1)
    const = lambda b, pt: (0, 0)
    grid_spec = pltpu.PrefetchScalarGridSpec(
        num_scalar_prefetch=1,
        grid=(DB,),
        in_specs=[pl.BlockSpec((T, W), lambda b, pt: (b, qcol)),
                  pl.BlockSpec((T, KW), lambda b, pt: (b, kcol)),
                  pl.BlockSpec((T, KW), lambda b, pt: (b, vcol)),
                  pl.BlockSpec(memory_space=pl.ANY),
                  pl.BlockSpec(memory_space=pl.ANY),
                  pl.BlockSpec((R, T), const),
                  pl.BlockSpec((R, MB_BLOCK), const),
                  pl.BlockSpec((R, 1), const)],
        out_specs=pl.BlockSpec((T, W), lambda b, pt: (b, 0)),
        scratch_shapes=[pltpu.VMEM((2, past * MB_KV_HEADS, MB_HD), F32),
                        pltpu.VMEM((2, past * MB_KV_HEADS, MB_HD), F32),
                        pltpu.VMEM((R, past), F32),
                        pltpu.SemaphoreType.DMA((2, 2))],
    )
    return pl.pallas_call(
        functools.partial(_moba_sample_kernel, T=T, n_pages=n_pages, n_seq=DB),
        grid_spec=grid_spec,
        out_shape=jax.ShapeDtypeStruct((DB * T, W), F32),
        compiler_params=_params("arbitrary"),
        name="moba_sample",
    )(page_table, z, zf, zf, ck, cv, own, adj, b31)


def _xblock_math(x, g, wq_ref, k_ref, v_ref, wo_ref):
    q = jnp.dot(_rms(x, g).astype(BF16), wq_ref[...], preferred_element_type=F32)
    hd = k_ref.shape[2]
    scale = hd ** -0.5
    outs = []
    for h in range(MX_HEADS):
        qh = (q[:, h * hd:(h + 1) * hd] * scale).astype(BF16)
        s = lax.dot_general(qh, k_ref[h], (((1,), (1,)), ((), ())), preferred_element_type=F32)
        p = jnp.exp(s - jnp.max(s, axis=1, keepdims=True))
        o = jnp.dot(p.astype(BF16), v_ref[h], preferred_element_type=F32)
        outs.append((o / jnp.sum(p, axis=1, keepdims=True)).astype(BF16))
    return x + jnp.dot(jnp.concatenate(outs, axis=1), wo_ref[...], preferred_element_type=F32)


def _merge_kernel(oa_ref, ob_ref, ga_ref, gb_ref, x_ref, wpa_ref, wpb_ref, wo_ref, *rest):
    o_ref = rest[-1]
    a = jnp.dot(oa_ref[...].astype(BF16), wpa_ref[...], preferred_element_type=F32)
    b = jnp.dot(ob_ref[...].astype(BF16), wpb_ref[...], preferred_element_type=F32)
    merged = _sigmoid(ga_ref[...].astype(F32)) * a + _sigmoid(gb_ref[...].astype(F32)) * b
    h = x_ref[...] + jnp.dot(merged.astype(BF16), wo_ref[...], preferred_element_type=F32)
    if len(rest) > 1:
        gx_ref, wq_ref, k_ref, v_ref, wxo_ref = rest[:-1]
        h = _xblock_math(h, gx_ref[...], wq_ref, k_ref, v_ref, wxo_ref)
    o_ref[...] = h


def _merge(oa, ob, z, gcol, x, wpa, wpb, wo, xblock=None, tm=512):
    M, D = x.shape
    tm = min(tm, M)
    row = lambda i: (i, 0)
    const = lambda i: (0, 0)
    once = pl.Buffered(1)
    wspec = pl.BlockSpec((D, D), const, pipeline_mode=once)
    args = [oa, ob, z, z, x, wpa, wpb, wo]
    specs = [pl.BlockSpec((tm, D), row), pl.BlockSpec((tm, D), row),
             pl.BlockSpec((tm, D), lambda i: (i, gcol)), pl.BlockSpec((tm, D), lambda i: (i, gcol + 1)),
             pl.BlockSpec((tm, D), row), wspec, wspec, wspec]
    if xblock is not None:
        gx, wq, mk, mv, wxo = xblock
        args += [gx.reshape(1, D), wq, mk, mv, wxo]
        specs += [pl.BlockSpec((1, D), const), wspec,
                  pl.BlockSpec(mk.shape, lambda i: (0, 0, 0)), pl.BlockSpec(mv.shape, lambda i: (0, 0, 0)), wspec]
    return pl.pallas_call(
        _merge_kernel,
        grid=(M // tm,),
        in_specs=specs,
        out_specs=pl.BlockSpec((tm, D), row),
        out_shape=jax.ShapeDtypeStruct((M, D), F32),
        compiler_params=_params("parallel"),
        name="merge",
    )(*args)


def _xattn_kernel(q_ref, mk_hbm, mv_hbm, o_ref, kbuf, vbuf, sem, *, n_steps, per_step, L):
    b = pl.program_id(0)
    hd = q_ref.shape[1] // MX_HEADS
    scale = hd ** -0.5
    slot = b % 2

    def head_copies(step, slot):
        cps = []
        for j in range(per_step):
            seq = step * per_step + j
            for h in range(MX_HEADS):
                cps.append(pltpu.make_async_copy(mk_hbm.at[seq, :, h, :], kbuf.at[slot, j, h], sem.at[0, slot]))
                cps.append(pltpu.make_async_copy(mv_hbm.at[seq, :, h, :], vbuf.at[slot, j, h], sem.at[1, slot]))
        return cps

    @pl.when(b == 0)
    def _():
        for cp in head_copies(0, 0):
            cp.start()

    @pl.when(b + 1 < n_steps)
    def _():
        for cp in head_copies(b + 1, 1 - slot):
            cp.start()

    for cp in head_copies(b, slot):
        cp.wait()

    q = (q_ref[...] * scale).astype(BF16)
    pairs = [(j, h) for j in range(per_step) for h in range(MX_HEADS)]
    s = [lax.dot_general(q[j * L:(j + 1) * L, h * hd:(h + 1) * hd], kbuf[slot, j, h].astype(BF16),
                         (((1,), (1,)), ((), ())), preferred_element_type=F32) for j, h in pairs]
    p = [jnp.exp(x - jnp.max(x, axis=1, keepdims=True)) for x in s]
    o = [jnp.dot(p[c].astype(BF16), vbuf[slot, j, h].astype(BF16), preferred_element_type=F32)
         / jnp.sum(p[c], axis=1, keepdims=True) for c, (j, h) in enumerate(pairs)]
    o_ref[...] = jnp.concatenate(
        [jnp.concatenate(o[j * MX_HEADS:(j + 1) * MX_HEADS], axis=1) for j in range(per_step)], axis=0)


def _xattn(q, mk, mv, B, L, per_step=4):
    D = q.shape[1]
    mem, hd = mk.shape[1], mk.shape[3]
    per_step = math.gcd(per_step, B)
    n_steps = B // per_step
    rows = per_step * L
    return pl.pallas_call(
        functools.partial(_xattn_kernel, n_steps=n_steps, per_step=per_step, L=L),
        grid=(n_steps,),
        in_specs=[pl.BlockSpec((rows, D), lambda b: (b, 0)),
                  pl.BlockSpec(memory_space=pl.ANY),
                  pl.BlockSpec(memory_space=pl.ANY)],
        out_specs=pl.BlockSpec((rows, D), lambda b: (b, 0)),
        out_shape=jax.ShapeDtypeStruct((B * L, D), F32),
        scratch_shapes=[pltpu.VMEM((2, per_step, MX_HEADS, mem, hd), F32),
                        pltpu.VMEM((2, per_step, MX_HEADS, mem, hd), F32),
                        pltpu.SemaphoreType.DMA((2, 2))],
        compiler_params=_params("arbitrary"),
        name="xattn",
    )(q, mk, mv)


def _ffn_kernel(x_ref, g_ref, wg_ref, wu_ref, wd_ref, fg_ref, o_ref, *, tf):
    x = x_ref[...]
    xn = _rms(x, g_ref[...]).astype(BF16)
    acc = x
    for c in range(wg_ref.shape[1] // tf):
        sl = slice(c * tf, (c + 1) * tf)
        a = jnp.dot(xn, wg_ref[:, sl], preferred_element_type=F32)
        u = jnp.dot(xn, wu_ref[:, sl], preferred_element_type=F32)
        acc = acc + jnp.dot((_silu(a) * u).astype(BF16), wd_ref[sl, :], preferred_element_type=F32)
    o_ref[...] = _rms(acc, fg_ref[...])


def _ffn(x, g, wg, wu, wd, fg, tm=1024, tf=256):
    M, D = x.shape
    FF = wg.shape[1]
    tm = min(tm, M)
    row = lambda i: (i, 0)
    const = lambda i: (0, 0)
    once = pl.Buffered(1)
    return pl.pallas_call(
        functools.partial(_ffn_kernel, tf=tf),
        grid=(M // tm,),
        in_specs=[pl.BlockSpec((tm, D), row), pl.BlockSpec((1, D), const),
                  pl.BlockSpec((D, FF), const, pipeline_mode=once),
                  pl.BlockSpec((D, FF), const, pipeline_mode=once),
                  pl.BlockSpec((FF, D), const, pipeline_mode=once),
                  pl.BlockSpec((1, D), const)],
        out_specs=pl.BlockSpec((tm, D), row),
        out_shape=jax.ShapeDtypeStruct((M, D), F32),
        compiler_params=_params("parallel"),
        name="ffn",
    )(x, g.reshape(1, D), wg, wu, wd, fg.reshape(1, D))


def _mixer(x2, B, L, chunk, s0, moba_fn, W, za_dtype, xblock=None):
    za, zf = _proj_in(x2, W["norm_mix_g"], W["w_in"], W["a_cols"], za_dtype, tm=2048, tn=768)
    oa, S = _gla(za, zf, W["lb_logits"], W["hg_norm_g"], s0, B, L, chunk)
    ob = moba_fn(za, zf)
    h = _merge(oa, ob, za, _ZA_GA, x2, W["w_pa"], W["w_pb"], W["w_out"], xblock=xblock)
    return h, S, zf


def kernel(x_prompt, x_sample, cache_k, cache_v, cache_mem_k, cache_mem_v, state_hgrn, page_table, mem_prompt, norm_mix_g, w_in, hg_lb_logits, hg_norm_g, w_proj_a, w_proj_b, w_out, rel_bias, norm_x_g, w_xq, w_mk, w_mv, w_xo, norm_ffn_g, w_gate, w_up, w_down, final_norm_g):
    Bp, Lp, D = x_prompt.shape
    DB, T, _ = x_sample.shape
    l = 0
    win = w_in[l]
    hw = HG_HEADS * HG_DK
    hq, hf, hi, hg = (win[:, j * hw:(j + 1) * hw] for j in range(4))
    c0 = 4 * hw + MB_HEADS * MB_HD
    c1 = c0 + 2 * MB_KV_HEADS * MB_HD
    W = {
        "w_in": jnp.concatenate([hq, hi, hg, win[:, 4 * hw:c0], win[:, c1:], hf, win[:, c0:c1]],
                                axis=1).astype(BF16),
        "a_cols": win.shape[1] - hw - (c1 - c0),
        "norm_mix_g": norm_mix_g[l],
        "lb_logits": hg_lb_logits,
        "hg_norm_g": hg_norm_g[l],
        "w_pa": w_proj_a[l].astype(BF16), "w_pb": w_proj_b[l].astype(BF16), "w_out": w_out[l].astype(BF16),
    }
    wxq, wxo = w_xq[l].astype(BF16), w_xo[l].astype(BF16)
    wmkv = jnp.concatenate([w_mk[l], w_mv[l]], axis=1).astype(BF16)
    wg, wu, wd = w_gate[l].astype(BF16), w_up[l].astype(BF16), w_down[l].astype(BF16)

    bias_own, bias_prev = _bias_tiles(rel_bias)

    xp = x_prompt.reshape(Bp * Lp, D)

    assert Bp == 1, "the prompt kernels take one sequence"
    kw = MB_KV_HEADS * MB_HD

    def new_kv(zf, B, L):
        return (zf[:, _ZF_K * kw:(_ZF_K + 1) * kw].reshape(B, L, MB_KV_HEADS, MB_HD),
                zf[:, _ZF_V * kw:(_ZF_V + 1) * kw].reshape(B, L, MB_KV_HEADS, MB_HD))

    def moba_p(za, zf):
        ka, va, means = _moba_prep(zf, Lp, 2 * _ZF_K, 2 * _ZF_V)
        return _moba_prompt(za, Lp, 2 * _ZA_MQ, ka, va, means, bias_own, bias_prev)

    mem = mem_prompt.reshape(Bp * mem_prompt.shape[1], D)
    mkv = _matmul(mem, wmkv, tm=256, tn=512)
    mshape = (Bp, mem_prompt.shape[1], MX_HEADS, D // MX_HEADS)
    mkp, mvp = mkv[:, :D].reshape(mshape), mkv[:, D:].reshape(mshape)
    xblock = (norm_x_g[l], wxq, mkp[0].swapaxes(0, 1).astype(BF16), mvp[0].swapaxes(0, 1).astype(BF16), wxo)

    s0 = jnp.zeros((Bp, HG_HEADS, HG_DK, HG_DV), F32)
    hp, Sp, zfp = _mixer(xp, Bp, Lp, math.gcd(Lp, 64), s0, moba_p, W, BF16, xblock=xblock)
    kp, vp = new_kv(zfp, Bp, Lp)
    yp = _ffn(hp, norm_ffn_g[l], wg, wu, wd, final_norm_g)

    xs = x_sample.reshape(DB * T, D)

    def moba_s(za, zf):
        return _moba_sample(za, zf, page_table, cache_k[l], cache_v[l], bias_own, bias_prev, DB, T,
                            _ZA_MQ, _ZF_K, _ZF_V)

    hs, Ss, zfs = _mixer(xs, DB, T, T, state_hgrn[l], moba_s, W, F32)
    ksn, vsn = new_kv(zfs, DB, T)
    qxs = _matmul(hs, wxq, g=norm_x_g[l], tn=512)
    oxs = _xattn(qxs, cache_mem_k[l], cache_mem_v[l], DB, T)
    hs = _matmul(oxs, wxo, res=hs, tn=512)
    ys = _ffn(hs, norm_ffn_g[l], wg, wu, wd, final_norm_g)

    return (yp.reshape(Bp, Lp, D), ys.reshape(DB, T, D),
            Sp[None].astype(state_hgrn.dtype), kp[None], vp[None],
            mkp[None], mvp[None],
            Ss[None].astype(state_hgrn.dtype), ksn[None], vsn[None])
```

```python
import functools
import math

import jax
import jax.numpy as jnp
import numpy as np
from jax import lax
from jax.experimental import pallas as pl
from jax.experimental.pallas import tpu as pltpu

F32 = jnp.float32
BF16 = jnp.bfloat16

EPS = 1e-6
HG_HEADS = 8
HG_DK = 128
HG_DV = 128
HG_SUB = 16
MB_HEADS = 8
MB_KV_HEADS = 2
MB_GROUP = MB_HEADS // MB_KV_HEADS
MB_HD = 128
MB_BLOCK = 256
MB_TOPK = 3
MB_UNROLL = 8
RB_BUCKETS = 32
RB_MAX_DIST = 128
MX_HEADS = 4
PAGE_SIZE = 128
NEG = -1e30
LOG2E = math.log2(math.e)

_ZA_HQ, _ZA_HI, _ZA_HG, _ZA_MQ, _ZA_GA = 0, 1, 2, 3, 4
_ZF_HF = 0
_ZF_K, _ZF_V = 4, 5
LANES = 128
VMEM_LIMIT = 56 * 1024 * 1024


def _params(*sem):
    return pltpu.CompilerParams(dimension_semantics=sem, vmem_limit_bytes=VMEM_LIMIT)


def _sigmoid(x):
    return 1.0 / (1.0 + jnp.exp(-x))


def _silu(x):
    return x * _sigmoid(x)


def _rms(x, g):
    return x * lax.rsqrt(jnp.mean(x * x, axis=-1, keepdims=True) + EPS) * g


def _matmul_kernel(*refs, norm, residual):
    refs = list(refs)
    x_ref = refs.pop(0)
    g_ref = refs.pop(0) if norm else None
    w_ref = refs.pop(0)
    r_ref = refs.pop(0) if residual else None
    o_ref, xs_ref = refs

    @pl.when(pl.program_id(1) == 0)
    def _():
        x = x_ref[...]
        if norm:
            x = _rms(x, g_ref[...])
        xs_ref[...] = x.astype(BF16)

    acc = jnp.dot(xs_ref[...], w_ref[...], preferred_element_type=F32)
    if residual:
        acc = acc + r_ref[...]
    o_ref[...] = acc.astype(o_ref.dtype)


def _matmul(x, w, g=None, res=None, tm=512, tn=512, out_dtype=F32):
    M, K = x.shape
    N = w.shape[1]
    tm = min(tm, M)
    tn = min(tn, N)
    assert M % tm == 0 and N % tn == 0
    args = [x]
    specs = [pl.BlockSpec((tm, K), lambda i, j: (i, 0))]
    if g is not None:
        args.append(g.reshape(1, K))
        specs.append(pl.BlockSpec((1, K), lambda i, j: (0, 0)))
    args.append(w)
    specs.append(pl.BlockSpec((K, tn), lambda i, j: (0, j)))
    if res is not None:
        args.append(res)
        specs.append(pl.BlockSpec((tm, tn), lambda i, j: (i, j)))
    return pl.pallas_call(
        functools.partial(_matmul_kernel, norm=g is not None, residual=res is not None),
        grid=(M // tm, N // tn),
        in_specs=specs,
        out_specs=pl.BlockSpec((tm, tn), lambda i, j: (i, j)),
        out_shape=jax.ShapeDtypeStruct((M, N), out_dtype),
        scratch_shapes=[pltpu.VMEM((tm, K), BF16)],
        compiler_params=_params("parallel", "arbitrary"),
        name="matmul",
    )(*args)


def _proj_in_kernel(x_ref, g_ref, w_ref, oa_ref, of_ref, xs_ref, *, na):
    j = pl.program_id(1)

    @pl.when(j == 0)
    def _():
        xs_ref[...] = _rms(x_ref[...], g_ref[...]).astype(BF16)

    acc = jnp.dot(xs_ref[...], w_ref[...], preferred_element_type=F32)

    @pl.when(j < na)
    def _():
        oa_ref[...] = acc.astype(oa_ref.dtype)

    @pl.when(j >= na)
    def _():
        of_ref[...] = acc


def _proj_in(x, g, w, a_cols, a_dtype, tm, tn=512):
    M, K = x.shape
    N = w.shape[1]
    tm = min(tm, M)
    na, nf = a_cols // tn, (N - a_cols) // tn
    assert M % tm == 0 and na * tn == a_cols and (na + nf) * tn == N
    return pl.pallas_call(
        functools.partial(_proj_in_kernel, na=na),
        grid=(M // tm, na + nf),
        in_specs=[pl.BlockSpec((tm, K), lambda i, j: (i, 0)),
                  pl.BlockSpec((1, K), lambda i, j: (0, 0)),
                  pl.BlockSpec((K, tn), lambda i, j: (0, j))],
        out_specs=[pl.BlockSpec((tm, tn), lambda i, j: (i, jnp.minimum(j, na - 1))),
                   pl.BlockSpec((tm, tn), lambda i, j: (i, jnp.maximum(j - na, 0)))],
        out_shape=[jax.ShapeDtypeStruct((M, a_cols), a_dtype),
                   jax.ShapeDtypeStruct((M, N - a_cols), F32)],
        scratch_shapes=[pltpu.VMEM((tm, K), BF16)],
        compiler_params=_params("parallel", "arbitrary"),
        name="proj_in",
    )(x, g.reshape(1, K), w)


def _subblock_cumsum(g, sub):
    row = lax.broadcasted_iota(jnp.int32, g.shape, 0) % sub
    s = 1
    while s < sub:
        g = g + jnp.where(row >= s, pltpu.roll(g, s, 0), 0.0)
        s *= 2
    return g


def _gla_kernel(q_ref, f_ref, i_ref, g_ref, lbl_ref, ng_ref, s0_ref, o_ref, so_ref, st_ref, *, C, sub):
    c = pl.program_id(1)
    nsub = C // sub
    mid = sub // 2 - 1

    @pl.when(c == 0)
    def _():
        st_ref[...] = s0_ref[0]

    lbl = lbl_ref[...]
    e = jnp.exp(lbl - jnp.max(lbl, axis=0, keepdims=True))
    lb_all = e[0:1] / jnp.sum(e, axis=0, keepdims=True)
    ng = ng_ref[...]

    row = lax.broadcasted_iota(jnp.int32, (C, C), 0)
    col = lax.broadcasted_iota(jnp.int32, (C, C), 1)
    diag_mask = (row // sub == col // sub) & (row >= col)

    W = HG_HEADS * HG_DK
    q3 = q_ref[...].astype(F32).reshape(nsub, sub, W)
    f = lb_all + (1.0 - lb_all) * _sigmoid(f_ref[...])
    k3 = (1.0 - f).reshape(nsub, sub, W)
    cs = _subblock_cumsum(jnp.log(f), sub).reshape(nsub, sub, W)
    v_all = i_ref[...].astype(BF16)
    silu_g = _silu(g_ref[...].astype(F32))

    T = cs[:, sub - 1:sub, :]
    m = cs[:, mid:mid + 1, :]
    qd = q3 * jnp.exp(cs - m)
    kd = k3 * jnp.exp(m - cs)
    qt = qd * jnp.exp(m)
    kh = kd * jnp.exp(T - m)
    qd_all = qd.reshape(C, W).astype(BF16)
    kd_all = kd.reshape(C, W).astype(BF16)

    PT = [jnp.zeros((1, W), F32)]
    for I in range(nsub):
        PT.append(PT[-1] + T[I])

    def scaled_k(I):
        parts = []
        for J in range(nsub):
            if J < I:
                parts.append(kh[J] * jnp.exp(PT[I] - PT[J + 1]))
            else:
                parts.append(jnp.zeros((sub, W), F32))
        return jnp.concatenate(parts, axis=0).astype(BF16) if nsub > 1 else parts[0].astype(BF16)

    qt_b = [qt[I].astype(BF16) for I in range(nsub)]
    ks_all = [scaled_k(I) for I in range(1, nsub + 1)]
    qc_all = jnp.concatenate([qt[I] * jnp.exp(PT[I]) for I in range(nsub)], axis=0).astype(BF16)
    decay_all = jnp.exp(PT[nsub])

    sls = [slice(h * HG_DK, (h + 1) * HG_DK) for h in range(HG_HEADS)]
    nt = (((1,), (1,)), ((), ()))
    atts = []
    for sl in sls:
        att = jnp.where(diag_mask,
                        lax.dot_general(qd_all[:, sl], kd_all[:, sl], nt, preferred_element_type=F32), 0.0)
        if nsub > 1:
            rows = [jnp.zeros((sub, C), F32)]
            for I in range(1, nsub):
                rows.append(lax.dot_general(qt_b[I][:, sl], ks_all[I - 1][:, sl], nt,
                                            preferred_element_type=F32))
            att = att + jnp.concatenate(rows, axis=0)
        atts.append(att.astype(BF16))

    sts = [st_ref[h] for h in range(HG_HEADS)]
    outs = [jnp.dot(atts[h], v_all[:, sl], preferred_element_type=F32)
            + jnp.dot(qc_all[:, sl], sts[h].astype(BF16), preferred_element_type=F32)
            for h, sl in enumerate(sls)]
    for h, sl in enumerate(sls):
        dcol = jnp.transpose(jnp.broadcast_to(decay_all[:, sl], (8, HG_DK)))[:, 0:1]
        st_ref[h] = (sts[h] * dcol
                     + lax.dot_general(ks_all[-1][:, sl], v_all[:, sl], (((0,), (0,)), ((), ())),
                                       preferred_element_type=F32))
    for h, sl in enumerate(sls):
        o = outs[h]
        o = o * lax.rsqrt(jnp.mean(o * o, axis=-1, keepdims=True) + EPS) * ng * silu_g[:, sl]
        o_ref[:, sl] = o.astype(o_ref.dtype)

    @pl.when(c == pl.num_programs(1) - 1)
    def _():
        so_ref[0] = st_ref[...]


def _gla(za, zf, lb_logits, norm_g, s0, B, L, C):
    nc = L // C
    sub = min(HG_SUB, C)
    W = HG_HEADS * HG_DK

    def zspec(j):
        return pl.BlockSpec((C, W), lambda b, c, j=j: (b * nc + c, j))

    return pl.pallas_call(
        functools.partial(_gla_kernel, C=C, sub=sub),
        grid=(B, nc),
        in_specs=[zspec(_ZA_HQ), zspec(_ZF_HF), zspec(_ZA_HI), zspec(_ZA_HG),
                  pl.BlockSpec((2, W), lambda b, c: (0, 0)),
                  pl.BlockSpec((1, HG_DV), lambda b, c: (0, 0)),
                  pl.BlockSpec((1, HG_HEADS, HG_DK, HG_DV), lambda b, c: (b, 0, 0, 0))],
        out_specs=[pl.BlockSpec((C, W), lambda b, c: (b * nc + c, 0)),
                   pl.BlockSpec((1, HG_HEADS, HG_DK, HG_DV), lambda b, c: (b, 0, 0, 0))],
        out_shape=[jax.ShapeDtypeStruct((B * L, W), za.dtype),
                   jax.ShapeDtypeStruct((B, HG_HEADS, HG_DK, HG_DV), F32)],
        scratch_shapes=[pltpu.VMEM((HG_HEADS, HG_DK, HG_DV), F32)],
        compiler_params=_params("arbitrary", "arbitrary"),
        name="gla",
    )(za, zf, za, za, lb_logits, norm_g.reshape(1, HG_DV), s0)


def _bucket_upper_bounds():
    max_exact = RB_BUCKETS // 2
    d = np.arange(1, 4 * RB_MAX_DIST)
    large = max_exact + (np.log(d.astype(np.float32) / max_exact) / math.log(RB_MAX_DIST / max_exact)
                         * (RB_BUCKETS - max_exact)).astype(np.int32)
    bucket = np.where(d < max_exact, d, np.minimum(large, RB_BUCKETS - 1))
    bucket = np.concatenate([[0], bucket])
    return [int(np.max(np.nonzero(bucket == b)[0])) for b in range(RB_BUCKETS - 1)]


def _bias_kernel(rb_ref, own_ref, prev_ref):
    h = pl.program_id(0)
    i = lax.broadcasted_iota(jnp.int32, (MB_BLOCK, MB_BLOCK), 0)
    j = lax.broadcasted_iota(jnp.int32, (MB_BLOCK, MB_BLOCK), 1)
    dmax = _bucket_upper_bounds()

    def table(d):
        val = jnp.full(d.shape, rb_ref[RB_BUCKETS - 1, h], F32)
        for b in range(RB_BUCKETS - 2, -1, -1):
            val = jnp.where(d <= dmax[b], rb_ref[b, h], val)
        return val

    d = i - j
    own_ref[0] = jnp.where(d >= 0, table(d), NEG)
    prev_ref[0] = table(d + MB_BLOCK)


def _bias_tiles(rel_bias):
    shp = jax.ShapeDtypeStruct((MB_HEADS, MB_BLOCK, MB_BLOCK), F32)
    spec = pl.BlockSpec((1, MB_BLOCK, MB_BLOCK), lambda h: (h, 0, 0))
    return pl.pallas_call(
        _bias_kernel,
        grid=(MB_HEADS,),
        in_specs=[pl.BlockSpec(memory_space=pltpu.SMEM)],
        out_specs=[spec, spec],
        out_shape=[shp, shp],
        compiler_params=_params("arbitrary"),
        name="bias_tiles",
    )(rel_bias)


def _top_select(gate, lane, axis=1):
    sel = jnp.zeros(gate.shape, jnp.bool_)
    lane = lane.astype(F32)
    for _ in range(MB_TOPK):
        m = jnp.max(gate, axis=axis, keepdims=True)
        idx = jnp.min(jnp.where(gate == m, lane, float(2 ** 30)), axis=axis, keepdims=True)
        hit = (lane == idx) & (m > -jnp.inf)
        sel = sel | hit
        gate = jnp.where(hit, -jnp.inf, gate)
    return sel


def _moba_prep_kernel(k_ref, v_ref, ka_ref, vb_ref, mean_ref, ko_ref, vo_ref, *, per_step):
    rows = per_step * MB_BLOCK
    block = (lax.broadcasted_iota(jnp.int32, (rows, LANES), 0) // MB_BLOCK
             + pl.program_id(0) * per_step)
    lane = lax.broadcasted_iota(jnp.int32, (rows, LANES), 1)
    onehot = jnp.where(lane == block, 1.0, 0.0).astype(BF16)
    ones = jnp.ones((rows, LANES), BF16)
    for g in range(MB_KV_HEADS):
        k = k_ref[:, g * MB_HD:(g + 1) * MB_HD]
        v = v_ref[:, g * MB_HD:(g + 1) * MB_HD]
        ka_ref[g] = jnp.concatenate([k.astype(BF16), onehot], axis=1)
        vb_ref[g] = jnp.concatenate([v.astype(BF16), ones], axis=1)
        mean_ref[g] = jnp.mean(k.reshape(per_step, MB_BLOCK, MB_HD), axis=1)
        ko_ref[pl.ds(g, rows, stride=MB_KV_HEADS), :] = k
        vo_ref[pl.ds(g, rows, stride=MB_KV_HEADS), :] = v


def _moba_prep(z, L, kcol, vcol):
    nb = L // MB_BLOCK
    per_step = math.gcd(nb, 8)
    rows = per_step * MB_BLOCK
    KVH = MB_KV_HEADS
    return pl.pallas_call(
        functools.partial(_moba_prep_kernel, per_step=per_step),
        grid=(nb // per_step,),
        in_specs=[pl.BlockSpec((rows, KVH * MB_HD), lambda n: (n, kcol)),
                  pl.BlockSpec((rows, KVH * MB_HD), lambda n: (n, vcol))],
        out_specs=[pl.BlockSpec((KVH, rows, 2 * MB_HD), lambda n: (0, n, 0)),
                   pl.BlockSpec((KVH, rows, 2 * MB_HD), lambda n: (0, n, 0)),
                   pl.BlockSpec((KVH, per_step, MB_HD), lambda n: (0, n, 0)),
                   pl.BlockSpec((rows * KVH, MB_HD), lambda n: (n, 0)),
                   pl.BlockSpec((rows * KVH, MB_HD), lambda n: (n, 0))],
        out_shape=[jax.ShapeDtypeStruct((KVH, L, 2 * MB_HD), BF16),
                   jax.ShapeDtypeStruct((KVH, L, 2 * MB_HD), BF16),
                   jax.ShapeDtypeStruct((KVH, nb, MB_HD), F32),
                   jax.ShapeDtypeStruct((L * KVH, MB_HD), F32),
                   jax.ShapeDtypeStruct((L * KVH, MB_HD), F32)],
        compiler_params=_params("arbitrary"),
        name="moba_prep",
    )(z, z)


def _moba_prompt_kernel(q_ref, ka_ref, va_ref, mean_ref, own_ref, prev_ref, o_ref,
                        qa_ref, m_ref, acc_ref, s_ref):
    i = pl.program_id(1)
    G, B = MB_GROUP, MB_BLOCK
    R = G * B
    scale = MB_HD ** -0.5
    blk = lax.broadcasted_iota(jnp.int32, (LANES, B), 0)
    means = mean_ref[0]

    for j in range(G):
        q = q_ref[:, j * MB_HD:(j + 1) * MB_HD].astype(F32)
        gate = lax.dot_general(means, q, (((1,), (1,)), ((), ())), precision=lax.Precision.HIGHEST,
                               preferred_element_type=F32)
        sel = _top_select(jnp.where(blk < i, gate, -jnp.inf), blk, axis=0)
        b31 = prev_ref[j, 0:1, 0:1]
        mb = jnp.where(sel, jnp.where(blk == i - 1, 0.0, b31), NEG)
        mb = jnp.where(blk == i, 0.0, mb).T
        qa_ref[j * B:(j + 1) * B, :] = jnp.concatenate(
            [(q * (scale * LOG2E)).astype(BF16), (mb * LOG2E).astype(BF16)], axis=1)

    def block_rows(n):
        return pl.ds(pl.multiple_of(n * B, B), B)

    def qk(n):
        return lax.dot_general(qa_ref[...], ka_ref[0, block_rows(n), :], (((1,), (1,)), ((), ())),
                               preferred_element_type=F32)

    def softmax_pv(s, n, first):
        vn = va_ref[0, block_rows(n), :]
        mx = jnp.max(s, axis=1, keepdims=True)
        if first:
            m_new = jnp.broadcast_to(mx, (R, LANES))
            p = jnp.exp2(s - mx)
            acc_ref[...] = jnp.dot(p.astype(BF16), vn, preferred_element_type=F32)
        else:
            m_prev = m_ref[...]
            m_new = jnp.maximum(m_prev, mx)
            alpha = jnp.exp2(m_prev - m_new)
            p = jnp.exp2(s - jnp.concatenate([m_new, m_new], axis=1))
            acc_ref[...] = (jnp.concatenate([alpha, alpha], axis=1) * acc_ref[...]
                            + jnp.dot(p.astype(BF16), vn, preferred_element_type=F32))
        m_ref[...] = m_new

    def own_logits():
        return qk(i) + own_ref[...].reshape(R, B) * LOG2E

    def prev_logits():
        return qk(i - 1) + prev_ref[...].reshape(R, B) * LOG2E

    @pl.when(i == 0)
    def _():
        softmax_pv(own_logits(), i, True)

    @pl.when(i == 1)
    def _():
        softmax_pv(own_logits(), i, True)
        softmax_pv(prev_logits(), i - 1, False)

    @pl.when(i >= 2)
    def _():
        s_own = own_logits()
        s_prev = prev_logits()
        softmax_pv(s_own, i, True)
        s_ref[...] = qk(0)
        softmax_pv(s_prev, i - 1, False)
        last = i - 2

        def run(n0, count):
            s_cur = s_ref[...]
            for u in range(count):
                s_next = qk(jnp.minimum(n0 + u + 1, last))
                softmax_pv(s_cur, n0 + u, False)
                s_cur = s_next
            s_ref[...] = s_cur

        def unrolled(t, carry):
            run(MB_UNROLL * t, MB_UNROLL)
            return carry

        total = i - 1
        n_main = total // MB_UNROLL
        lax.fori_loop(0, n_main, unrolled, 0)

        base = n_main * MB_UNROLL
        piece = MB_UNROLL // 2
        while piece >= 1:
            take = (total & piece) != 0

            @pl.when(take)
            def _(base=base, piece=piece):
                run(base, piece)

            base = base + jnp.where(take, piece, 0)
            piece //= 2

    acc = acc_ref[...]
    o = acc[:, :MB_HD] / acc[:, MB_HD:]
    for j in range(G):
        o_ref[:, j * MB_HD:(j + 1) * MB_HD] = o[j * B:(j + 1) * B].astype(o_ref.dtype)


def _moba_prompt(z, L, qcol, ka, vb, means, bias_own, bias_prev):
    nb = L // MB_BLOCK
    G, B = MB_GROUP, MB_BLOCK
    means_p = jnp.pad(means.reshape(MB_KV_HEADS, nb, MB_HD), ((0, 0), (0, LANES - nb), (0, 0)))
    return pl.pallas_call(
        _moba_prompt_kernel,
        grid=(MB_KV_HEADS, nb),
        in_specs=[pl.BlockSpec((B, G * MB_HD), lambda g, i: (i, qcol + g)),
                  pl.BlockSpec((1, L, 2 * MB_HD), lambda g, i: (g, 0, 0), pipeline_mode=pl.Buffered(1)),
                  pl.BlockSpec((1, L, 2 * MB_HD), lambda g, i: (g, 0, 0), pipeline_mode=pl.Buffered(1)),
                  pl.BlockSpec((1, LANES, MB_HD), lambda g, i: (g, 0, 0)),
                  pl.BlockSpec((G, B, B), lambda g, i: (g, 0, 0)),
                  pl.BlockSpec((G, B, B), lambda g, i: (g, 0, 0))],
        out_specs=pl.BlockSpec((B, G * MB_HD), lambda g, i: (i, g)),
        out_shape=jax.ShapeDtypeStruct((L, MB_HEADS * MB_HD), z.dtype),
        scratch_shapes=[pltpu.VMEM((G * B, 2 * MB_HD), BF16),
                        pltpu.VMEM((G * B, LANES), F32),
                        pltpu.VMEM((G * B, 2 * MB_HD), F32),
                        pltpu.VMEM((G * B, B), F32)],
        compiler_params=_params("arbitrary", "arbitrary"),
        name="moba_prompt",
    )(z, ka, vb, means_p, bias_own, bias_prev)


def _moba_sample_kernel(pt_ref, q_ref, kn_ref, vn_ref, ck_hbm, cv_hbm, own_ref, adj_ref, b31_ref, o_ref,
                        kbuf, vbuf, lg_ref, sem, *, T, n_pages, n_seq):
    b = pl.program_id(0)
    B = MB_BLOCK
    past = n_pages * PAGE_SIZE
    nb = past // B
    R = MB_HEADS * T
    half = R // MB_KV_HEADS
    scale = MB_HD ** -0.5

    KVH = MB_KV_HEADS
    prow = PAGE_SIZE * KVH

    def page_copies(seq, slot, j):
        dst = pl.ds(pl.multiple_of(j * prow, prow), prow)
        page = pt_ref[seq, j]
        return (pltpu.make_async_copy(ck_hbm.at[page], kbuf.at[slot, dst, :], sem.at[0, slot]),
                pltpu.make_async_copy(cv_hbm.at[page], vbuf.at[slot, dst, :], sem.at[1, slot]))

    def start_seq(seq, slot):
        def body(j, c):
            ck, cv = page_copies(seq, slot, j)
            ck.start()
            cv.start()
            return c
        lax.fori_loop(0, n_pages, body, 0)

    def wait_seq(seq, slot):
        def body(j, c):
            ck, cv = page_copies(seq, slot, j)
            ck.wait()
            cv.wait()
            return c
        lax.fori_loop(0, n_pages, body, 0)

    slot = b % 2

    @pl.when(b == 0)
    def _():
        start_seq(0, 0)

    @pl.when(b + 1 < n_seq)
    def _():
        start_seq(b + 1, 1 - slot)

    wait_seq(b, slot)

    def block(buf, n, g):
        return buf[slot, pl.ds(n * B * KVH + g, B, stride=KVH), :]

    q = q_ref[...]
    Q = jnp.concatenate([q[:, h * MB_HD:(h + 1) * MB_HD] for h in range(MB_HEADS)], axis=0)
    qf = Q * scale
    qs = qf.astype(BF16)
    grow = [slice(g * half, (g + 1) * half) for g in range(KVH)]

    mean_rows = [[] for _ in range(KVH)]
    for n in range(nb):
        for g in range(KVH):
            kf = block(kbuf, n, g)
            mean_rows[g].append(jnp.mean(kf, axis=0, keepdims=True))
            lg_ref[grow[g], n * B:(n + 1) * B] = lax.dot_general(
                qs[grow[g]], kf.astype(BF16), (((1,), (1,)), ((), ())), preferred_element_type=F32)

    gates = []
    for g in range(KVH):
        means = jnp.concatenate(mean_rows[g] + [jnp.zeros((LANES - nb, MB_HD), F32)], axis=0)
        gates.append(lax.dot_general(Q[grow[g]], means, (((1,), (1,)), ((), ())),
                                     precision=lax.Precision.HIGHEST, preferred_element_type=F32))
    gate = jnp.concatenate(gates, axis=0)
    lane = lax.broadcasted_iota(jnp.int32, (R, LANES), 1)
    sel = _top_select(jnp.where(lane < nb, gate, -jnp.inf), lane)
    mb = jnp.where(sel, jnp.where(lane == nb - 1, 0.0, b31_ref[...]), NEG)
    adj = adj_ref[...]

    def logits(n):
        s = lg_ref[:, n * B:(n + 1) * B] + mb[:, n:n + 1]
        return s + adj if n == nb - 1 else s

    kn = kn_ref[...]
    vn = vn_ref[...]
    s_own = jnp.concatenate(
        [lax.dot_general(qf[grow[g]], kn[:, g * MB_HD:(g + 1) * MB_HD], (((1,), (1,)), ((), ())),
                         preferred_element_type=F32) for g in range(KVH)], axis=0) + own_ref[...]

    m = jnp.max(s_own, axis=1, keepdims=True)
    for n in range(nb):
        m = jnp.maximum(m, jnp.max(logits(n), axis=1, keepdims=True))

    p_own = jnp.exp(s_own - m)
    l = jnp.sum(p_own, axis=1, keepdims=True)
    acc = [jnp.dot(p_own[grow[g]], vn[:, g * MB_HD:(g + 1) * MB_HD], preferred_element_type=F32)
           for g in range(KVH)]
    for n in range(nb):
        p = jnp.exp(logits(n) - m)
        l = l + jnp.sum(p, axis=1, keepdims=True)
        pb = p.astype(BF16)
        for g in range(KVH):
            acc[g] = acc[g] + jnp.dot(pb[grow[g]], block(vbuf, n, g).astype(BF16),
                                      preferred_element_type=F32)
    rows = jnp.concatenate(acc, axis=0) / l
    o_ref[...] = jnp.concatenate([rows[h * T:(h + 1) * T] for h in range(MB_HEADS)], axis=1)


def _moba_sample(z, zf, page_table, cache_k, cache_v, bias_own, bias_prev, DB, T, qcol, kcol, vcol):
    n_pages = page_table.shape[1]
    past = n_pages * PAGE_SIZE
    R = MB_HEADS * T
    W = MB_HEADS * MB_HD
    KW = MB_KV_HEADS * MB_HD
    ck = cache_k.reshape(cache_k.shape[0], PAGE_SIZE * MB_KV_HEADS, MB_HD)
    cv = cache_v.reshape(cache_v.shape[0], PAGE_SIZE * MB_KV_HEADS, MB_HD)
    own = bias_own[:, :T, :T].reshape(R, T)
    adj = bias_prev[:, :T, :].reshape(R, MB_BLOCK)
    b31 = bias_prev[:, :T, 0:1].reshape(R, 1)
    const = lambda b, pt: (0, 0)
    grid_spec = pltpu.PrefetchScalarGridSpec(
        num_scalar_prefetch=1,
        grid=(DB,),
        in_specs=[pl.BlockSpec((T, W), lambda b, pt: (b, qcol)),
                  pl.BlockSpec((T, KW), lambda b, pt: (b, kcol)),
                  pl.BlockSpec((T, KW), lambda b, pt: (b, vcol)),
                  pl.BlockSpec(memory_space=pl.ANY),
                  pl.BlockSpec(memory_space=pl.ANY),
                  pl.BlockSpec((R, T), const),
                  pl.BlockSpec((R, MB_BLOCK), const),
                  pl.BlockSpec((R, 1), const)],
        out_specs=pl.BlockSpec((T, W), lambda b, pt: (b, 0)),
        scratch_shapes=[pltpu.VMEM((2, past * MB_KV_HEADS, MB_HD), F32),
                        pltpu.VMEM((2, past * MB_KV_HEADS, MB_HD), F32),
                        pltpu.VMEM((R, past), F32),
                        pltpu.SemaphoreType.DMA((2, 2))],
    )
    return pl.pallas_call(
        functools.partial(_moba_sample_kernel, T=T, n_pages=n_pages, n_seq=DB),
        grid_spec=grid_spec,
        out_shape=jax.ShapeDtypeStruct((DB * T, W), F32),
        compiler_params=_params("arbitrary"),
        name="moba_sample",
    )(page_table, z, zf, zf, ck, cv, own, adj, b31)


def _xblock_math(x, g, wq_ref, k_ref, v_ref, wo_ref):
    q = jnp.dot(_rms(x, g).astype(BF16), wq_ref[...], preferred_element_type=F32)
    hd = k_ref.shape[2]
    scale = hd ** -0.5
    outs = []
    for h in range(MX_HEADS):
        qh = (q[:, h * hd:(h + 1) * hd] * scale).astype(BF16)
        s = lax.dot_general(qh, k_ref[h], (((1,), (1,)), ((), ())), preferred_element_type=F32)
        p = jnp.exp(s - jnp.max(s, axis=1, keepdims=True))
        o = jnp.dot(p.astype(BF16), v_ref[h], preferred_element_type=F32)
        outs.append((o / jnp.sum(p, axis=1, keepdims=True)).astype(BF16))
    return x + jnp.dot(jnp.concatenate(outs, axis=1), wo_ref[...], preferred_element_type=F32)


def _merge_kernel(oa_ref, ob_ref, ga_ref, gb_ref, x_ref, wpa_ref, wpb_ref, wo_ref, *rest):
    o_ref = rest[-1]
    a = jnp.dot(oa_ref[...].astype(BF16), wpa_ref[...], preferred_element_type=F32)
    b = jnp.dot(ob_ref[...].astype(BF16), wpb_ref[...], preferred_element_type=F32)
    merged = _sigmoid(ga_ref[...].astype(F32)) * a + _sigmoid(gb_ref[...].astype(F32)) * b
    h = x_ref[...] + jnp.dot(merged.astype(BF16), wo_ref[...], preferred_element_type=F32)
    if len(rest) > 1:
        gx_ref, wq_ref, k_ref, v_ref, wxo_ref = rest[:-1]
        h = _xblock_math(h, gx_ref[...], wq_ref, k_ref, v_ref, wxo_ref)
    o_ref[...] = h


def _merge(oa, ob, z, gcol, x, wpa, wpb, wo, xblock=None, tm=512):
    M, D = x.shape
    tm = min(tm, M)
    row = lambda i: (i, 0)
    const = lambda i: (0, 0)
    once = pl.Buffered(1)
    wspec = pl.BlockSpec((D, D), const, pipeline_mode=once)
    args = [oa, ob, z, z, x, wpa, wpb, wo]
    specs = [pl.BlockSpec((tm, D), row), pl.BlockSpec((tm, D), row),
             pl.BlockSpec((tm, D), lambda i: (i, gcol)), pl.BlockSpec((tm, D), lambda i: (i, gcol + 1)),
             pl.BlockSpec((tm, D), row), wspec, wspec, wspec]
    if xblock is not None:
        gx, wq, mk, mv, wxo = xblock
        args += [gx.reshape(1, D), wq, mk, mv, wxo]
        specs += [pl.BlockSpec((1, D), const), wspec,
                  pl.BlockSpec(mk.shape, lambda i: (0, 0, 0)), pl.BlockSpec(mv.shape, lambda i: (0, 0, 0)), wspec]
    return pl.pallas_call(
        _merge_kernel,
        grid=(M // tm,),
        in_specs=specs,
        out_specs=pl.BlockSpec((tm, D), row),
        out_shape=jax.ShapeDtypeStruct((M, D), F32),
        compiler_params=_params("parallel"),
        name="merge",
    )(*args)


def _xattn_kernel(q_ref, mk_hbm, mv_hbm, o_ref, kbuf, vbuf, sem, *, n_steps, per_step, L):
    b = pl.program_id(0)
    hd = q_ref.shape[1] // MX_HEADS
    scale = hd ** -0.5
    slot = b % 2

    def head_copies(step, slot):
        cps = []
        for j in range(per_step):
            seq = step * per_step + j
            for h in range(MX_HEADS):
                cps.append(pltpu.make_async_copy(mk_hbm.at[seq, :, h, :], kbuf.at[slot, j, h], sem.at[0, slot]))
                cps.append(pltpu.make_async_copy(mv_hbm.at[seq, :, h, :], vbuf.at[slot, j, h], sem.at[1, slot]))
        return cps

    @pl.when(b == 0)
    def _():
        for cp in head_copies(0, 0):
            cp.start()

    @pl.when(b + 1 < n_steps)
    def _():
        for cp in head_copies(b + 1, 1 - slot):
            cp.start()

    for cp in head_copies(b, slot):
        cp.wait()

    q = (q_ref[...] * scale).astype(BF16)
    pairs = [(j, h) for j in range(per_step) for h in range(MX_HEADS)]
    s = [lax.dot_general(q[j * L:(j + 1) * L, h * hd:(h + 1) * hd], kbuf[slot, j, h].astype(BF16),
                         (((1,), (1,)), ((), ())), preferred_element_type=F32) for j, h in pairs]
    p = [jnp.exp(x - jnp.max(x, axis=1, keepdims=True)) for x in s]
    o = [jnp.dot(p[c].astype(BF16), vbuf[slot, j, h].astype(BF16), preferred_element_type=F32)
         / jnp.sum(p[c], axis=1, keepdims=True) for c, (j, h) in enumerate(pairs)]
    o_ref[...] = jnp.concatenate(
        [jnp.concatenate(o[j * MX_HEADS:(j + 1) * MX_HEADS], axis=1) for j in range(per_step)], axis=0)


def _xattn(q, mk, mv, B, L, per_step=4):
    D = q.shape[1]
    mem, hd = mk.shape[1], mk.shape[3]
    per_step = math.gcd(per_step, B)
    n_steps = B // per_step
    rows = per_step * L
    return pl.pallas_call(
        functools.partial(_xattn_kernel, n_steps=n_steps, per_step=per_step, L=L),
        grid=(n_steps,),
        in_specs=[pl.BlockSpec((rows, D), lambda b: (b, 0)),
                  pl.BlockSpec(memory_space=pl.ANY),
                  pl.BlockSpec(memory_space=pl.ANY)],
        out_specs=pl.BlockSpec((rows, D), lambda b: (b, 0)),
        out_shape=jax.ShapeDtypeStruct((B * L, D), F32),
        scratch_shapes=[pltpu.VMEM((2, per_step, MX_HEADS, mem, hd), F32),
                        pltpu.VMEM((2, per_step, MX_HEADS, mem, hd), F32),
                        pltpu.SemaphoreType.DMA((2, 2))],
        compiler_params=_params("arbitrary"),
        name="xattn",
    )(q, mk, mv)


def _ffn_kernel(x_ref, g_ref, wg_ref, wu_ref, wd_ref, fg_ref, o_ref, *, tf):
    x = x_ref[...]
    xn = _rms(x, g_ref[...]).astype(BF16)
    acc = x
    for c in range(wg_ref.shape[1] // tf):
        sl = slice(c * tf, (c + 1) * tf)
        a = jnp.dot(xn, wg_ref[:, sl], preferred_element_type=F32)
        u = jnp.dot(xn, wu_ref[:, sl], preferred_element_type=F32)
        acc = acc + jnp.dot((_silu(a) * u).astype(BF16), wd_ref[sl, :], preferred_element_type=F32)
    o_ref[...] = _rms(acc, fg_ref[...])


def _ffn(x, g, wg, wu, wd, fg, tm=1024, tf=256):
    M, D = x.shape
    FF = wg.shape[1]
    tm = min(tm, M)
    row = lambda i: (i, 0)
    const = lambda i: (0, 0)
    once = pl.Buffered(1)
    return pl.pallas_call(
        functools.partial(_ffn_kernel, tf=tf),
        grid=(M // tm,),
        in_specs=[pl.BlockSpec((tm, D), row), pl.BlockSpec((1, D), const),
                  pl.BlockSpec((D, FF), const, pipeline_mode=once),
                  pl.BlockSpec((D, FF), const, pipeline_mode=once),
                  pl.BlockSpec((FF, D), const, pipeline_mode=once),
                  pl.BlockSpec((1, D), const)],
        out_specs=pl.BlockSpec((tm, D), row),
        out_shape=jax.ShapeDtypeStruct((M, D), F32),
        compiler_params=_params("parallel"),
        name="ffn",
    )(x, g.reshape(1, D), wg, wu, wd, fg.reshape(1, D))


def _mixer(x2, B, L, chunk, s0, moba_fn, W, za_dtype, xblock=None):
    za, zf = _proj_in(x2, W["norm_mix_g"], W["w_in"], W["a_cols"], za_dtype, tm=2048, tn=768)
    oa, S = _gla(za, zf, W["lb_logits"], W["hg_norm_g"], s0, B, L, chunk)
    ob = moba_fn(za, zf)
    h = _merge(oa, ob, za, _ZA_GA, x2, W["w_pa"], W["w_pb"], W["w_out"], xblock=xblock)
    return h, S, zf


def kernel(x_prompt, x_sample, cache_k, cache_v, cache_mem_k, cache_mem_v, state_hgrn, page_table, mem_prompt, norm_mix_g, w_in, hg_lb_logits, hg_norm_g, w_proj_a, w_proj_b, w_out, rel_bias, norm_x_g, w_xq, w_mk, w_mv, w_xo, norm_ffn_g, w_gate, w_up, w_down, final_norm_g):
    Bp, Lp, D = x_prompt.shape
    DB, T, _ = x_sample.shape
    l = 0
    win = w_in[l]
    hw = HG_HEADS * HG_DK
    hq, hf, hi, hg = (win[:, j * hw:(j + 1) * hw] for j in range(4))
    c0 = 4 * hw + MB_HEADS * MB_HD
    c1 = c0 + 2 * MB_KV_HEADS * MB_HD
    W = {
        "w_in": jnp.concatenate([hq, hi, hg, win[:, 4 * hw:c0], win[:, c1:], hf, win[:, c0:c1]],
                                axis=1).astype(BF16),
        "a_cols": win.shape[1] - hw - (c1 - c0),
        "norm_mix_g": norm_mix_g[l],
        "lb_logits": hg_lb_logits,
        "hg_norm_g": hg_norm_g[l],
        "w_pa": w_proj_a[l].astype(BF16), "w_pb": w_proj_b[l].astype(BF16), "w_out": w_out[l].astype(BF16),
    }
    wxq, wxo = w_xq[l].astype(BF16), w_xo[l].astype(BF16)
    wmkv = jnp.concatenate([w_mk[l], w_mv[l]], axis=1).astype(BF16)
    wg, wu, wd = w_gate[l].astype(BF16), w_up[l].astype(BF16), w_down[l].astype(BF16)

    bias_own, bias_prev = _bias_tiles(rel_bias)

    xp = x_prompt.reshape(Bp * Lp, D)

    assert Bp == 1, "the prompt kernels take one sequence"
    kw = MB_KV_HEADS * MB_HD

    def new_kv(zf, B, L):
        return (zf[:, _ZF_K * kw:(_ZF_K + 1) * kw].reshape(B, L, MB_KV_HEADS, MB_HD),
                zf[:, _ZF_V * kw:(_ZF_V + 1) * kw].reshape(B, L, MB_KV_HEADS, MB_HD))

    prompt_kv = []

    def moba_p(za, zf):
        ka, va, means, ko, vo = _moba_prep(zf, Lp, _ZF_K, _ZF_V)
        prompt_kv.extend([ko, vo])
        return _moba_prompt(za, Lp, 2 * _ZA_MQ, ka, va, means, bias_own, bias_prev)

    mem = mem_prompt.reshape(Bp * mem_prompt.shape[1], D)
    mkv = _matmul(mem, wmkv, tm=256, tn=512)
    mshape = (Bp, mem_prompt.shape[1], MX_HEADS, D // MX_HEADS)
    mkp, mvp = mkv[:, :D].reshape(mshape), mkv[:, D:].reshape(mshape)
    xblock = (norm_x_g[l], wxq, mkp[0].swapaxes(0, 1).astype(BF16), mvp[0].swapaxes(0, 1).astype(BF16), wxo)

    s0 = jnp.zeros((Bp, HG_HEADS, HG_DK, HG_DV), F32)
    hp, Sp, zfp = _mixer(xp, Bp, Lp, math.gcd(Lp, 64), s0, moba_p, W, BF16, xblock=xblock)
    kp, vp = (a.reshape(Bp, Lp, MB_KV_HEADS, MB_HD) for a in prompt_kv)
    yp = _ffn(hp, norm_ffn_g[l], wg, wu, wd, final_norm_g)

    xs = x_sample.reshape(DB * T, D)

    def moba_s(za, zf):
        return _moba_sample(za, zf, page_table, cache_k[l], cache_v[l], bias_own, bias_prev, DB, T,
                            _ZA_MQ, _ZF_K, _ZF_V)

    hs, Ss, zfs = _mixer(xs, DB, T, T, state_hgrn[l], moba_s, W, F32)
    ksn, vsn = new_kv(zfs, DB, T)
    qxs = _matmul(hs, wxq, g=norm_x_g[l], tn=512)
    oxs = _xattn(qxs, cache_mem_k[l], cache_mem_v[l], DB, T)
    hs = _matmul(oxs, wxo, res=hs, tn=512)
    ys = _ffn(hs, norm_ffn_g[l], wg, wu, wd, final_norm_g)

    return (yp.reshape(Bp, Lp, D), ys.reshape(DB, T, D),
            Sp[None].astype(state_hgrn.dtype), kp[None], vp[None],
            mkp[None], mvp[None],
            Ss[None].astype(state_hgrn.dtype), ksn[None], vsn[None])
```

```python
import functools
import math

import jax
import jax.numpy as jnp
import numpy as np
from jax import lax
from jax.experimental import pallas as pl
from jax.experimental.pallas import tpu as pltpu

F32 = jnp.float32
BF16 = jnp.bfloat16

EPS = 1e-6
HG_HEADS = 8
HG_DK = 128
HG_DV = 128
HG_SUB = 16
MB_HEADS = 8
MB_KV_HEADS = 2
MB_GROUP = MB_HEADS // MB_KV_HEADS
MB_HD = 128
MB_BLOCK = 256
MB_TOPK = 3
MB_UNROLL = 16
RB_BUCKETS = 32
RB_MAX_DIST = 128
MX_HEADS = 4
PAGE_SIZE = 128
NEG = -1e30
LOG2E = math.log2(math.e)

_ZA_HQ, _ZA_HI, _ZA_HG, _ZA_MQ, _ZA_GA = 0, 1, 2, 3, 4
_ZF_HF = 0
_ZF_K, _ZF_V = 4, 5
LANES = 128
VMEM_LIMIT = 56 * 1024 * 1024


def _params(*sem):
    return pltpu.CompilerParams(dimension_semantics=sem, vmem_limit_bytes=VMEM_LIMIT)


def _sigmoid(x):
    return 1.0 / (1.0 + jnp.exp(-x))


def _silu(x):
    return x * _sigmoid(x)


def _rms(x, g):
    return x * lax.rsqrt(jnp.mean(x * x, axis=-1, keepdims=True) + EPS) * g


def _matmul_kernel(*refs, norm, residual):
    refs = list(refs)
    x_ref = refs.pop(0)
    g_ref = refs.pop(0) if norm else None
    w_ref = refs.pop(0)
    r_ref = refs.pop(0) if residual else None
    o_ref, xs_ref = refs

    @pl.when(pl.program_id(1) == 0)
    def _():
        x = x_ref[...]
        if norm:
            x = _rms(x, g_ref[...])
        xs_ref[...] = x.astype(BF16)

    acc = jnp.dot(xs_ref[...], w_ref[...], preferred_element_type=F32)
    if residual:
        acc = acc + r_ref[...]
    o_ref[...] = acc.astype(o_ref.dtype)


def _matmul(x, w, g=None, res=None, tm=512, tn=512, out_dtype=F32):
    M, K = x.shape
    N = w.shape[1]
    tm = min(tm, M)
    tn = min(tn, N)
    assert M % tm == 0 and N % tn == 0
    args = [x]
    specs = [pl.BlockSpec((tm, K), lambda i, j: (i, 0))]
    if g is not None:
        args.append(g.reshape(1, K))
        specs.append(pl.BlockSpec((1, K), lambda i, j: (0, 0)))
    args.append(w)
    specs.append(pl.BlockSpec((K, tn), lambda i, j: (0, j)))
    if res is not None:
        args.append(res)
        specs.append(pl.BlockSpec((tm, tn), lambda i, j: (i, j)))
    return pl.pallas_call(
        functools.partial(_matmul_kernel, norm=g is not None, residual=res is not None),
        grid=(M // tm, N // tn),
        in_specs=specs,
        out_specs=pl.BlockSpec((tm, tn), lambda i, j: (i, j)),
        out_shape=jax.ShapeDtypeStruct((M, N), out_dtype),
        scratch_shapes=[pltpu.VMEM((tm, K), BF16)],
        compiler_params=_params("parallel", "arbitrary"),
        name="matmul",
    )(*args)


def _proj_in_kernel(x_ref, g_ref, w_ref, oa_ref, of_ref, *, tn):
    xs = _rms(x_ref[...], g_ref[...]).astype(BF16)
    a_cols = oa_ref.shape[1]
    for j in range(w_ref.shape[1] // tn):
        acc = jnp.dot(xs, w_ref[:, j * tn:(j + 1) * tn], preferred_element_type=F32)
        if j * tn < a_cols:
            oa_ref[:, j * tn:(j + 1) * tn] = acc.astype(oa_ref.dtype)
        else:
            of_ref[:, j * tn - a_cols:(j + 1) * tn - a_cols] = acc


def _proj_in(x, g, w, a_cols, a_dtype, tm=512, tn=512):
    M, K = x.shape
    N = w.shape[1]
    tm = min(tm, M)
    assert M % tm == 0 and a_cols % tn == 0 and N % tn == 0
    return pl.pallas_call(
        functools.partial(_proj_in_kernel, tn=tn),
        grid=(M // tm,),
        in_specs=[pl.BlockSpec((tm, K), lambda i: (i, 0)),
                  pl.BlockSpec((1, K), lambda i: (0, 0)),
                  pl.BlockSpec((K, N), lambda i: (0, 0), pipeline_mode=pl.Buffered(1))],
        out_specs=[pl.BlockSpec((tm, a_cols), lambda i: (i, 0)),
                   pl.BlockSpec((tm, N - a_cols), lambda i: (i, 0))],
        out_shape=[jax.ShapeDtypeStruct((M, a_cols), a_dtype),
                   jax.ShapeDtypeStruct((M, N - a_cols), F32)],
        compiler_params=_params("parallel"),
        name="proj_in",
    )(x, g.reshape(1, K), w)


def _subblock_cumsum(g, sub):
    row = lax.broadcasted_iota(jnp.int32, g.shape, 0) % sub
    s = 1
    while s < sub:
        g = g + jnp.where(row >= s, pltpu.roll(g, s, 0), 0.0)
        s *= 2
    return g


def _gla_kernel(q_ref, f_ref, i_ref, g_ref, lbl_ref, ng_ref, s0_ref, o_ref, so_ref, st_ref, *, C, sub):
    c = pl.program_id(1)
    nsub = C // sub
    mid = sub // 2 - 1

    @pl.when(c == 0)
    def _():
        st_ref[...] = s0_ref[0]

    lbl = lbl_ref[...]
    e = jnp.exp(lbl - jnp.max(lbl, axis=0, keepdims=True))
    lb_all = e[0:1] / jnp.sum(e, axis=0, keepdims=True)
    ng = ng_ref[...]

    row = lax.broadcasted_iota(jnp.int32, (C, C), 0)
    col = lax.broadcasted_iota(jnp.int32, (C, C), 1)
    diag_mask = (row // sub == col // sub) & (row >= col)

    W = HG_HEADS * HG_DK
    q3 = q_ref[...].astype(F32).reshape(nsub, sub, W)
    f = lb_all + (1.0 - lb_all) * _sigmoid(f_ref[...])
    k3 = (1.0 - f).reshape(nsub, sub, W)
    cs = _subblock_cumsum(jnp.log(f), sub).reshape(nsub, sub, W)
    v_all = i_ref[...].astype(BF16)
    silu_g = _silu(g_ref[...].astype(F32))

    T = cs[:, sub - 1:sub, :]
    m = cs[:, mid:mid + 1, :]
    qd = q3 * jnp.exp(cs - m)
    kd = k3 * jnp.exp(m - cs)
    qt = qd * jnp.exp(m)
    kh = kd * jnp.exp(T - m)
    qd_all = qd.reshape(C, W).astype(BF16)
    kd_all = kd.reshape(C, W).astype(BF16)

    PT = [jnp.zeros((1, W), F32)]
    for I in range(nsub):
        PT.append(PT[-1] + T[I])

    def scaled_k(I):
        parts = []
        for J in range(nsub):
            if J < I:
                parts.append(kh[J] * jnp.exp(PT[I] - PT[J + 1]))
            else:
                parts.append(jnp.zeros((sub, W), F32))
        return jnp.concatenate(parts, axis=0).astype(BF16) if nsub > 1 else parts[0].astype(BF16)

    qt_b = [qt[I].astype(BF16) for I in range(nsub)]
    ks_all = [scaled_k(I) for I in range(1, nsub + 1)]
    qc_all = jnp.concatenate([qt[I] * jnp.exp(PT[I]) for I in range(nsub)], axis=0).astype(BF16)
    decay_all = jnp.exp(PT[nsub])

    sls = [slice(h * HG_DK, (h + 1) * HG_DK) for h in range(HG_HEADS)]
    nt = (((1,), (1,)), ((), ()))
    atts = []
    for sl in sls:
        att = jnp.where(diag_mask,
                        lax.dot_general(qd_all[:, sl], kd_all[:, sl], nt, preferred_element_type=F32), 0.0)
        if nsub > 1:
            rows = [jnp.zeros((sub, C), F32)]
            for I in range(1, nsub):
                rows.append(lax.dot_general(qt_b[I][:, sl], ks_all[I - 1][:, sl], nt,
                                            preferred_element_type=F32))
            att = att + jnp.concatenate(rows, axis=0)
        atts.append(att.astype(BF16))

    sts = [st_ref[h] for h in range(HG_HEADS)]
    outs = [jnp.dot(atts[h], v_all[:, sl], preferred_element_type=F32)
            + jnp.dot(qc_all[:, sl], sts[h].astype(BF16), preferred_element_type=F32)
            for h, sl in enumerate(sls)]
    for h, sl in enumerate(sls):
        dcol = jnp.transpose(jnp.broadcast_to(decay_all[:, sl], (8, HG_DK)))[:, 0:1]
        st_ref[h] = (sts[h] * dcol
                     + lax.dot_general(ks_all[-1][:, sl], v_all[:, sl], (((0,), (0,)), ((), ())),
                                       preferred_element_type=F32))
    for h, sl in enumerate(sls):
        o = outs[h]
        o = o * lax.rsqrt(jnp.mean(o * o, axis=-1, keepdims=True) + EPS) * ng * silu_g[:, sl]
        o_ref[:, sl] = o.astype(o_ref.dtype)

    @pl.when(c == pl.num_programs(1) - 1)
    def _():
        so_ref[0] = st_ref[...]


def _gla(za, zf, lb_logits, norm_g, s0, B, L, C):
    nc = L // C
    sub = min(HG_SUB, C)
    W = HG_HEADS * HG_DK

    def zspec(j):
        return pl.BlockSpec((C, W), lambda b, c, j=j: (b * nc + c, j))

    return pl.pallas_call(
        functools.partial(_gla_kernel, C=C, sub=sub),
        grid=(B, nc),
        in_specs=[zspec(_ZA_HQ), zspec(_ZF_HF), zspec(_ZA_HI), zspec(_ZA_HG),
                  pl.BlockSpec((2, W), lambda b, c: (0, 0)),
                  pl.BlockSpec((1, HG_DV), lambda b, c: (0, 0)),
                  pl.BlockSpec((1, HG_HEADS, HG_DK, HG_DV), lambda b, c: (b, 0, 0, 0))],
        out_specs=[pl.BlockSpec((C, W), lambda b, c: (b * nc + c, 0)),
                   pl.BlockSpec((1, HG_HEADS, HG_DK, HG_DV), lambda b, c: (b, 0, 0, 0))],
        out_shape=[jax.ShapeDtypeStruct((B * L, W), za.dtype),
                   jax.ShapeDtypeStruct((B, HG_HEADS, HG_DK, HG_DV), F32)],
        scratch_shapes=[pltpu.VMEM((HG_HEADS, HG_DK, HG_DV), F32)],
        compiler_params=_params("arbitrary", "arbitrary"),
        name="gla",
    )(za, zf, za, za, lb_logits, norm_g.reshape(1, HG_DV), s0)


def _bucket_upper_bounds():
    max_exact = RB_BUCKETS // 2
    d = np.arange(1, 4 * RB_MAX_DIST)
    large = max_exact + (np.log(d.astype(np.float32) / max_exact) / math.log(RB_MAX_DIST / max_exact)
                         * (RB_BUCKETS - max_exact)).astype(np.int32)
    bucket = np.where(d < max_exact, d, np.minimum(large, RB_BUCKETS - 1))
    bucket = np.concatenate([[0], bucket])
    return [int(np.max(np.nonzero(bucket == b)[0])) for b in range(RB_BUCKETS - 1)]


def _bias_kernel(rb_ref, own_ref, prev_ref):
    h = pl.program_id(0)
    i = lax.broadcasted_iota(jnp.int32, (MB_BLOCK, MB_BLOCK), 0)
    j = lax.broadcasted_iota(jnp.int32, (MB_BLOCK, MB_BLOCK), 1)
    dmax = _bucket_upper_bounds()

    def table(d):
        val = jnp.full(d.shape, rb_ref[RB_BUCKETS - 1, h], F32)
        for b in range(RB_BUCKETS - 2, -1, -1):
            val = jnp.where(d <= dmax[b], rb_ref[b, h], val)
        return val

    d = i - j
    own_ref[0] = jnp.where(d >= 0, table(d), NEG)
    prev_ref[0] = table(d + MB_BLOCK)


def _bias_tiles(rel_bias):
    shp = jax.ShapeDtypeStruct((MB_HEADS, MB_BLOCK, MB_BLOCK), F32)
    spec = pl.BlockSpec((1, MB_BLOCK, MB_BLOCK), lambda h: (h, 0, 0))
    return pl.pallas_call(
        _bias_kernel,
        grid=(MB_HEADS,),
        in_specs=[pl.BlockSpec(memory_space=pltpu.SMEM)],
        out_specs=[spec, spec],
        out_shape=[shp, shp],
        compiler_params=_params("arbitrary"),
        name="bias_tiles",
    )(rel_bias)


def _top_select(gate, lane, axis=1):
    sel = jnp.zeros(gate.shape, jnp.bool_)
    lane = lane.astype(F32)
    for _ in range(MB_TOPK):
        m = jnp.max(gate, axis=axis, keepdims=True)
        idx = jnp.min(jnp.where(gate == m, lane, float(2 ** 30)), axis=axis, keepdims=True)
        hit = (lane == idx) & (m > -jnp.inf)
        sel = sel | hit
        gate = jnp.where(hit, -jnp.inf, gate)
    return sel


def _moba_prep_kernel(k_ref, v_ref, ka_ref, vb_ref, mean_ref, ko_ref, vo_ref, *, per_step):
    rows = per_step * MB_BLOCK
    block = (lax.broadcasted_iota(jnp.int32, (rows, LANES), 0) // MB_BLOCK
             + pl.program_id(0) * per_step)
    lane = lax.broadcasted_iota(jnp.int32, (rows, LANES), 1)
    onehot = jnp.where(lane == block, 1.0, 0.0).astype(BF16)
    ones = jnp.ones((rows, LANES), BF16)
    for g in range(MB_KV_HEADS):
        k = k_ref[:, g * MB_HD:(g + 1) * MB_HD]
        v = v_ref[:, g * MB_HD:(g + 1) * MB_HD]
        ka_ref[g] = jnp.concatenate([k.astype(BF16), onehot], axis=1)
        vb_ref[g] = jnp.concatenate([v.astype(BF16), ones], axis=1)
        mean_ref[g] = jnp.mean(k.reshape(per_step, MB_BLOCK, MB_HD), axis=1)
        ko_ref[pl.ds(g, rows, stride=MB_KV_HEADS), :] = k
        vo_ref[pl.ds(g, rows, stride=MB_KV_HEADS), :] = v


def _moba_prep(z, L, kcol, vcol):
    nb = L // MB_BLOCK
    per_step = math.gcd(nb, 8)
    rows = per_step * MB_BLOCK
    KVH = MB_KV_HEADS
    return pl.pallas_call(
        functools.partial(_moba_prep_kernel, per_step=per_step),
        grid=(nb // per_step,),
        in_specs=[pl.BlockSpec((rows, KVH * MB_HD), lambda n: (n, kcol)),
                  pl.BlockSpec((rows, KVH * MB_HD), lambda n: (n, vcol))],
        out_specs=[pl.BlockSpec((KVH, rows, 2 * MB_HD), lambda n: (0, n, 0)),
                   pl.BlockSpec((KVH, rows, 2 * MB_HD), lambda n: (0, n, 0)),
                   pl.BlockSpec((KVH, per_step, MB_HD), lambda n: (0, n, 0)),
                   pl.BlockSpec((rows * KVH, MB_HD), lambda n: (n, 0)),
                   pl.BlockSpec((rows * KVH, MB_HD), lambda n: (n, 0))],
        out_shape=[jax.ShapeDtypeStruct((KVH, L, 2 * MB_HD), BF16),
                   jax.ShapeDtypeStruct((KVH, L, 2 * MB_HD), BF16),
                   jax.ShapeDtypeStruct((KVH, nb, MB_HD), F32),
                   jax.ShapeDtypeStruct((L * KVH, MB_HD), F32),
                   jax.ShapeDtypeStruct((L * KVH, MB_HD), F32)],
        compiler_params=_params("arbitrary"),
        name="moba_prep",
    )(z, z)


def _moba_prompt_kernel(q_ref, ka_ref, va_ref, mean_ref, own_ref, prev_ref, o_ref,
                        qa_ref, m_ref, acc_ref, s_ref):
    i = pl.program_id(1)
    G, B = MB_GROUP, MB_BLOCK
    R = G * B
    scale = MB_HD ** -0.5
    means = mean_ref[0]
    nbp = means.shape[0]
    blk = lax.broadcasted_iota(jnp.int32, (nbp, B), 0)
    no_block = jnp.zeros((LANES - nbp, B), F32)

    for j in range(G):
        q = q_ref[:, j * MB_HD:(j + 1) * MB_HD].astype(F32)
        gate = lax.dot_general(means, q, (((1,), (1,)), ((), ())), precision=lax.Precision.HIGHEST,
                               preferred_element_type=F32)
        sel = _top_select(jnp.where(blk < i, gate, -jnp.inf), blk, axis=0)
        b31 = prev_ref[j, 0:1, 0:1]
        mb = jnp.where(sel, jnp.where(blk == i - 1, 0.0, b31), NEG)
        mb = jnp.where(blk == i, 0.0, mb)
        mb = jnp.concatenate([mb, no_block], axis=0).T
        qa_ref[j * B:(j + 1) * B, :] = jnp.concatenate(
            [(q * (scale * LOG2E)).astype(BF16), (mb * LOG2E).astype(BF16)], axis=1)

    def block_rows(n):
        return pl.ds(pl.multiple_of(n * B, B), B)

    def qk(n):
        return lax.dot_general(qa_ref[...], ka_ref[0, block_rows(n), :], (((1,), (1,)), ((), ())),
                               preferred_element_type=F32)

    def softmax_pv(s, n, first):
        vn = va_ref[0, block_rows(n), :]
        mx = jnp.max(s, axis=1, keepdims=True)
        if first:
            m_new = jnp.broadcast_to(mx, (R, LANES))
            p = jnp.exp2(s - mx)
            acc_ref[...] = jnp.dot(p.astype(BF16), vn, preferred_element_type=F32)
        else:
            m_prev = m_ref[...]
            m_new = jnp.maximum(m_prev, mx)
            alpha = jnp.exp2(m_prev - m_new)
            p = jnp.exp2(s - jnp.concatenate([m_new, m_new], axis=1))
            acc_ref[...] = (jnp.concatenate([alpha, alpha], axis=1) * acc_ref[...]
                            + jnp.dot(p.astype(BF16), vn, preferred_element_type=F32))
        m_ref[...] = m_new

    def own_logits():
        return qk(i) + own_ref[...].reshape(R, B) * LOG2E

    def prev_logits():
        return qk(i - 1) + prev_ref[...].reshape(R, B) * LOG2E

    @pl.when(i == 0)
    def _():
        softmax_pv(own_logits(), i, True)

    @pl.when(i == 1)
    def _():
        softmax_pv(own_logits(), i, True)
        softmax_pv(prev_logits(), i - 1, False)

    @pl.when(i >= 2)
    def _():
        s_own = own_logits()
        s_prev = prev_logits()
        softmax_pv(s_own, i, True)
        s_ref[...] = qk(0)
        softmax_pv(s_prev, i - 1, False)
        last = i - 2

        def run(n0, count):
            s_cur = s_ref[...]
            for u in range(count):
                s_next = qk(jnp.minimum(n0 + u + 1, last))
                softmax_pv(s_cur, n0 + u, False)
                s_cur = s_next
            s_ref[...] = s_cur

        def unrolled(t, carry):
            run(MB_UNROLL * t, MB_UNROLL)
            return carry

        total = i - 1
        n_main = total // MB_UNROLL
        lax.fori_loop(0, n_main, unrolled, 0)

        base = n_main * MB_UNROLL
        piece = MB_UNROLL // 2
        while piece >= 1:
            take = (total & piece) != 0

            @pl.when(take)
            def _(base=base, piece=piece):
                run(base, piece)

            base = base + jnp.where(take, piece, 0)
            piece //= 2

    acc = acc_ref[...]
    o = acc[:, :MB_HD] / acc[:, MB_HD:]
    for j in range(G):
        o_ref[:, j * MB_HD:(j + 1) * MB_HD] = o[j * B:(j + 1) * B].astype(o_ref.dtype)


def _moba_prompt(z, L, qcol, ka, vb, means, bias_own, bias_prev):
    nb = L // MB_BLOCK
    G, B = MB_GROUP, MB_BLOCK
    assert nb <= LANES, "block ids ride a 128-lane one-hot"
    nbp = -(-nb // 8) * 8
    means_p = jnp.pad(means, ((0, 0), (0, nbp - nb), (0, 0)))
    return pl.pallas_call(
        _moba_prompt_kernel,
        grid=(MB_KV_HEADS, nb),
        in_specs=[pl.BlockSpec((B, G * MB_HD), lambda g, i: (i, qcol + g)),
                  pl.BlockSpec((1, L, 2 * MB_HD), lambda g, i: (g, 0, 0), pipeline_mode=pl.Buffered(1)),
                  pl.BlockSpec((1, L, 2 * MB_HD), lambda g, i: (g, 0, 0), pipeline_mode=pl.Buffered(1)),
                  pl.BlockSpec((1, nbp, MB_HD), lambda g, i: (g, 0, 0)),
                  pl.BlockSpec((G, B, B), lambda g, i: (g, 0, 0)),
                  pl.BlockSpec((G, B, B), lambda g, i: (g, 0, 0))],
        out_specs=pl.BlockSpec((B, G * MB_HD), lambda g, i: (i, g)),
        out_shape=jax.ShapeDtypeStruct((L, MB_HEADS * MB_HD), z.dtype),
        scratch_shapes=[pltpu.VMEM((G * B, 2 * MB_HD), BF16),
                        pltpu.VMEM((G * B, LANES), F32),
                        pltpu.VMEM((G * B, 2 * MB_HD), F32),
                        pltpu.VMEM((G * B, B), F32)],
        compiler_params=_params("arbitrary", "arbitrary"),
        name="moba_prompt",
    )(z, ka, vb, means_p, bias_own, bias_prev)


def _moba_sample_kernel(pt_ref, q_ref, kn_ref, vn_ref, ck_hbm, cv_hbm, own_ref, adj_ref, b31_ref, o_ref,
                        kbuf, vbuf, lg_ref, sem, *, T, n_pages, n_seq):
    b = pl.program_id(0)
    B = MB_BLOCK
    past = n_pages * PAGE_SIZE
    nb = past // B
    R = MB_HEADS * T
    half = R // MB_KV_HEADS
    scale = MB_HD ** -0.5

    KVH = MB_KV_HEADS
    prow = PAGE_SIZE * KVH

    def page_copies(seq, slot, j):
        dst = pl.ds(pl.multiple_of(j * prow, prow), prow)
        page = pt_ref[seq, j]
        return (pltpu.make_async_copy(ck_hbm.at[page], kbuf.at[slot, dst, :], sem.at[0, slot]),
                pltpu.make_async_copy(cv_hbm.at[page], vbuf.at[slot, dst, :], sem.at[1, slot]))

    def start_seq(seq, slot):
        def body(j, c):
            ck, cv = page_copies(seq, slot, j)
            ck.start()
            cv.start()
            return c
        lax.fori_loop(0, n_pages, body, 0)

    def wait_seq(seq, slot):
        def body(j, c):
            ck, cv = page_copies(seq, slot, j)
            ck.wait()
            cv.wait()
            return c
        lax.fori_loop(0, n_pages, body, 0)

    slot = b % 2

    @pl.when(b == 0)
    def _():
        start_seq(0, 0)

    @pl.when(b + 1 < n_seq)
    def _():
        start_seq(b + 1, 1 - slot)

    wait_seq(b, slot)

    def block(buf, n, g):
        return buf[slot, pl.ds(n * B * KVH + g, B, stride=KVH), :]

    q = q_ref[...]
    Q = jnp.concatenate([q[:, h * MB_HD:(h + 1) * MB_HD] for h in range(MB_HEADS)], axis=0)
    qf = Q * scale
    qs = qf.astype(BF16)
    grow = [slice(g * half, (g + 1) * half) for g in range(KVH)]

    mean_rows = [[] for _ in range(KVH)]
    for n in range(nb):
        for g in range(KVH):
            kf = block(kbuf, n, g)
            mean_rows[g].append(jnp.mean(kf, axis=0, keepdims=True))
            lg_ref[grow[g], n * B:(n + 1) * B] = lax.dot_general(
                qs[grow[g]], kf.astype(BF16), (((1,), (1,)), ((), ())), preferred_element_type=F32)

    gates = []
    for g in range(KVH):
        means = jnp.concatenate(mean_rows[g] + [jnp.zeros((LANES - nb, MB_HD), F32)], axis=0)
        gates.append(lax.dot_general(Q[grow[g]], means, (((1,), (1,)), ((), ())),
                                     precision=lax.Precision.HIGHEST, preferred_element_type=F32))
    gate = jnp.concatenate(gates, axis=0)
    lane = lax.broadcasted_iota(jnp.int32, (R, LANES), 1)
    sel = _top_select(jnp.where(lane < nb, gate, -jnp.inf), lane)
    mb = jnp.where(sel, jnp.where(lane == nb - 1, 0.0, b31_ref[...]), NEG)
    adj = adj_ref[...]

    def logits(n):
        s = lg_ref[:, n * B:(n + 1) * B] + mb[:, n:n + 1]
        return s + adj if n == nb - 1 else s

    kn = kn_ref[...]
    vn = vn_ref[...]
    s_own = jnp.concatenate(
        [lax.dot_general(qf[grow[g]], kn[:, g * MB_HD:(g + 1) * MB_HD], (((1,), (1,)), ((), ())),
                         preferred_element_type=F32) for g in range(KVH)], axis=0) + own_ref[...]

    m = jnp.max(s_own, axis=1, keepdims=True)
    for n in range(nb):
        m = jnp.maximum(m, jnp.max(logits(n), axis=1, keepdims=True))

    p_own = jnp.exp(s_own - m)
    l = jnp.sum(p_own, axis=1, keepdims=True)
    acc = [jnp.dot(p_own[grow[g]], vn[:, g * MB_HD:(g + 1) * MB_HD], preferred_element_type=F32)
           for g in range(KVH)]
    for n in range(nb):
        p = jnp.exp(logits(n) - m)
        l = l + jnp.sum(p, axis=1, keepdims=True)
        pb = p.astype(BF16)
        for g in range(KVH):
            acc[g] = acc[g] + jnp.dot(pb[grow[g]], block(vbuf, n, g).astype(BF16),
                                      preferred_element_type=F32)
    rows = jnp.concatenate(acc, axis=0) / l
    o_ref[...] = jnp.concatenate([rows[h * T:(h + 1) * T] for h in range(MB_HEADS)], axis=1)


def _moba_sample(z, zf, page_table, cache_k, cache_v, bias_own, bias_prev, DB, T, qcol, kcol, vcol):
    n_pages = page_table.shape[1]
    past = n_pages * PAGE_SIZE
    R = MB_HEADS * T
    W = MB_HEADS * MB_HD
    KW = MB_KV_HEADS * MB_HD
    ck = cache_k.reshape(cache_k.shape[0], PAGE_SIZE * MB_KV_HEADS, MB_HD)
    cv = cache_v.reshape(cache_v.shape[0], PAGE_SIZE * MB_KV_HEADS, MB_HD)
    own = bias_own[:, :T, :T].reshape(R, T)
    adj = bias_prev[:, :T, :].reshape(R, MB_BLOCK)
    b31 = bias_prev[:, :T, 0:1].reshape(R, 1)
    const = lambda b, pt: (0, 0)
    grid_spec = pltpu.PrefetchScalarGridSpec(
        num_scalar_prefetch=1,
        grid=(DB,),
        in_specs=[pl.BlockSpec((T, W), lambda b, pt: (b, qcol)),
                  pl.BlockSpec((T, KW), lambda b, pt: (b, kcol)),
                  pl.BlockSpec((T, KW), lambda b, pt: (b, vcol)),
                  pl.BlockSpec(memory_space=pl.ANY),
                  pl.BlockSpec(memory_space=pl.ANY),
                  pl.BlockSpec((R, T), const),
                  pl.BlockSpec((R, MB_BLOCK), const),
                  pl.BlockSpec((R, 1), const)],
        out_specs=pl.BlockSpec((T, W), lambda b, pt: (b, 0)),
        scratch_shapes=[pltpu.VMEM((2, past * MB_KV_HEADS, MB_HD), F32),
                        pltpu.VMEM((2, past * MB_KV_HEADS, MB_HD), F32),
                        pltpu.VMEM((R, past), F32),
                        pltpu.SemaphoreType.DMA((2, 2))],
    )
    return pl.pallas_call(
        functools.partial(_moba_sample_kernel, T=T, n_pages=n_pages, n_seq=DB),
        grid_spec=grid_spec,
        out_shape=jax.ShapeDtypeStruct((DB * T, W), F32),
        compiler_params=_params("arbitrary"),
        name="moba_sample",
    )(page_table, z, zf, zf, ck, cv, own, adj, b31)


def _xblock_math(x, g, wq_ref, k_ref, v_ref, wo_ref):
    q = jnp.dot(_rms(x, g).astype(BF16), wq_ref[...], preferred_element_type=F32)
    hd = k_ref.shape[2]
    scale = hd ** -0.5
    outs = []
    for h in range(MX_HEADS):
        qh = (q[:, h * hd:(h + 1) * hd] * scale).astype(BF16)
        s = lax.dot_general(qh, k_ref[h], (((1,), (1,)), ((), ())), preferred_element_type=F32)
        p = jnp.exp(s - jnp.max(s, axis=1, keepdims=True))
        o = jnp.dot(p.astype(BF16), v_ref[h], preferred_element_type=F32)
        outs.append((o / jnp.sum(p, axis=1, keepdims=True)).astype(BF16))
    return x + jnp.dot(jnp.concatenate(outs, axis=1), wo_ref[...], preferred_element_type=F32)


def _merge_kernel(oa_ref, ob_ref, ga_ref, gb_ref, x_ref, wpa_ref, wpb_ref, wo_ref, *rest):
    o_ref = rest[-1]
    a = jnp.dot(oa_ref[...].astype(BF16), wpa_ref[...], preferred_element_type=F32)
    b = jnp.dot(ob_ref[...].astype(BF16), wpb_ref[...], preferred_element_type=F32)
    merged = _sigmoid(ga_ref[...].astype(F32)) * a + _sigmoid(gb_ref[...].astype(F32)) * b
    h = x_ref[...] + jnp.dot(merged.astype(BF16), wo_ref[...], preferred_element_type=F32)
    if len(rest) > 1:
        gx_ref, wq_ref, k_ref, v_ref, wxo_ref = rest[:-1]
        h = _xblock_math(h, gx_ref[...], wq_ref, k_ref, v_ref, wxo_ref)
    o_ref[...] = h


def _merge(oa, ob, z, gcol, x, wpa, wpb, wo, xblock=None, tm=512):
    M, D = x.shape
    tm = min(tm, M)
    row = lambda i: (i, 0)
    const = lambda i: (0, 0)
    once = pl.Buffered(1)
    wspec = pl.BlockSpec((D, D), const, pipeline_mode=once)
    args = [oa, ob, z, z, x, wpa, wpb, wo]
    specs = [pl.BlockSpec((tm, D), row), pl.BlockSpec((tm, D), row),
             pl.BlockSpec((tm, D), lambda i: (i, gcol)), pl.BlockSpec((tm, D), lambda i: (i, gcol + 1)),
             pl.BlockSpec((tm, D), row), wspec, wspec, wspec]
    if xblock is not None:
        gx, wq, mk, mv, wxo = xblock
        args += [gx.reshape(1, D), wq, mk, mv, wxo]
        specs += [pl.BlockSpec((1, D), const), wspec,
                  pl.BlockSpec(mk.shape, lambda i: (0, 0, 0)), pl.BlockSpec(mv.shape, lambda i: (0, 0, 0)), wspec]
    return pl.pallas_call(
        _merge_kernel,
        grid=(M // tm,),
        in_specs=specs,
        out_specs=pl.BlockSpec((tm, D), row),
        out_shape=jax.ShapeDtypeStruct((M, D), F32),
        compiler_params=_params("parallel"),
        name="merge",
    )(*args)


def _xattn_kernel(q_ref, mk_hbm, mv_hbm, o_ref, kbuf, vbuf, sem, *, n_steps, per_step, L):
    b = pl.program_id(0)
    hd = q_ref.shape[1] // MX_HEADS
    scale = hd ** -0.5
    slot = b % 2

    def head_copies(step, slot):
        cps = []
        for j in range(per_step):
            seq = step * per_step + j
            for h in range(MX_HEADS):
                cps.append(pltpu.make_async_copy(mk_hbm.at[seq, :, h, :], kbuf.at[slot, j, h], sem.at[0, slot]))
                cps.append(pltpu.make_async_copy(mv_hbm.at[seq, :, h, :], vbuf.at[slot, j, h], sem.at[1, slot]))
        return cps

    @pl.when(b == 0)
    def _():
        for cp in head_copies(0, 0):
            cp.start()

    @pl.when(b + 1 < n_steps)
    def _():
        for cp in head_copies(b + 1, 1 - slot):
            cp.start()

    for cp in head_copies(b, slot):
        cp.wait()

    q = (q_ref[...] * scale).astype(BF16)
    pairs = [(j, h) for j in range(per_step) for h in range(MX_HEADS)]
    s = [lax.dot_general(q[j * L:(j + 1) * L, h * hd:(h + 1) * hd], kbuf[slot, j, h].astype(BF16),
                         (((1,), (1,)), ((), ())), preferred_element_type=F32) for j, h in pairs]
    p = [jnp.exp(x - jnp.max(x, axis=1, keepdims=True)) for x in s]
    o = [jnp.dot(p[c].astype(BF16), vbuf[slot, j, h].astype(BF16), preferred_element_type=F32)
         / jnp.sum(p[c], axis=1, keepdims=True) for c, (j, h) in enumerate(pairs)]
    o_ref[...] = jnp.concatenate(
        [jnp.concatenate(o[j * MX_HEADS:(j + 1) * MX_HEADS], axis=1) for j in range(per_step)], axis=0)


def _xattn(q, mk, mv, B, L, per_step=4):
    D = q.shape[1]
    mem, hd = mk.shape[1], mk.shape[3]
    per_step = math.gcd(per_step, B)
    n_steps = B // per_step
    rows = per_step * L
    return pl.pallas_call(
        functools.partial(_xattn_kernel, n_steps=n_steps, per_step=per_step, L=L),
        grid=(n_steps,),
        in_specs=[pl.BlockSpec((rows, D), lambda b: (b, 0)),
                  pl.BlockSpec(memory_space=pl.ANY),
                  pl.BlockSpec(memory_space=pl.ANY)],
        out_specs=pl.BlockSpec((rows, D), lambda b: (b, 0)),
        out_shape=jax.ShapeDtypeStruct((B * L, D), F32),
        scratch_shapes=[pltpu.VMEM((2, per_step, MX_HEADS, mem, hd), F32),
                        pltpu.VMEM((2, per_step, MX_HEADS, mem, hd), F32),
                        pltpu.SemaphoreType.DMA((2, 2))],
        compiler_params=_params("arbitrary"),
        name="xattn",
    )(q, mk, mv)


def _ffn_kernel(x_ref, g_ref, wg_ref, wu_ref, wd_ref, fg_ref, o_ref, *, tf):
    x = x_ref[...]
    xn = _rms(x, g_ref[...]).astype(BF16)
    acc = x
    for c in range(wg_ref.shape[1] // tf):
        sl = slice(c * tf, (c + 1) * tf)
        a = jnp.dot(xn, wg_ref[:, sl], preferred_element_type=F32)
        u = jnp.dot(xn, wu_ref[:, sl], preferred_element_type=F32)
        acc = acc + jnp.dot((_silu(a) * u).astype(BF16), wd_ref[sl, :], preferred_element_type=F32)
    o_ref[...] = _rms(acc, fg_ref[...])


def _ffn(x, g, wg, wu, wd, fg, tm=1024, tf=256):
    M, D = x.shape
    FF = wg.shape[1]
    tm = min(tm, M)
    row = lambda i: (i, 0)
    const = lambda i: (0, 0)
    once = pl.Buffered(1)
    return pl.pallas_call(
        functools.partial(_ffn_kernel, tf=tf),
        grid=(M // tm,),
        in_specs=[pl.BlockSpec((tm, D), row), pl.BlockSpec((1, D), const),
                  pl.BlockSpec((D, FF), const, pipeline_mode=once),
                  pl.BlockSpec((D, FF), const, pipeline_mode=once),
                  pl.BlockSpec((FF, D), const, pipeline_mode=once),
                  pl.BlockSpec((1, D), const)],
        out_specs=pl.BlockSpec((tm, D), row),
        out_shape=jax.ShapeDtypeStruct((M, D), F32),
        compiler_params=_params("parallel"),
        name="ffn",
    )(x, g.reshape(1, D), wg, wu, wd, fg.reshape(1, D))


def _mixer(x2, B, L, chunk, s0, moba_fn, W, za_dtype, xblock=None):
    tm = 512 if za_dtype == BF16 else 256
    za, zf = _proj_in(x2, W["norm_mix_g"], W["w_in"], W["a_cols"], za_dtype, tm=tm)
    oa, S = _gla(za, zf, W["lb_logits"], W["hg_norm_g"], s0, B, L, chunk)
    ob = moba_fn(za, zf)
    h = _merge(oa, ob, za, _ZA_GA, x2, W["w_pa"], W["w_pb"], W["w_out"], xblock=xblock)
    return h, S, zf


def kernel(x_prompt, x_sample, cache_k, cache_v, cache_mem_k, cache_mem_v, state_hgrn, page_table, mem_prompt, norm_mix_g, w_in, hg_lb_logits, hg_norm_g, w_proj_a, w_proj_b, w_out, rel_bias, norm_x_g, w_xq, w_mk, w_mv, w_xo, norm_ffn_g, w_gate, w_up, w_down, final_norm_g):
    Bp, Lp, D = x_prompt.shape
    DB, T, _ = x_sample.shape
    l = 0
    win = w_in[l]
    hw = HG_HEADS * HG_DK
    hq, hf, hi, hg = (win[:, j * hw:(j + 1) * hw] for j in range(4))
    c0 = 4 * hw + MB_HEADS * MB_HD
    c1 = c0 + 2 * MB_KV_HEADS * MB_HD
    W = {
        "w_in": jnp.concatenate([hq, hi, hg, win[:, 4 * hw:c0], win[:, c1:], hf, win[:, c0:c1]],
                                axis=1).astype(BF16),
        "a_cols": win.shape[1] - hw - (c1 - c0),
        "norm_mix_g": norm_mix_g[l],
        "lb_logits": hg_lb_logits,
        "hg_norm_g": hg_norm_g[l],
        "w_pa": w_proj_a[l].astype(BF16), "w_pb": w_proj_b[l].astype(BF16), "w_out": w_out[l].astype(BF16),
    }
    wxq, wxo = w_xq[l].astype(BF16), w_xo[l].astype(BF16)
    wmkv = jnp.concatenate([w_mk[l], w_mv[l]], axis=1).astype(BF16)
    wg, wu, wd = w_gate[l].astype(BF16), w_up[l].astype(BF16), w_down[l].astype(BF16)

    bias_own, bias_prev = _bias_tiles(rel_bias)

    xp = x_prompt.reshape(Bp * Lp, D)

    assert Bp == 1, "the prompt kernels take one sequence"
    kw = MB_KV_HEADS * MB_HD

    def new_kv(zf, B, L):
        return (zf[:, _ZF_K * kw:(_ZF_K + 1) * kw].reshape(B, L, MB_KV_HEADS, MB_HD),
                zf[:, _ZF_V * kw:(_ZF_V + 1) * kw].reshape(B, L, MB_KV_HEADS, MB_HD))

    prompt_kv = []

    def moba_p(za, zf):
        ka, va, means, ko, vo = _moba_prep(zf, Lp, _ZF_K, _ZF_V)
        prompt_kv.extend([ko, vo])
        return _moba_prompt(za, Lp, 2 * _ZA_MQ, ka, va, means, bias_own, bias_prev)

    mem = mem_prompt.reshape(Bp * mem_prompt.shape[1], D)
    mkv = _matmul(mem, wmkv, tm=256, tn=512)
    mshape = (Bp, mem_prompt.shape[1], MX_HEADS, D // MX_HEADS)
    mkp, mvp = mkv[:, :D].reshape(mshape), mkv[:, D:].reshape(mshape)
    xblock = (norm_x_g[l], wxq, mkp[0].swapaxes(0, 1).astype(BF16), mvp[0].swapaxes(0, 1).astype(BF16), wxo)

    s0 = jnp.zeros((Bp, HG_HEADS, HG_DK, HG_DV), F32)
    hp, Sp, zfp = _mixer(xp, Bp, Lp, math.gcd(Lp, 64), s0, moba_p, W, BF16, xblock=xblock)
    kp, vp = (a.reshape(Bp, Lp, MB_KV_HEADS, MB_HD) for a in prompt_kv)
    yp = _ffn(hp, norm_ffn_g[l], wg, wu, wd, final_norm_g)

    xs = x_sample.reshape(DB * T, D)

    def moba_s(za, zf):
        return _moba_sample(za, zf, page_table, cache_k[l], cache_v[l], bias_own, bias_prev, DB, T,
                            _ZA_MQ, _ZF_K, _ZF_V)

    hs, Ss, zfs = _mixer(xs, DB, T, T, state_hgrn[l], moba_s, W, F32)
    ksn, vsn = new_kv(zfs, DB, T)
    qxs = _matmul(hs, wxq, g=norm_x_g[l], tn=512)
    oxs = _xattn(qxs, cache_mem_k[l], cache_mem_v[l], DB, T)
    hs = _matmul(oxs, wxo, res=hs, tn=512)
    ys = _ffn(hs, norm_ffn_g[l], wg, wu, wd, final_norm_g)

    return (yp.reshape(Bp, Lp, D), ys.reshape(DB, T, D),
            Sp[None].astype(state_hgrn.dtype), kp[None], vp[None],
            mkp[None], mvp[None],
            Ss[None].astype(state_hgrn.dtype), ksn[None], vsn[None])
```

```python
import functools
import math

import jax
import jax.numpy as jnp
import numpy as np
from jax import lax
from jax.experimental import pallas as pl
from jax.experimental.pallas import tpu as pltpu

F32 = jnp.float32
BF16 = jnp.bfloat16

EPS = 1e-6
HG_HEADS = 8
HG_DK = 128
HG_DV = 128
HG_SUB = 16
MB_HEADS = 8
MB_KV_HEADS = 2
MB_GROUP = MB_HEADS // MB_KV_HEADS
MB_HD = 128
MB_BLOCK = 256
MB_TOPK = 3
MB_UNROLL = 16
RB_BUCKETS = 32
RB_MAX_DIST = 128
MX_HEADS = 4
PAGE_SIZE = 128
NEG = -1e30
LOG2E = math.log2(math.e)

_ZA_HQ, _ZA_HI, _ZA_HG, _ZA_MQ, _ZA_GA = 0, 1, 2, 3, 4
_ZF_HF = 0
_ZF_K, _ZF_V = 4, 5
LANES = 128
VMEM_LIMIT = 56 * 1024 * 1024


def _params(*sem):
    return pltpu.CompilerParams(dimension_semantics=sem, vmem_limit_bytes=VMEM_LIMIT)


def _sigmoid(x):
    return 1.0 / (1.0 + jnp.exp(-x))


def _silu(x):
    return x * _sigmoid(x)


def _rms(x, g):
    return x * lax.rsqrt(jnp.mean(x * x, axis=-1, keepdims=True) + EPS) * g


def _matmul_kernel(*refs, norm, residual):
    refs = list(refs)
    x_ref = refs.pop(0)
    g_ref = refs.pop(0) if norm else None
    w_ref = refs.pop(0)
    r_ref = refs.pop(0) if residual else None
    o_ref, xs_ref = refs

    @pl.when(pl.program_id(1) == 0)
    def _():
        x = x_ref[...]
        if norm:
            x = _rms(x, g_ref[...])
        xs_ref[...] = x.astype(BF16)

    acc = jnp.dot(xs_ref[...], w_ref[...], preferred_element_type=F32)
    if residual:
        acc = acc + r_ref[...]
    o_ref[...] = acc.astype(o_ref.dtype)


def _matmul(x, w, g=None, res=None, tm=512, tn=512, out_dtype=F32):
    M, K = x.shape
    N = w.shape[1]
    tm = min(tm, M)
    tn = min(tn, N)
    assert M % tm == 0 and N % tn == 0
    args = [x]
    specs = [pl.BlockSpec((tm, K), lambda i, j: (i, 0))]
    if g is not None:
        args.append(g.reshape(1, K))
        specs.append(pl.BlockSpec((1, K), lambda i, j: (0, 0)))
    args.append(w)
    specs.append(pl.BlockSpec((K, tn), lambda i, j: (0, j)))
    if res is not None:
        args.append(res)
        specs.append(pl.BlockSpec((tm, tn), lambda i, j: (i, j)))
    return pl.pallas_call(
        functools.partial(_matmul_kernel, norm=g is not None, residual=res is not None),
        grid=(M // tm, N // tn),
        in_specs=specs,
        out_specs=pl.BlockSpec((tm, tn), lambda i, j: (i, j)),
        out_shape=jax.ShapeDtypeStruct((M, N), out_dtype),
        scratch_shapes=[pltpu.VMEM((tm, K), BF16)],
        compiler_params=_params("parallel", "arbitrary"),
        name="matmul",
    )(*args)


def _proj_in_kernel(x_ref, g_ref, w_ref, oa_ref, of_ref, *, tn):
    xs = _rms(x_ref[...], g_ref[...]).astype(BF16)
    a_cols = oa_ref.shape[1]
    for j in range(w_ref.shape[1] // tn):
        acc = jnp.dot(xs, w_ref[:, j * tn:(j + 1) * tn], preferred_element_type=F32)
        if j * tn < a_cols:
            oa_ref[:, j * tn:(j + 1) * tn] = acc.astype(oa_ref.dtype)
        else:
            of_ref[:, j * tn - a_cols:(j + 1) * tn - a_cols] = acc


def _proj_in(x, g, w, a_cols, a_dtype, tm=512, tn=512):
    M, K = x.shape
    N = w.shape[1]
    tm = min(tm, M)
    assert M % tm == 0 and a_cols % tn == 0 and N % tn == 0
    return pl.pallas_call(
        functools.partial(_proj_in_kernel, tn=tn),
        grid=(M // tm,),
        in_specs=[pl.BlockSpec((tm, K), lambda i: (i, 0)),
                  pl.BlockSpec((1, K), lambda i: (0, 0)),
                  pl.BlockSpec((K, N), lambda i: (0, 0), pipeline_mode=pl.Buffered(1))],
        out_specs=[pl.BlockSpec((tm, a_cols), lambda i: (i, 0)),
                   pl.BlockSpec((tm, N - a_cols), lambda i: (i, 0))],
        out_shape=[jax.ShapeDtypeStruct((M, a_cols), a_dtype),
                   jax.ShapeDtypeStruct((M, N - a_cols), F32)],
        compiler_params=_params("parallel"),
        name="proj_in",
    )(x, g.reshape(1, K), w)


def _subblock_cumsum(g, sub):
    row = lax.broadcasted_iota(jnp.int32, g.shape, 0) % sub
    s = 1
    while s < sub:
        g = g + jnp.where(row >= s, pltpu.roll(g, s, 0), 0.0)
        s *= 2
    return g


def _gla_kernel(q_ref, f_ref, i_ref, g_ref, lbl_ref, ng_ref, s0_ref, o_ref, so_ref, st_ref, *,
                C, sub, per_step):
    c = pl.program_id(1)
    nsub = C // sub
    mid = sub // 2 - 1

    @pl.when(c == 0)
    def _():
        st_ref[...] = s0_ref[0]

    lbl = lbl_ref[...]
    e = jnp.exp(lbl - jnp.max(lbl, axis=0, keepdims=True))
    lb_all = e[0:1] / jnp.sum(e, axis=0, keepdims=True)
    ng = ng_ref[...]

    row = lax.broadcasted_iota(jnp.int32, (C, C), 0)
    col = lax.broadcasted_iota(jnp.int32, (C, C), 1)
    diag_mask = (row // sub == col // sub) & (row >= col)

    W = HG_HEADS * HG_DK
    sls = [slice(h * HG_DK, (h + 1) * HG_DK) for h in range(HG_HEADS)]
    nt = (((1,), (1,)), ((), ()))

    def elementwise(rows):
        q3 = q_ref[rows, :].astype(F32).reshape(nsub, sub, W)
        f = lb_all + (1.0 - lb_all) * _sigmoid(f_ref[rows, :])
        k3 = (1.0 - f).reshape(nsub, sub, W)
        cs = _subblock_cumsum(jnp.log(f), sub).reshape(nsub, sub, W)

        T = cs[:, sub - 1:sub, :]
        m = cs[:, mid:mid + 1, :]
        qd = q3 * jnp.exp(cs - m)
        kd = k3 * jnp.exp(m - cs)
        qt = qd * jnp.exp(m)
        kh = kd * jnp.exp(T - m)

        PT = [jnp.zeros((1, W), F32)]
        for I in range(nsub):
            PT.append(PT[-1] + T[I])

        def scaled_k(I):
            parts = []
            for J in range(nsub):
                if J < I:
                    parts.append(kh[J] * jnp.exp(PT[I] - PT[J + 1]))
                else:
                    parts.append(jnp.zeros((sub, W), F32))
            return jnp.concatenate(parts, axis=0).astype(BF16) if nsub > 1 else parts[0].astype(BF16)

        return dict(
            qd=qd.reshape(C, W).astype(BF16), kd=kd.reshape(C, W).astype(BF16),
            qt=[qt[I].astype(BF16) for I in range(nsub)],
            ks=[scaled_k(I) for I in range(1, nsub + 1)],
            qc=jnp.concatenate([qt[I] * jnp.exp(PT[I]) for I in range(nsub)], axis=0).astype(BF16),
            decay=jnp.exp(PT[nsub]),
            v=i_ref[rows, :].astype(BF16),
            gate=_silu(g_ref[rows, :].astype(F32)))

    def intra(e):
        atts = []
        for sl in sls:
            att = jnp.where(diag_mask,
                            lax.dot_general(e["qd"][:, sl], e["kd"][:, sl], nt, preferred_element_type=F32), 0.0)
            if nsub > 1:
                blocks = [jnp.zeros((sub, C), F32)]
                for I in range(1, nsub):
                    blocks.append(lax.dot_general(e["qt"][I][:, sl], e["ks"][I - 1][:, sl], nt,
                                                  preferred_element_type=F32))
                att = att + jnp.concatenate(blocks, axis=0)
            atts.append(att.astype(BF16))
        return atts

    def recur(e, atts, rows):
        sts = [st_ref[h] for h in range(HG_HEADS)]
        outs = [jnp.dot(atts[h], e["v"][:, sl], preferred_element_type=F32)
                + jnp.dot(e["qc"][:, sl], sts[h].astype(BF16), preferred_element_type=F32)
                for h, sl in enumerate(sls)]
        for h, sl in enumerate(sls):
            dcol = jnp.transpose(jnp.broadcast_to(e["decay"][:, sl], (8, HG_DK)))[:, 0:1]
            st_ref[h] = (sts[h] * dcol
                         + lax.dot_general(e["ks"][-1][:, sl], e["v"][:, sl], (((0,), (0,)), ((), ())),
                                           preferred_element_type=F32))
        for h, sl in enumerate(sls):
            o = outs[h]
            o = o * lax.rsqrt(jnp.mean(o * o, axis=-1, keepdims=True) + EPS) * ng * e["gate"][:, sl]
            o_ref[rows, sl] = o.astype(o_ref.dtype)

    chunk_rows = [slice(u * C, (u + 1) * C) for u in range(per_step)]
    parts = [elementwise(rows) for rows in chunk_rows]
    atts = [intra(e) for e in parts]
    for e, a, rows in zip(parts, atts, chunk_rows):
        recur(e, a, rows)

    @pl.when(c == pl.num_programs(1) - 1)
    def _():
        so_ref[0] = st_ref[...]


def _gla(za, zf, lb_logits, norm_g, s0, B, L, C):
    per_step = math.gcd(L // C, 2)
    R = per_step * C
    nc = L // R
    sub = min(HG_SUB, C)
    W = HG_HEADS * HG_DK

    def zspec(j):
        return pl.BlockSpec((R, W), lambda b, c, j=j: (b * nc + c, j))

    return pl.pallas_call(
        functools.partial(_gla_kernel, C=C, sub=sub, per_step=per_step),
        grid=(B, nc),
        in_specs=[zspec(_ZA_HQ), zspec(_ZF_HF), zspec(_ZA_HI), zspec(_ZA_HG),
                  pl.BlockSpec((2, W), lambda b, c: (0, 0)),
                  pl.BlockSpec((1, HG_DV), lambda b, c: (0, 0)),
                  pl.BlockSpec((1, HG_HEADS, HG_DK, HG_DV), lambda b, c: (b, 0, 0, 0))],
        out_specs=[pl.BlockSpec((R, W), lambda b, c: (b * nc + c, 0)),
                   pl.BlockSpec((1, HG_HEADS, HG_DK, HG_DV), lambda b, c: (b, 0, 0, 0))],
        out_shape=[jax.ShapeDtypeStruct((B * L, W), za.dtype),
                   jax.ShapeDtypeStruct((B, HG_HEADS, HG_DK, HG_DV), F32)],
        scratch_shapes=[pltpu.VMEM((HG_HEADS, HG_DK, HG_DV), F32)],
        compiler_params=_params("arbitrary", "arbitrary"),
        name="gla",
    )(za, zf, za, za, lb_logits, norm_g.reshape(1, HG_DV), s0)


def _bucket_upper_bounds():
    max_exact = RB_BUCKETS // 2
    d = np.arange(1, 4 * RB_MAX_DIST)
    large = max_exact + (np.log(d.astype(np.float32) / max_exact) / math.log(RB_MAX_DIST / max_exact)
                         * (RB_BUCKETS - max_exact)).astype(np.int32)
    bucket = np.where(d < max_exact, d, np.minimum(large, RB_BUCKETS - 1))
    bucket = np.concatenate([[0], bucket])
    return [int(np.max(np.nonzero(bucket == b)[0])) for b in range(RB_BUCKETS - 1)]


def _bias_kernel(rb_ref, own_ref, prev_ref):
    h = pl.program_id(0)
    i = lax.broadcasted_iota(jnp.int32, (MB_BLOCK, MB_BLOCK), 0)
    j = lax.broadcasted_iota(jnp.int32, (MB_BLOCK, MB_BLOCK), 1)
    dmax = _bucket_upper_bounds()

    def table(d):
        val = jnp.full(d.shape, rb_ref[RB_BUCKETS - 1, h], F32)
        for b in range(RB_BUCKETS - 2, -1, -1):
            val = jnp.where(d <= dmax[b], rb_ref[b, h], val)
        return val

    d = i - j
    own_ref[0] = jnp.where(d >= 0, table(d), NEG)
    prev_ref[0] = table(d + MB_BLOCK)


def _bias_tiles(rel_bias):
    shp = jax.ShapeDtypeStruct((MB_HEADS, MB_BLOCK, MB_BLOCK), F32)
    spec = pl.BlockSpec((1, MB_BLOCK, MB_BLOCK), lambda h: (h, 0, 0))
    return pl.pallas_call(
        _bias_kernel,
        grid=(MB_HEADS,),
        in_specs=[pl.BlockSpec(memory_space=pltpu.SMEM)],
        out_specs=[spec, spec],
        out_shape=[shp, shp],
        compiler_params=_params("arbitrary"),
        name="bias_tiles",
    )(rel_bias)


def _top_select(gate, lane, axis=1):
    sel = jnp.zeros(gate.shape, F32)
    lane = lane.astype(F32)
    for _ in range(MB_TOPK):
        m = jnp.max(gate, axis=axis, keepdims=True)
        idx = jnp.min(jnp.where(gate == m, lane, float(2 ** 30)), axis=axis, keepdims=True)
        hit = lane == idx
        found = jnp.where(m > -jnp.inf, 1.0, 0.0)
        sel = jnp.maximum(sel, jnp.where(hit, found, 0.0))
        gate = jnp.where(hit, -jnp.inf, gate)
    return sel > 0.0


def _moba_prep_kernel(k_ref, v_ref, ka_ref, vb_ref, mean_ref, ko_ref, vo_ref, *, per_step):
    rows = per_step * MB_BLOCK
    block = (lax.broadcasted_iota(jnp.int32, (rows, LANES), 0) // MB_BLOCK
             + pl.program_id(0) * per_step)
    lane = lax.broadcasted_iota(jnp.int32, (rows, LANES), 1)
    onehot = jnp.where(lane == block, 1.0, 0.0).astype(BF16)
    ones = jnp.ones((rows, LANES), BF16)
    for g in range(MB_KV_HEADS):
        k = k_ref[:, g * MB_HD:(g + 1) * MB_HD]
        v = v_ref[:, g * MB_HD:(g + 1) * MB_HD]
        ka_ref[g] = jnp.concatenate([k.astype(BF16), onehot], axis=1)
        vb_ref[g] = jnp.concatenate([v.astype(BF16), ones], axis=1)
        mean_ref[g] = jnp.mean(k.reshape(per_step, MB_BLOCK, MB_HD), axis=1)
        ko_ref[pl.ds(g, rows, stride=MB_KV_HEADS), :] = k
        vo_ref[pl.ds(g, rows, stride=MB_KV_HEADS), :] = v


def _moba_prep(z, L, kcol, vcol):
    nb = L // MB_BLOCK
    per_step = math.gcd(nb, 8)
    rows = per_step * MB_BLOCK
    KVH = MB_KV_HEADS
    return pl.pallas_call(
        functools.partial(_moba_prep_kernel, per_step=per_step),
        grid=(nb // per_step,),
        in_specs=[pl.BlockSpec((rows, KVH * MB_HD), lambda n: (n, kcol)),
                  pl.BlockSpec((rows, KVH * MB_HD), lambda n: (n, vcol))],
        out_specs=[pl.BlockSpec((KVH, rows, 2 * MB_HD), lambda n: (0, n, 0)),
                   pl.BlockSpec((KVH, rows, 2 * MB_HD), lambda n: (0, n, 0)),
                   pl.BlockSpec((KVH, per_step, MB_HD), lambda n: (0, n, 0)),
                   pl.BlockSpec((rows * KVH, MB_HD), lambda n: (n, 0)),
                   pl.BlockSpec((rows * KVH, MB_HD), lambda n: (n, 0))],
        out_shape=[jax.ShapeDtypeStruct((KVH, L, 2 * MB_HD), BF16),
                   jax.ShapeDtypeStruct((KVH, L, 2 * MB_HD), BF16),
                   jax.ShapeDtypeStruct((KVH, nb, MB_HD), F32),
                   jax.ShapeDtypeStruct((L * KVH, MB_HD), F32),
                   jax.ShapeDtypeStruct((L * KVH, MB_HD), F32)],
        compiler_params=_params("arbitrary"),
        name="moba_prep",
    )(z, z)


def _moba_prompt_kernel(q_ref, ka_ref, va_ref, mean_ref, own_ref, prev_ref, o_ref,
                        qa_ref, m_ref, acc_ref, s_ref):
    i = pl.program_id(1)
    G, B = MB_GROUP, MB_BLOCK
    R = G * B
    scale = MB_HD ** -0.5
    means = mean_ref[0]
    nbp = means.shape[0]
    blk = lax.broadcasted_iota(jnp.int32, (nbp, B), 0)
    no_block = jnp.zeros((LANES - nbp, B), F32)

    m_hi = means.astype(BF16)
    r1 = means - m_hi.astype(F32)
    m_mid = r1.astype(BF16)
    m_lo = (r1 - m_mid.astype(F32)).astype(BF16)
    nt = (((1,), (1,)), ((), ()))

    for j in range(G):
        qb = q_ref[:, j * MB_HD:(j + 1) * MB_HD]
        q = qb.astype(F32)
        gate = (lax.dot_general(m_hi, qb, nt, preferred_element_type=F32)
                + lax.dot_general(m_mid, qb, nt, preferred_element_type=F32)
                + lax.dot_general(m_lo, qb, nt, preferred_element_type=F32))
        sel = _top_select(jnp.where(blk < i, gate, -jnp.inf), blk, axis=0)
        b31 = prev_ref[j, 0:1, 0:1]
        mb = jnp.where(sel, jnp.where(blk == i - 1, 0.0, b31), NEG)
        mb = jnp.where(blk == i, 0.0, mb)
        mb = jnp.concatenate([mb, no_block], axis=0).T
        qa_ref[j * B:(j + 1) * B, :] = jnp.concatenate(
            [(q * (scale * LOG2E)).astype(BF16), (mb * LOG2E).astype(BF16)], axis=1)

    def block_rows(n):
        return pl.ds(pl.multiple_of(n * B, B), B)

    def qk(n):
        return lax.dot_general(qa_ref[...], ka_ref[0, block_rows(n), :], (((1,), (1,)), ((), ())),
                               preferred_element_type=F32)

    def softmax_pv(s, n, first):
        vn = va_ref[0, block_rows(n), :]
        mx = jnp.max(s, axis=1, keepdims=True)
        if first:
            m_new = jnp.broadcast_to(mx, (R, LANES))
            p = jnp.exp2(s - mx)
            acc_ref[...] = jnp.dot(p.astype(BF16), vn, preferred_element_type=F32)
        else:
            m_prev = m_ref[...]
            m_new = jnp.maximum(m_prev, mx)
            alpha = jnp.exp2(m_prev - m_new)
            p = jnp.exp2(s - jnp.concatenate([m_new, m_new], axis=1))
            acc_ref[...] = (jnp.concatenate([alpha, alpha], axis=1) * acc_ref[...]
                            + jnp.dot(p.astype(BF16), vn, preferred_element_type=F32))
        m_ref[...] = m_new

    def own_logits():
        return qk(i) + own_ref[...].reshape(R, B) * LOG2E

    def prev_logits():
        return qk(i - 1) + prev_ref[...].reshape(R, B) * LOG2E

    @pl.when(i == 0)
    def _():
        softmax_pv(own_logits(), i, True)

    @pl.when(i == 1)
    def _():
        softmax_pv(own_logits(), i, True)
        softmax_pv(prev_logits(), i - 1, False)

    @pl.when(i >= 2)
    def _():
        s_own = own_logits()
        s_prev = prev_logits()
        softmax_pv(s_own, i, True)
        s_ref[...] = qk(0)
        softmax_pv(s_prev, i - 1, False)
        last = i - 2

        def run(n0, count):
            s_cur = s_ref[...]
            for u in range(count):
                s_next = qk(jnp.minimum(n0 + u + 1, last))
                softmax_pv(s_cur, n0 + u, False)
                s_cur = s_next
            s_ref[...] = s_cur

        def unrolled(t, carry):
            run(MB_UNROLL * t, MB_UNROLL)
            return carry

        total = i - 1
        n_main = total // MB_UNROLL
        lax.fori_loop(0, n_main, unrolled, 0)

        base = n_main * MB_UNROLL
        piece = MB_UNROLL // 2
        while piece >= 1:
            take = (total & piece) != 0

            @pl.when(take)
            def _(base=base, piece=piece):
                run(base, piece)

            base = base + jnp.where(take, piece, 0)
            piece //= 2

    acc = acc_ref[...]
    o = acc[:, :MB_HD] / acc[:, MB_HD:]
    for j in range(G):
        o_ref[:, j * MB_HD:(j + 1) * MB_HD] = o[j * B:(j + 1) * B].astype(o_ref.dtype)


def _moba_prompt(z, L, qcol, ka, vb, means, bias_own, bias_prev):
    nb = L // MB_BLOCK
    G, B = MB_GROUP, MB_BLOCK
    assert nb <= LANES, "block ids ride a 128-lane one-hot"
    assert z.dtype == BF16, "the gate matmul takes q exactly as stored"
    nbp = -(-nb // 8) * 8
    means_p = jnp.pad(means, ((0, 0), (0, nbp - nb), (0, 0)))
    return pl.pallas_call(
        _moba_prompt_kernel,
        grid=(MB_KV_HEADS, nb),
        in_specs=[pl.BlockSpec((B, G * MB_HD), lambda g, i: (i, qcol + g)),
                  pl.BlockSpec((1, L, 2 * MB_HD), lambda g, i: (g, 0, 0), pipeline_mode=pl.Buffered(1)),
                  pl.BlockSpec((1, L, 2 * MB_HD), lambda g, i: (g, 0, 0), pipeline_mode=pl.Buffered(1)),
                  pl.BlockSpec((1, nbp, MB_HD), lambda g, i: (g, 0, 0)),
                  pl.BlockSpec((G, B, B), lambda g, i: (g, 0, 0)),
                  pl.BlockSpec((G, B, B), lambda g, i: (g, 0, 0))],
        out_specs=pl.BlockSpec((B, G * MB_HD), lambda g, i: (i, g)),
        out_shape=jax.ShapeDtypeStruct((L, MB_HEADS * MB_HD), z.dtype),
        scratch_shapes=[pltpu.VMEM((G * B, 2 * MB_HD), BF16),
                        pltpu.VMEM((G * B, LANES), F32),
                        pltpu.VMEM((G * B, 2 * MB_HD), F32),
                        pltpu.VMEM((G * B, B), F32)],
        compiler_params=_params("arbitrary", "arbitrary"),
        name="moba_prompt",
    )(z, ka, vb, means_p, bias_own, bias_prev)


def _moba_sample_kernel(pt_ref, q_ref, kn_ref, vn_ref, ck_hbm, cv_hbm, own_ref, adj_ref, b31_ref, o_ref,
                        kbuf, vbuf, lg_ref, sem, *, T, n_pages, n_seq):
    b = pl.program_id(0)
    B = MB_BLOCK
    past = n_pages * PAGE_SIZE
    nb = past // B
    R = MB_HEADS * T
    half = R // MB_KV_HEADS
    scale = MB_HD ** -0.5

    KVH = MB_KV_HEADS
    prow = PAGE_SIZE * KVH

    def page_copies(seq, slot, j):
        dst = pl.ds(pl.multiple_of(j * prow, prow), prow)
        page = pt_ref[seq, j]
        return (pltpu.make_async_copy(ck_hbm.at[page], kbuf.at[slot, dst, :], sem.at[0, slot]),
                pltpu.make_async_copy(cv_hbm.at[page], vbuf.at[slot, dst, :], sem.at[1, slot]))

    def start_seq(seq, slot):
        def body(j, c):
            ck, cv = page_copies(seq, slot, j)
            ck.start()
            cv.start()
            return c
        lax.fori_loop(0, n_pages, body, 0)

    def wait_seq(seq, slot):
        def body(j, c):
            ck, cv = page_copies(seq, slot, j)
            ck.wait()
            cv.wait()
            return c
        lax.fori_loop(0, n_pages, body, 0)

    slot = b % 2

    @pl.when(b == 0)
    def _():
        start_seq(0, 0)

    @pl.when(b + 1 < n_seq)
    def _():
        start_seq(b + 1, 1 - slot)

    wait_seq(b, slot)

    def block(buf, n, g):
        return buf[slot, pl.ds(n * B * KVH + g, B, stride=KVH), :]

    q = q_ref[...]
    Q = jnp.concatenate([q[:, h * MB_HD:(h + 1) * MB_HD] for h in range(MB_HEADS)], axis=0)
    qf = Q * scale
    qs = qf.astype(BF16)
    grow = [slice(g * half, (g + 1) * half) for g in range(KVH)]

    mean_rows = [[] for _ in range(KVH)]
    for n in range(nb):
        for g in range(KVH):
            kf = block(kbuf, n, g)
            mean_rows[g].append(jnp.mean(kf, axis=0, keepdims=True))
            lg_ref[grow[g], n * B:(n + 1) * B] = lax.dot_general(
                qs[grow[g]], kf.astype(BF16), (((1,), (1,)), ((), ())), preferred_element_type=F32)

    gates = []
    for g in range(KVH):
        means = jnp.concatenate(mean_rows[g] + [jnp.zeros((LANES - nb, MB_HD), F32)], axis=0)
        gates.append(lax.dot_general(Q[grow[g]], means, (((1,), (1,)), ((), ())),
                                     precision=lax.Precision.HIGHEST, preferred_element_type=F32))
    gate = jnp.concatenate(gates, axis=0)
    lane = lax.broadcasted_iota(jnp.int32, (R, LANES), 1)
    sel = _top_select(jnp.where(lane < nb, gate, -jnp.inf), lane)
    mb = jnp.where(sel, jnp.where(lane == nb - 1, 0.0, b31_ref[...]), NEG)
    adj = adj_ref[...]

    def logits(n):
        s = lg_ref[:, n * B:(n + 1) * B] + mb[:, n:n + 1]
        return s + adj if n == nb - 1 else s

    kn = kn_ref[...]
    vn = vn_ref[...]
    s_own = jnp.concatenate(
        [lax.dot_general(qf[grow[g]], kn[:, g * MB_HD:(g + 1) * MB_HD], (((1,), (1,)), ((), ())),
                         preferred_element_type=F32) for g in range(KVH)], axis=0) + own_ref[...]

    m = jnp.max(s_own, axis=1, keepdims=True)
    for n in range(nb):
        m = jnp.maximum(m, jnp.max(logits(n), axis=1, keepdims=True))

    p_own = jnp.exp(s_own - m)
    l = jnp.sum(p_own, axis=1, keepdims=True)
    acc = [jnp.dot(p_own[grow[g]], vn[:, g * MB_HD:(g + 1) * MB_HD], preferred_element_type=F32)
           for g in range(KVH)]
    for n in range(nb):
        p = jnp.exp(logits(n) - m)
        l = l + jnp.sum(p, axis=1, keepdims=True)
        pb = p.astype(BF16)
        for g in range(KVH):
            acc[g] = acc[g] + jnp.dot(pb[grow[g]], block(vbuf, n, g).astype(BF16),
                                      preferred_element_type=F32)
    rows = jnp.concatenate(acc, axis=0) / l
    o_ref[...] = jnp.concatenate([rows[h * T:(h + 1) * T] for h in range(MB_HEADS)], axis=1)


def _moba_sample(z, zf, page_table, cache_k, cache_v, bias_own, bias_prev, DB, T, qcol, kcol, vcol):
    n_pages = page_table.shape[1]
    past = n_pages * PAGE_SIZE
    R = MB_HEADS * T
    W = MB_HEADS * MB_HD
    KW = MB_KV_HEADS * MB_HD
    ck = cache_k.reshape(cache_k.shape[0], PAGE_SIZE * MB_KV_HEADS, MB_HD)
    cv = cache_v.reshape(cache_v.shape[0], PAGE_SIZE * MB_KV_HEADS, MB_HD)
    own = bias_own[:, :T, :T].reshape(R, T)
    adj = bias_prev[:, :T, :].reshape(R, MB_BLOCK)
    b31 = bias_prev[:, :T, 0:1].reshape(R, 1)
    const = lambda b, pt: (0, 0)
    grid_spec = pltpu.PrefetchScalarGridSpec(
        num_scalar_prefetch=1,
        grid=(DB,),
        in_specs=[pl.BlockSpec((T, W), lambda b, pt: (b, qcol)),
                  pl.BlockSpec((T, KW), lambda b, pt: (b, kcol)),
                  pl.BlockSpec((T, KW), lambda b, pt: (b, vcol)),
                  pl.BlockSpec(memory_space=pl.ANY),
                  pl.BlockSpec(memory_space=pl.ANY),
                  pl.BlockSpec((R, T), const),
                  pl.BlockSpec((R, MB_BLOCK), const),
                  pl.BlockSpec((R, 1), const)],
        out_specs=pl.BlockSpec((T, W), lambda b, pt: (b, 0)),
        scratch_shapes=[pltpu.VMEM((2, past * MB_KV_HEADS, MB_HD), F32),
                        pltpu.VMEM((2, past * MB_KV_HEADS, MB_HD), F32),
                        pltpu.VMEM((R, past), F32),
                        pltpu.SemaphoreType.DMA((2, 2))],
    )
    return pl.pallas_call(
        functools.partial(_moba_sample_kernel, T=T, n_pages=n_pages, n_seq=DB),
        grid_spec=grid_spec,
        out_shape=jax.ShapeDtypeStruct((DB * T, W), F32),
        compiler_params=_params("arbitrary"),
        name="moba_sample",
    )(page_table, z, zf, zf, ck, cv, own, adj, b31)


def _xblock_math(x, g, wq_ref, k_ref, v_ref, wo_ref):
    q = jnp.dot(_rms(x, g).astype(BF16), wq_ref[...], preferred_element_type=F32)
    hd = k_ref.shape[2]
    scale = hd ** -0.5
    outs = []
    for h in range(MX_HEADS):
        qh = (q[:, h * hd:(h + 1) * hd] * scale).astype(BF16)
        s = lax.dot_general(qh, k_ref[h], (((1,), (1,)), ((), ())), preferred_element_type=F32)
        p = jnp.exp(s - jnp.max(s, axis=1, keepdims=True))
        o = jnp.dot(p.astype(BF16), v_ref[h], preferred_element_type=F32)
        outs.append((o / jnp.sum(p, axis=1, keepdims=True)).astype(BF16))
    return x + jnp.dot(jnp.concatenate(outs, axis=1), wo_ref[...], preferred_element_type=F32)


def _merge_kernel(oa_ref, ob_ref, ga_ref, gb_ref, x_ref, wpa_ref, wpb_ref, wo_ref, *rest):
    o_ref = rest[-1]
    a = jnp.dot(oa_ref[...].astype(BF16), wpa_ref[...], preferred_element_type=F32)
    b = jnp.dot(ob_ref[...].astype(BF16), wpb_ref[...], preferred_element_type=F32)
    merged = _sigmoid(ga_ref[...].astype(F32)) * a + _sigmoid(gb_ref[...].astype(F32)) * b
    h = x_ref[...] + jnp.dot(merged.astype(BF16), wo_ref[...], preferred_element_type=F32)
    if len(rest) > 1:
        gx_ref, wq_ref, k_ref, v_ref, wxo_ref = rest[:-1]
        h = _xblock_math(h, gx_ref[...], wq_ref, k_ref, v_ref, wxo_ref)
    o_ref[...] = h


def _merge(oa, ob, z, gcol, x, wpa, wpb, wo, xblock=None, tm=512):
    M, D = x.shape
    tm = min(tm, M)
    row = lambda i: (i, 0)
    const = lambda i: (0, 0)
    once = pl.Buffered(1)
    wspec = pl.BlockSpec((D, D), const, pipeline_mode=once)
    args = [oa, ob, z, z, x, wpa, wpb, wo]
    specs = [pl.BlockSpec((tm, D), row), pl.BlockSpec((tm, D), row),
             pl.BlockSpec((tm, D), lambda i: (i, gcol)), pl.BlockSpec((tm, D), lambda i: (i, gcol + 1)),
             pl.BlockSpec((tm, D), row), wspec, wspec, wspec]
    if xblock is not None:
        gx, wq, mk, mv, wxo = xblock
        args += [gx.reshape(1, D), wq, mk, mv, wxo]
        specs += [pl.BlockSpec((1, D), const), wspec,
                  pl.BlockSpec(mk.shape, lambda i: (0, 0, 0)), pl.BlockSpec(mv.shape, lambda i: (0, 0, 0)), wspec]
    return pl.pallas_call(
        _merge_kernel,
        grid=(M // tm,),
        in_specs=specs,
        out_specs=pl.BlockSpec((tm, D), row),
        out_shape=jax.ShapeDtypeStruct((M, D), F32),
        compiler_params=_params("parallel"),
        name="merge",
    )(*args)


def _xattn_kernel(q_ref, mk_hbm, mv_hbm, o_ref, kbuf, vbuf, sem, *, n_steps, per_step, L):
    b = pl.program_id(0)
    hd = q_ref.shape[1] // MX_HEADS
    scale = hd ** -0.5
    slot = b % 2

    def head_copies(step, slot):
        cps = []
        for j in range(per_step):
            seq = step * per_step + j
            for h in range(MX_HEADS):
                cps.append(pltpu.make_async_copy(mk_hbm.at[seq, :, h, :], kbuf.at[slot, j, h], sem.at[0, slot]))
                cps.append(pltpu.make_async_copy(mv_hbm.at[seq, :, h, :], vbuf.at[slot, j, h], sem.at[1, slot]))
        return cps

    @pl.when(b == 0)
    def _():
        for cp in head_copies(0, 0):
            cp.start()

    @pl.when(b + 1 < n_steps)
    def _():
        for cp in head_copies(b + 1, 1 - slot):
            cp.start()

    for cp in head_copies(b, slot):
        cp.wait()

    q = (q_ref[...] * scale).astype(BF16)
    pairs = [(j, h) for j in range(per_step) for h in range(MX_HEADS)]
    s = [lax.dot_general(q[j * L:(j + 1) * L, h * hd:(h + 1) * hd], kbuf[slot, j, h].astype(BF16),
                         (((1,), (1,)), ((), ())), preferred_element_type=F32) for j, h in pairs]
    p = [jnp.exp(x - jnp.max(x, axis=1, keepdims=True)) for x in s]
    o = [jnp.dot(p[c].astype(BF16), vbuf[slot, j, h].astype(BF16), preferred_element_type=F32)
         / jnp.sum(p[c], axis=1, keepdims=True) for c, (j, h) in enumerate(pairs)]
    o_ref[...] = jnp.concatenate(
        [jnp.concatenate(o[j * MX_HEADS:(j + 1) * MX_HEADS], axis=1) for j in range(per_step)], axis=0)


def _xattn(q, mk, mv, B, L, per_step=4):
    D = q.shape[1]
    mem, hd = mk.shape[1], mk.shape[3]
    per_step = math.gcd(per_step, B)
    n_steps = B // per_step
    rows = per_step * L
    return pl.pallas_call(
        functools.partial(_xattn_kernel, n_steps=n_steps, per_step=per_step, L=L),
        grid=(n_steps,),
        in_specs=[pl.BlockSpec((rows, D), lambda b: (b, 0)),
                  pl.BlockSpec(memory_space=pl.ANY),
                  pl.BlockSpec(memory_space=pl.ANY)],
        out_specs=pl.BlockSpec((rows, D), lambda b: (b, 0)),
        out_shape=jax.ShapeDtypeStruct((B * L, D), F32),
        scratch_shapes=[pltpu.VMEM((2, per_step, MX_HEADS, mem, hd), F32),
                        pltpu.VMEM((2, per_step, MX_HEADS, mem, hd), F32),
                        pltpu.SemaphoreType.DMA((2, 2))],
        compiler_params=_params("arbitrary"),
        name="xattn",
    )(q, mk, mv)


def _ffn_kernel(x_ref, g_ref, wg_ref, wu_ref, wd_ref, fg_ref, o_ref, *, tf):
    x = x_ref[...]
    xn = _rms(x, g_ref[...]).astype(BF16)
    acc = x
    for c in range(wg_ref.shape[1] // tf):
        sl = slice(c * tf, (c + 1) * tf)
        a = jnp.dot(xn, wg_ref[:, sl], preferred_element_type=F32)
        u = jnp.dot(xn, wu_ref[:, sl], preferred_element_type=F32)
        acc = acc + jnp.dot((_silu(a) * u).astype(BF16), wd_ref[sl, :], preferred_element_type=F32)
    o_ref[...] = _rms(acc, fg_ref[...])


def _ffn(x, g, wg, wu, wd, fg, tm=1024, tf=256):
    M, D = x.shape
    FF = wg.shape[1]
    tm = min(tm, M)
    row = lambda i: (i, 0)
    const = lambda i: (0, 0)
    once = pl.Buffered(1)
    return pl.pallas_call(
        functools.partial(_ffn_kernel, tf=tf),
        grid=(M // tm,),
        in_specs=[pl.BlockSpec((tm, D), row), pl.BlockSpec((1, D), const),
                  pl.BlockSpec((D, FF), const, pipeline_mode=once),
                  pl.BlockSpec((D, FF), const, pipeline_mode=once),
                  pl.BlockSpec((FF, D), const, pipeline_mode=once),
                  pl.BlockSpec((1, D), const)],
        out_specs=pl.BlockSpec((tm, D), row),
        out_shape=jax.ShapeDtypeStruct((M, D), F32),
        compiler_params=_params("parallel"),
        name="ffn",
    )(x, g.reshape(1, D), wg, wu, wd, fg.reshape(1, D))


def _mixer(x2, B, L, chunk, s0, moba_fn, W, za_dtype, xblock=None):
    tm = 512 if za_dtype == BF16 else 256
    za, zf = _proj_in(x2, W["norm_mix_g"], W["w_in"], W["a_cols"], za_dtype, tm=tm)
    oa, S = _gla(za, zf, W["lb_logits"], W["hg_norm_g"], s0, B, L, chunk)
    ob = moba_fn(za, zf)
    h = _merge(oa, ob, za, _ZA_GA, x2, W["w_pa"], W["w_pb"], W["w_out"], xblock=xblock)
    return h, S, zf


def kernel(x_prompt, x_sample, cache_k, cache_v, cache_mem_k, cache_mem_v, state_hgrn, page_table, mem_prompt, norm_mix_g, w_in, hg_lb_logits, hg_norm_g, w_proj_a, w_proj_b, w_out, rel_bias, norm_x_g, w_xq, w_mk, w_mv, w_xo, norm_ffn_g, w_gate, w_up, w_down, final_norm_g):
    Bp, Lp, D = x_prompt.shape
    DB, T, _ = x_sample.shape
    l = 0
    win = w_in[l]
    hw = HG_HEADS * HG_DK
    hq, hf, hi, hg = (win[:, j * hw:(j + 1) * hw] for j in range(4))
    c0 = 4 * hw + MB_HEADS * MB_HD
    c1 = c0 + 2 * MB_KV_HEADS * MB_HD
    W = {
        "w_in": jnp.concatenate([hq, hi, hg, win[:, 4 * hw:c0], win[:, c1:], hf, win[:, c0:c1]],
                                axis=1).astype(BF16),
        "a_cols": win.shape[1] - hw - (c1 - c0),
        "norm_mix_g": norm_mix_g[l],
        "lb_logits": hg_lb_logits,
        "hg_norm_g": hg_norm_g[l],
        "w_pa": w_proj_a[l].astype(BF16), "w_pb": w_proj_b[l].astype(BF16), "w_out": w_out[l].astype(BF16),
    }
    wxq, wxo = w_xq[l].astype(BF16), w_xo[l].astype(BF16)
    wmkv = jnp.concatenate([w_mk[l], w_mv[l]], axis=1).astype(BF16)
    wg, wu, wd = w_gate[l].astype(BF16), w_up[l].astype(BF16), w_down[l].astype(BF16)

    bias_own, bias_prev = _bias_tiles(rel_bias)

    xp = x_prompt.reshape(Bp * Lp, D)

    assert Bp == 1, "the prompt kernels take one sequence"
    kw = MB_KV_HEADS * MB_HD

    def new_kv(zf, B, L):
        return (zf[:, _ZF_K * kw:(_ZF_K + 1) * kw].reshape(B, L, MB_KV_HEADS, MB_HD),
                zf[:, _ZF_V * kw:(_ZF_V + 1) * kw].reshape(B, L, MB_KV_HEADS, MB_HD))

    prompt_kv = []

    def moba_p(za, zf):
        ka, va, means, ko, vo = _moba_prep(zf, Lp, _ZF_K, _ZF_V)
        prompt_kv.extend([ko, vo])
        return _moba_prompt(za, Lp, 2 * _ZA_MQ, ka, va, means, bias_own, bias_prev)

    mem = mem_prompt.reshape(Bp * mem_prompt.shape[1], D)
    mkv = _matmul(mem, wmkv, tm=256, tn=512)
    mshape = (Bp, mem_prompt.shape[1], MX_HEADS, D // MX_HEADS)
    mkp, mvp = mkv[:, :D].reshape(mshape), mkv[:, D:].reshape(mshape)
    xblock = (norm_x_g[l], wxq, mkp[0].swapaxes(0, 1).astype(BF16), mvp[0].swapaxes(0, 1).astype(BF16), wxo)

    s0 = jnp.zeros((Bp, HG_HEADS, HG_DK, HG_DV), F32)
    hp, Sp, zfp = _mixer(xp, Bp, Lp, math.gcd(Lp, 64), s0, moba_p, W, BF16, xblock=xblock)
    kp, vp = (a.reshape(Bp, Lp, MB_KV_HEADS, MB_HD) for a in prompt_kv)
    yp = _ffn(hp, norm_ffn_g[l], wg, wu, wd, final_norm_g)

    xs = x_sample.reshape(DB * T, D)

    def moba_s(za, zf):
        return _moba_sample(za, zf, page_table, cache_k[l], cache_v[l], bias_own, bias_prev, DB, T,
                            _ZA_MQ, _ZF_K, _ZF_V)

    hs, Ss, zfs = _mixer(xs, DB, T, T, state_hgrn[l], moba_s, W, F32)
    ksn, vsn = new_kv(zfs, DB, T)
    qxs = _matmul(hs, wxq, g=norm_x_g[l], tn=512)
    oxs = _xattn(qxs, cache_mem_k[l], cache_mem_v[l], DB, T)
    hs = _matmul(oxs, wxo, res=hs, tn=512)
    ys = _ffn(hs, norm_ffn_g[l], wg, wu, wd, final_norm_g)

    return (yp.reshape(Bp, Lp, D), ys.reshape(DB, T, D),
            Sp[None].astype(state_hgrn.dtype), kp[None], vp[None],
            mkp[None], mvp[None],
            Ss[None].astype(state_hgrn.dtype), ksn[None], vsn[None])
```

```python
import functools
import math

import jax
import jax.numpy as jnp
import numpy as np
from jax import lax
from jax.experimental import pallas as pl
from jax.experimental.pallas import tpu as pltpu

F32 = jnp.float32
BF16 = jnp.bfloat16

EPS = 1e-6
HG_HEADS = 8
HG_DK = 128
HG_DV = 128
HG_SUB = 16
MB_HEADS = 8
MB_KV_HEADS = 2
MB_GROUP = MB_HEADS // MB_KV_HEADS
MB_HD = 128
MB_BLOCK = 256
MB_TOPK = 3
MB_UNROLL = 16
RB_BUCKETS = 32
RB_MAX_DIST = 128
MX_HEADS = 4
PAGE_SIZE = 128
NEG = -1e30
LOG2E = math.log2(math.e)

_ZA_HQ, _ZA_HI, _ZA_HG, _ZA_MQ, _ZA_GA = 0, 1, 2, 3, 4
_ZF_HF = 0
_ZF_K, _ZF_V = 4, 5
LANES = 128
VMEM_LIMIT = 56 * 1024 * 1024


def _params(*sem):
    return pltpu.CompilerParams(dimension_semantics=sem, vmem_limit_bytes=VMEM_LIMIT)


def _sigmoid(x):
    return 1.0 / (1.0 + jnp.exp(-x))


def _silu(x):
    return x * _sigmoid(x)


def _rms(x, g):
    return x * lax.rsqrt(jnp.mean(x * x, axis=-1, keepdims=True) + EPS) * g


def _matmul_kernel(*refs, norm, residual):
    refs = list(refs)
    x_ref = refs.pop(0)
    g_ref = refs.pop(0) if norm else None
    w_ref = refs.pop(0)
    r_ref = refs.pop(0) if residual else None
    o_ref, xs_ref = refs

    @pl.when(pl.program_id(1) == 0)
    def _():
        x = x_ref[...]
        if norm:
            x = _rms(x, g_ref[...])
        xs_ref[...] = x.astype(BF16)

    acc = jnp.dot(xs_ref[...], w_ref[...], preferred_element_type=F32)
    if residual:
        acc = acc + r_ref[...]
    o_ref[...] = acc.astype(o_ref.dtype)


def _matmul(x, w, g=None, res=None, tm=512, tn=512, out_dtype=F32):
    M, K = x.shape
    N = w.shape[1]
    tm = min(tm, M)
    tn = min(tn, N)
    assert M % tm == 0 and N % tn == 0
    args = [x]
    specs = [pl.BlockSpec((tm, K), lambda i, j: (i, 0))]
    if g is not None:
        args.append(g.reshape(1, K))
        specs.append(pl.BlockSpec((1, K), lambda i, j: (0, 0)))
    args.append(w)
    specs.append(pl.BlockSpec((K, tn), lambda i, j: (0, j)))
    if res is not None:
        args.append(res)
        specs.append(pl.BlockSpec((tm, tn), lambda i, j: (i, j)))
    return pl.pallas_call(
        functools.partial(_matmul_kernel, norm=g is not None, residual=res is not None),
        grid=(M // tm, N // tn),
        in_specs=specs,
        out_specs=pl.BlockSpec((tm, tn), lambda i, j: (i, j)),
        out_shape=jax.ShapeDtypeStruct((M, N), out_dtype),
        scratch_shapes=[pltpu.VMEM((tm, K), BF16)],
        compiler_params=_params("parallel", "arbitrary"),
        name="matmul",
    )(*args)


def _proj_in_kernel(x_ref, g_ref, w_ref, oa_ref, of_ref, *, tn):
    xs = _rms(x_ref[...], g_ref[...]).astype(BF16)
    a_cols = oa_ref.shape[1]
    for j in range(w_ref.shape[1] // tn):
        acc = jnp.dot(xs, w_ref[:, j * tn:(j + 1) * tn], preferred_element_type=F32)
        if j * tn < a_cols:
            oa_ref[:, j * tn:(j + 1) * tn] = acc.astype(oa_ref.dtype)
        else:
            of_ref[:, j * tn - a_cols:(j + 1) * tn - a_cols] = acc


def _proj_in(x, g, w, a_cols, a_dtype, tm=512, tn=512):
    M, K = x.shape
    N = w.shape[1]
    tm = min(tm, M)
    assert M % tm == 0 and a_cols % tn == 0 and N % tn == 0
    return pl.pallas_call(
        functools.partial(_proj_in_kernel, tn=tn),
        grid=(M // tm,),
        in_specs=[pl.BlockSpec((tm, K), lambda i: (i, 0)),
                  pl.BlockSpec((1, K), lambda i: (0, 0)),
                  pl.BlockSpec((K, N), lambda i: (0, 0), pipeline_mode=pl.Buffered(1))],
        out_specs=[pl.BlockSpec((tm, a_cols), lambda i: (i, 0)),
                   pl.BlockSpec((tm, N - a_cols), lambda i: (i, 0))],
        out_shape=[jax.ShapeDtypeStruct((M, a_cols), a_dtype),
                   jax.ShapeDtypeStruct((M, N - a_cols), F32)],
        compiler_params=_params("parallel"),
        name="proj_in",
    )(x, g.reshape(1, K), w)


def _subblock_cumsum(g, sub):
    row = lax.broadcasted_iota(jnp.int32, g.shape, 0) % sub
    s = 1
    while s < sub:
        g = g + jnp.where(row >= s, pltpu.roll(g, s, 0), 0.0)
        s *= 2
    return g


def _gla_kernel(q_ref, f_ref, i_ref, g_ref, lbl_ref, ng_ref, s0_ref, o_ref, so_ref, st_ref, *,
                C, sub, per_step, seqs):
    c = pl.program_id(1)
    nsub = C // sub
    mid = sub // 2 - 1

    @pl.when(c == 0)
    def _():
        st_ref[...] = s0_ref[...]

    lbl = lbl_ref[...]
    e = jnp.exp(lbl - jnp.max(lbl, axis=0, keepdims=True))
    lb_all = e[0:1] / jnp.sum(e, axis=0, keepdims=True)
    ng = ng_ref[...]

    row = lax.broadcasted_iota(jnp.int32, (C, C), 0)
    col = lax.broadcasted_iota(jnp.int32, (C, C), 1)
    diag_mask = (row // sub == col // sub) & (row >= col)

    W = HG_HEADS * HG_DK
    sls = [slice(h * HG_DK, (h + 1) * HG_DK) for h in range(HG_HEADS)]
    nt = (((1,), (1,)), ((), ()))

    def elementwise(rows):
        q3 = q_ref[rows, :].astype(F32).reshape(nsub, sub, W)
        f = lb_all + (1.0 - lb_all) * _sigmoid(f_ref[rows, :])
        k3 = (1.0 - f).reshape(nsub, sub, W)
        cs = _subblock_cumsum(jnp.log(f), sub).reshape(nsub, sub, W)

        T = cs[:, sub - 1:sub, :]
        m = cs[:, mid:mid + 1, :]
        qd = q3 * jnp.exp(cs - m)
        kd = k3 * jnp.exp(m - cs)
        qt = qd * jnp.exp(m)
        kh = kd * jnp.exp(T - m)

        PT = [jnp.zeros((1, W), F32)]
        for I in range(nsub):
            PT.append(PT[-1] + T[I])

        def scaled_k(I):
            parts = []
            for J in range(nsub):
                if J < I:
                    parts.append(kh[J] * jnp.exp(PT[I] - PT[J + 1]))
                else:
                    parts.append(jnp.zeros((sub, W), F32))
            return jnp.concatenate(parts, axis=0).astype(BF16) if nsub > 1 else parts[0].astype(BF16)

        return dict(
            qd=qd.reshape(C, W).astype(BF16), kd=kd.reshape(C, W).astype(BF16),
            qt=[qt[I].astype(BF16) for I in range(nsub)],
            ks=[scaled_k(I) for I in range(1, nsub + 1)],
            qc=jnp.concatenate([qt[I] * jnp.exp(PT[I]) for I in range(nsub)], axis=0).astype(BF16),
            decay=jnp.exp(PT[nsub]),
            v=i_ref[rows, :].astype(BF16),
            gate=_silu(g_ref[rows, :].astype(F32)))

    def intra(e):
        atts = []
        for sl in sls:
            att = jnp.where(diag_mask,
                            lax.dot_general(e["qd"][:, sl], e["kd"][:, sl], nt, preferred_element_type=F32), 0.0)
            if nsub > 1:
                blocks = [jnp.zeros((sub, C), F32)]
                for I in range(1, nsub):
                    blocks.append(lax.dot_general(e["qt"][I][:, sl], e["ks"][I - 1][:, sl], nt,
                                                  preferred_element_type=F32))
                att = att + jnp.concatenate(blocks, axis=0)
            atts.append(att.astype(BF16))
        return atts

    def recur(e, atts, rows, seq):
        sts = [st_ref[seq, h] for h in range(HG_HEADS)]
        outs = [jnp.dot(atts[h], e["v"][:, sl], preferred_element_type=F32)
                + jnp.dot(e["qc"][:, sl], sts[h].astype(BF16), preferred_element_type=F32)
                for h, sl in enumerate(sls)]
        for h, sl in enumerate(sls):
            dcol = jnp.transpose(jnp.broadcast_to(e["decay"][:, sl], (8, HG_DK)))[:, 0:1]
            st_ref[seq, h] = (sts[h] * dcol
                              + lax.dot_general(e["ks"][-1][:, sl], e["v"][:, sl], (((0,), (0,)), ((), ())),
                                                preferred_element_type=F32))
        for h, sl in enumerate(sls):
            o = outs[h]
            o = o * lax.rsqrt(jnp.mean(o * o, axis=-1, keepdims=True) + EPS) * ng * e["gate"][:, sl]
            o_ref[rows, sl] = o.astype(o_ref.dtype)

    units = [(slice((s * per_step + u) * C, (s * per_step + u + 1) * C), s)
             for s in range(seqs) for u in range(per_step)]
    parts = [elementwise(rows) for rows, _ in units]
    atts = [intra(e) for e in parts]
    for e, a, (rows, seq) in zip(parts, atts, units):
        recur(e, a, rows, seq)

    @pl.when(c == pl.num_programs(1) - 1)
    def _():
        so_ref[...] = st_ref[...]


def _gla(za, zf, lb_logits, norm_g, s0, B, L, C):
    per_step = math.gcd(L // C, 4)
    seqs = math.gcd(B, 4 // per_step) if per_step * C == L else 1
    R = seqs * per_step * C
    nc = L // (per_step * C)
    sub = min(HG_SUB, C)
    W = HG_HEADS * HG_DK

    def zspec(j):
        return pl.BlockSpec((R, W), lambda b, c, j=j: (b * nc + c, j))

    sspec = pl.BlockSpec((seqs, HG_HEADS, HG_DK, HG_DV), lambda b, c: (b, 0, 0, 0))
    return pl.pallas_call(
        functools.partial(_gla_kernel, C=C, sub=sub, per_step=per_step, seqs=seqs),
        grid=(B // seqs, nc),
        in_specs=[zspec(_ZA_HQ), zspec(_ZF_HF), zspec(_ZA_HI), zspec(_ZA_HG),
                  pl.BlockSpec((2, W), lambda b, c: (0, 0)),
                  pl.BlockSpec((1, HG_DV), lambda b, c: (0, 0)),
                  sspec],
        out_specs=[pl.BlockSpec((R, W), lambda b, c: (b * nc + c, 0)), sspec],
        out_shape=[jax.ShapeDtypeStruct((B * L, W), za.dtype),
                   jax.ShapeDtypeStruct((B, HG_HEADS, HG_DK, HG_DV), F32)],
        scratch_shapes=[pltpu.VMEM((seqs, HG_HEADS, HG_DK, HG_DV), F32)],
        compiler_params=_params("arbitrary", "arbitrary"),
        name="gla",
    )(za, zf, za, za, lb_logits, norm_g.reshape(1, HG_DV), s0)


def _bucket_upper_bounds():
    max_exact = RB_BUCKETS // 2
    d = np.arange(1, 4 * RB_MAX_DIST)
    large = max_exact + (np.log(d.astype(np.float32) / max_exact) / math.log(RB_MAX_DIST / max_exact)
                         * (RB_BUCKETS - max_exact)).astype(np.int32)
    bucket = np.where(d < max_exact, d, np.minimum(large, RB_BUCKETS - 1))
    bucket = np.concatenate([[0], bucket])
    return [int(np.max(np.nonzero(bucket == b)[0])) for b in range(RB_BUCKETS - 1)]


def _bias_kernel(rb_ref, own_ref, prev_ref):
    h = pl.program_id(0)
    i = lax.broadcasted_iota(jnp.int32, (MB_BLOCK, MB_BLOCK), 0)
    j = lax.broadcasted_iota(jnp.int32, (MB_BLOCK, MB_BLOCK), 1)
    dmax = _bucket_upper_bounds()

    def table(d):
        val = jnp.full(d.shape, rb_ref[RB_BUCKETS - 1, h], F32)
        for b in range(RB_BUCKETS - 2, -1, -1):
            val = jnp.where(d <= dmax[b], rb_ref[b, h], val)
        return val

    d = i - j
    own_ref[0] = jnp.where(d >= 0, table(d), NEG)
    prev_ref[0] = table(d + MB_BLOCK)


def _bias_tiles(rel_bias):
    shp = jax.ShapeDtypeStruct((MB_HEADS, MB_BLOCK, MB_BLOCK), F32)
    spec = pl.BlockSpec((1, MB_BLOCK, MB_BLOCK), lambda h: (h, 0, 0))
    return pl.pallas_call(
        _bias_kernel,
        grid=(MB_HEADS,),
        in_specs=[pl.BlockSpec(memory_space=pltpu.SMEM)],
        out_specs=[spec, spec],
        out_shape=[shp, shp],
        compiler_params=_params("arbitrary"),
        name="bias_tiles",
    )(rel_bias)


def _top_select(gate, lane, axis=1):
    sel = jnp.zeros(gate.shape, F32)
    lane = lane.astype(F32)
    for _ in range(MB_TOPK):
        m = jnp.max(gate, axis=axis, keepdims=True)
        idx = jnp.min(jnp.where(gate == m, lane, float(2 ** 30)), axis=axis, keepdims=True)
        hit = lane == idx
        found = jnp.where(m > -jnp.inf, 1.0, 0.0)
        sel = jnp.maximum(sel, jnp.where(hit, found, 0.0))
        gate = jnp.where(hit, -jnp.inf, gate)
    return sel > 0.0


def _moba_prep_kernel(k_ref, v_ref, ka_ref, vb_ref, mean_ref, ko_ref, vo_ref, *, per_step):
    rows = per_step * MB_BLOCK
    block = (lax.broadcasted_iota(jnp.int32, (rows, LANES), 0) // MB_BLOCK
             + pl.program_id(0) * per_step)
    lane = lax.broadcasted_iota(jnp.int32, (rows, LANES), 1)
    onehot = jnp.where(lane == block, 1.0, 0.0).astype(BF16)
    ones = jnp.ones((rows, LANES), BF16)
    for g in range(MB_KV_HEADS):
        k = k_ref[:, g * MB_HD:(g + 1) * MB_HD]
        v = v_ref[:, g * MB_HD:(g + 1) * MB_HD]
        ka_ref[g] = jnp.concatenate([k.astype(BF16), onehot], axis=1)
        vb_ref[g] = jnp.concatenate([v.astype(BF16), ones], axis=1)
        mean_ref[g] = jnp.mean(k.reshape(per_step, MB_BLOCK, MB_HD), axis=1)
        ko_ref[pl.ds(g, rows, stride=MB_KV_HEADS), :] = k
        vo_ref[pl.ds(g, rows, stride=MB_KV_HEADS), :] = v


def _moba_prep(z, L, kcol, vcol):
    nb = L // MB_BLOCK
    per_step = math.gcd(nb, 8)
    rows = per_step * MB_BLOCK
    KVH = MB_KV_HEADS
    return pl.pallas_call(
        functools.partial(_moba_prep_kernel, per_step=per_step),
        grid=(nb // per_step,),
        in_specs=[pl.BlockSpec((rows, KVH * MB_HD), lambda n: (n, kcol)),
                  pl.BlockSpec((rows, KVH * MB_HD), lambda n: (n, vcol))],
        out_specs=[pl.BlockSpec((KVH, rows, 2 * MB_HD), lambda n: (0, n, 0)),
                   pl.BlockSpec((KVH, rows, 2 * MB_HD), lambda n: (0, n, 0)),
                   pl.BlockSpec((KVH, per_step, MB_HD), lambda n: (0, n, 0)),
                   pl.BlockSpec((rows * KVH, MB_HD), lambda n: (n, 0)),
                   pl.BlockSpec((rows * KVH, MB_HD), lambda n: (n, 0))],
        out_shape=[jax.ShapeDtypeStruct((KVH, L, 2 * MB_HD), BF16),
                   jax.ShapeDtypeStruct((KVH, L, 2 * MB_HD), BF16),
                   jax.ShapeDtypeStruct((KVH, nb, MB_HD), F32),
                   jax.ShapeDtypeStruct((L * KVH, MB_HD), F32),
                   jax.ShapeDtypeStruct((L * KVH, MB_HD), F32)],
        compiler_params=_params("arbitrary"),
        name="moba_prep",
    )(z, z)


def _moba_prompt_kernel(q_ref, ka_ref, va_ref, mean_ref, own_ref, prev_ref, o_ref,
                        qa_ref, m_ref, acc_ref, s_ref):
    i = pl.program_id(1)
    G, B = MB_GROUP, MB_BLOCK
    R = G * B
    scale = MB_HD ** -0.5
    means = mean_ref[0]
    nbp = means.shape[0]
    blk = lax.broadcasted_iota(jnp.int32, (nbp, B), 0)
    no_block = jnp.zeros((LANES - nbp, B), F32)

    m_hi = means.astype(BF16)
    r1 = means - m_hi.astype(F32)
    m_mid = r1.astype(BF16)
    m_lo = (r1 - m_mid.astype(F32)).astype(BF16)
    nt = (((1,), (1,)), ((), ()))

    qbs = [q_ref[:, j * MB_HD:(j + 1) * MB_HD] for j in range(G)]
    gate = jnp.concatenate(
        [lax.dot_general(m_hi, qb, nt, preferred_element_type=F32)
         + lax.dot_general(m_mid, qb, nt, preferred_element_type=F32)
         + lax.dot_general(m_lo, qb, nt, preferred_element_type=F32) for qb in qbs], axis=1)
    blk_all = lax.broadcasted_iota(jnp.int32, (nbp, G * B), 0)
    sel_all = _top_select(jnp.where(blk_all < i, gate, -jnp.inf), blk_all, axis=0)

    for j in range(G):
        sel = sel_all[:, j * B:(j + 1) * B]
        b31 = prev_ref[j, 0:1, 0:1]
        mb = jnp.where(sel, jnp.where(blk == i - 1, 0.0, b31), NEG)
        mb = jnp.where(blk == i, 0.0, mb)
        mb = jnp.concatenate([mb, no_block], axis=0).T
        qa_ref[j * B:(j + 1) * B, :] = jnp.concatenate(
            [(qbs[j].astype(F32) * (scale * LOG2E)).astype(BF16), (mb * LOG2E).astype(BF16)], axis=1)

    def block_rows(n):
        return pl.ds(pl.multiple_of(n * B, B), B)

    def qk(n):
        return lax.dot_general(qa_ref[...], ka_ref[0, block_rows(n), :], (((1,), (1,)), ((), ())),
                               preferred_element_type=F32)

    def softmax_pv(s, n, first):
        vn = va_ref[0, block_rows(n), :]
        mx = jnp.max(s, axis=1, keepdims=True)
        if first:
            m_new = jnp.broadcast_to(mx, (R, LANES))
            p = jnp.exp2(s - mx)
            acc_ref[...] = jnp.dot(p.astype(BF16), vn, preferred_element_type=F32)
        else:
            m_prev = m_ref[...]
            m_new = jnp.maximum(m_prev, mx)
            alpha = jnp.exp2(m_prev - m_new)
            p = jnp.exp2(s - jnp.concatenate([m_new, m_new], axis=1))
            acc_ref[...] = (jnp.concatenate([alpha, alpha], axis=1) * acc_ref[...]
                            + jnp.dot(p.astype(BF16), vn, preferred_element_type=F32))
        m_ref[...] = m_new

    def own_logits():
        return qk(i) + own_ref[...].reshape(R, B) * LOG2E

    def prev_logits():
        return qk(i - 1) + prev_ref[...].reshape(R, B) * LOG2E

    @pl.when(i == 0)
    def _():
        softmax_pv(own_logits(), i, True)

    @pl.when(i == 1)
    def _():
        softmax_pv(own_logits(), i, True)
        softmax_pv(prev_logits(), i - 1, False)

    @pl.when(i >= 2)
    def _():
        s_own = own_logits()
        s_prev = prev_logits()
        softmax_pv(s_own, i, True)
        s_ref[...] = qk(0)
        softmax_pv(s_prev, i - 1, False)
        last = i - 2

        def run(n0, count):
            s_cur = s_ref[...]
            for u in range(count):
                s_next = qk(jnp.minimum(n0 + u + 1, last))
                softmax_pv(s_cur, n0 + u, False)
                s_cur = s_next
            s_ref[...] = s_cur

        def unrolled(t, carry):
            run(MB_UNROLL * t, MB_UNROLL)
            return carry

        total = i - 1
        n_main = total // MB_UNROLL
        lax.fori_loop(0, n_main, unrolled, 0)

        base = n_main * MB_UNROLL
        piece = MB_UNROLL // 2
        while piece >= 1:
            take = (total & piece) != 0

            @pl.when(take)
            def _(base=base, piece=piece):
                run(base, piece)

            base = base + jnp.where(take, piece, 0)
            piece //= 2

    acc = acc_ref[...]
    o = acc[:, :MB_HD] / acc[:, MB_HD:]
    for j in range(G):
        o_ref[:, j * MB_HD:(j + 1) * MB_HD] = o[j * B:(j + 1) * B].astype(o_ref.dtype)


def _moba_prompt(z, L, qcol, ka, vb, means, bias_own, bias_prev):
    nb = L // MB_BLOCK
    G, B = MB_GROUP, MB_BLOCK
    assert nb <= LANES, "block ids ride a 128-lane one-hot"
    assert z.dtype == BF16, "the gate matmul takes q exactly as stored"
    nbp = -(-nb // 8) * 8
    means_p = jnp.pad(means, ((0, 0), (0, nbp - nb), (0, 0)))
    return pl.pallas_call(
        _moba_prompt_kernel,
        grid=(MB_KV_HEADS, nb),
        in_specs=[pl.BlockSpec((B, G * MB_HD), lambda g, i: (i, qcol + g)),
                  pl.BlockSpec((1, L, 2 * MB_HD), lambda g, i: (g, 0, 0), pipeline_mode=pl.Buffered(1)),
                  pl.BlockSpec((1, L, 2 * MB_HD), lambda g, i: (g, 0, 0), pipeline_mode=pl.Buffered(1)),
                  pl.BlockSpec((1, nbp, MB_HD), lambda g, i: (g, 0, 0)),
                  pl.BlockSpec((G, B, B), lambda g, i: (g, 0, 0)),
                  pl.BlockSpec((G, B, B), lambda g, i: (g, 0, 0))],
        out_specs=pl.BlockSpec((B, G * MB_HD), lambda g, i: (i, g)),
        out_shape=jax.ShapeDtypeStruct((L, MB_HEADS * MB_HD), z.dtype),
        scratch_shapes=[pltpu.VMEM((G * B, 2 * MB_HD), BF16),
                        pltpu.VMEM((G * B, LANES), F32),
                        pltpu.VMEM((G * B, 2 * MB_HD), F32),
                        pltpu.VMEM((G * B, B), F32)],
        compiler_params=_params("arbitrary", "arbitrary"),
        name="moba_prompt",
    )(z, ka, vb, means_p, bias_own, bias_prev)


def _moba_sample_kernel(pt_ref, q_ref, kn_ref, vn_ref, ck_hbm, cv_hbm, own_ref, adj_ref, b31_ref, o_ref,
                        kbuf, vbuf, lg_ref, sem, *, T, n_pages, n_seq):
    b = pl.program_id(0)
    B = MB_BLOCK
    past = n_pages * PAGE_SIZE
    nb = past // B
    R = MB_HEADS * T
    half = R // MB_KV_HEADS
    scale = MB_HD ** -0.5

    KVH = MB_KV_HEADS
    prow = PAGE_SIZE * KVH

    def page_copies(seq, slot, j):
        dst = pl.ds(pl.multiple_of(j * prow, prow), prow)
        page = pt_ref[seq, j]
        return (pltpu.make_async_copy(ck_hbm.at[page], kbuf.at[slot, dst, :], sem.at[0, slot]),
                pltpu.make_async_copy(cv_hbm.at[page], vbuf.at[slot, dst, :], sem.at[1, slot]))

    def start_seq(seq, slot):
        def body(j, c):
            ck, cv = page_copies(seq, slot, j)
            ck.start()
            cv.start()
            return c
        lax.fori_loop(0, n_pages, body, 0)

    def wait_seq(seq, slot):
        def body(j, c):
            ck, cv = page_copies(seq, slot, j)
            ck.wait()
            cv.wait()
            return c
        lax.fori_loop(0, n_pages, body, 0)

    slot = b % 2

    @pl.when(b == 0)
    def _():
        start_seq(0, 0)

    @pl.when(b + 1 < n_seq)
    def _():
        start_seq(b + 1, 1 - slot)

    wait_seq(b, slot)

    def block(buf, n, g):
        return buf[slot, pl.ds(n * B * KVH + g, B, stride=KVH), :]

    q = q_ref[...]
    Q = jnp.concatenate([q[:, h * MB_HD:(h + 1) * MB_HD] for h in range(MB_HEADS)], axis=0)
    qf = Q * scale
    qs = qf.astype(BF16)
    grow = [slice(g * half, (g + 1) * half) for g in range(KVH)]

    mean_rows = [[] for _ in range(KVH)]
    for n in range(nb):
        for g in range(KVH):
            kf = block(kbuf, n, g)
            mean_rows[g].append(jnp.mean(kf, axis=0, keepdims=True))
            lg_ref[grow[g], n * B:(n + 1) * B] = lax.dot_general(
                qs[grow[g]], kf.astype(BF16), (((1,), (1,)), ((), ())), preferred_element_type=F32)

    gates = []
    for g in range(KVH):
        means = jnp.concatenate(mean_rows[g] + [jnp.zeros((LANES - nb, MB_HD), F32)], axis=0)
        gates.append(lax.dot_general(Q[grow[g]], means, (((1,), (1,)), ((), ())),
                                     precision=lax.Precision.HIGHEST, preferred_element_type=F32))
    gate = jnp.concatenate(gates, axis=0)
    lane = lax.broadcasted_iota(jnp.int32, (R, LANES), 1)
    sel = _top_select(jnp.where(lane < nb, gate, -jnp.inf), lane)
    mb = jnp.where(sel, jnp.where(lane == nb - 1, 0.0, b31_ref[...]), NEG)
    adj = adj_ref[...]

    def logits(n):
        s = lg_ref[:, n * B:(n + 1) * B] + mb[:, n:n + 1]
        return s + adj if n == nb - 1 else s

    kn = kn_ref[...]
    vn = vn_ref[...]
    s_own = jnp.concatenate(
        [lax.dot_general(qf[grow[g]], kn[:, g * MB_HD:(g + 1) * MB_HD], (((1,), (1,)), ((), ())),
                         preferred_element_type=F32) for g in range(KVH)], axis=0) + own_ref[...]

    m = jnp.max(s_own, axis=1, keepdims=True)
    for n in range(nb):
        m = jnp.maximum(m, jnp.max(logits(n), axis=1, keepdims=True))

    p_own = jnp.exp(s_own - m)
    l = jnp.sum(p_own, axis=1, keepdims=True)
    acc = [jnp.dot(p_own[grow[g]], vn[:, g * MB_HD:(g + 1) * MB_HD], preferred_element_type=F32)
           for g in range(KVH)]
    for n in range(nb):
        p = jnp.exp(logits(n) - m)
        l = l + jnp.sum(p, axis=1, keepdims=True)
        pb = p.astype(BF16)
        for g in range(KVH):
            acc[g] = acc[g] + jnp.dot(pb[grow[g]], block(vbuf, n, g).astype(BF16),
                                      preferred_element_type=F32)
    rows = jnp.concatenate(acc, axis=0) / l
    o_ref[...] = jnp.concatenate([rows[h * T:(h + 1) * T] for h in range(MB_HEADS)], axis=1)


def _moba_sample(z, zf, page_table, cache_k, cache_v, bias_own, bias_prev, DB, T, qcol, kcol, vcol):
    n_pages = page_table.shape[1]
    past = n_pages * PAGE_SIZE
    R = MB_HEADS * T
    W = MB_HEADS * MB_HD
    KW = MB_KV_HEADS * MB_HD
    ck = cache_k.reshape(cache_k.shape[0], PAGE_SIZE * MB_KV_HEADS, MB_HD)
    cv = cache_v.reshape(cache_v.shape[0], PAGE_SIZE * MB_KV_HEADS, MB_HD)
    own = bias_own[:, :T, :T].reshape(R, T)
    adj = bias_prev[:, :T, :].reshape(R, MB_BLOCK)
    b31 = bias_prev[:, :T, 0:1].reshape(R, 1)
    const = lambda b, pt: (0, 0)
    grid_spec = pltpu.PrefetchScalarGridSpec(
        num_scalar_prefetch=1,
        grid=(DB,),
        in_specs=[pl.BlockSpec((T, W), lambda b, pt: (b, qcol)),
                  pl.BlockSpec((T, KW), lambda b, pt: (b, kcol)),
                  pl.BlockSpec((T, KW), lambda b, pt: (b, vcol)),
                  pl.BlockSpec(memory_space=pl.ANY),
                  pl.BlockSpec(memory_space=pl.ANY),
                  pl.BlockSpec((R, T), const),
                  pl.BlockSpec((R, MB_BLOCK), const),
                  pl.BlockSpec((R, 1), const)],
        out_specs=pl.BlockSpec((T, W), lambda b, pt: (b, 0)),
        scratch_shapes=[pltpu.VMEM((2, past * MB_KV_HEADS, MB_HD), F32),
                        pltpu.VMEM((2, past * MB_KV_HEADS, MB_HD), F32),
                        pltpu.VMEM((R, past), F32),
                        pltpu.SemaphoreType.DMA((2, 2))],
    )
    return pl.pallas_call(
        functools.partial(_moba_sample_kernel, T=T, n_pages=n_pages, n_seq=DB),
        grid_spec=grid_spec,
        out_shape=jax.ShapeDtypeStruct((DB * T, W), F32),
        compiler_params=_params("arbitrary"),
        name="moba_sample",
    )(page_table, z, zf, zf, ck, cv, own, adj, b31)


def _xblock_math(x, g, wq_ref, k_ref, v_ref, wo_ref):
    q = jnp.dot(_rms(x, g).astype(BF16), wq_ref[...], preferred_element_type=F32)
    hd = k_ref.shape[2]
    scale = hd ** -0.5
    outs = []
    for h in range(MX_HEADS):
        qh = (q[:, h * hd:(h + 1) * hd] * scale).astype(BF16)
        s = lax.dot_general(qh, k_ref[h], (((1,), (1,)), ((), ())), preferred_element_type=F32)
        p = jnp.exp(s - jnp.max(s, axis=1, keepdims=True))
        o = jnp.dot(p.astype(BF16), v_ref[h], preferred_element_type=F32)
        outs.append((o / jnp.sum(p, axis=1, keepdims=True)).astype(BF16))
    return x + jnp.dot(jnp.concatenate(outs, axis=1), wo_ref[...], preferred_element_type=F32)


def _merge_kernel(oa_ref, ob_ref, ga_ref, gb_ref, x_ref, wpa_ref, wpb_ref, wo_ref, *rest):
    o_ref = rest[-1]
    a = jnp.dot(oa_ref[...].astype(BF16), wpa_ref[...], preferred_element_type=F32)
    b = jnp.dot(ob_ref[...].astype(BF16), wpb_ref[...], preferred_element_type=F32)
    merged = _sigmoid(ga_ref[...].astype(F32)) * a + _sigmoid(gb_ref[...].astype(F32)) * b
    h = x_ref[...] + jnp.dot(merged.astype(BF16), wo_ref[...], preferred_element_type=F32)
    if len(rest) > 1:
        gx_ref, wq_ref, k_ref, v_ref, wxo_ref = rest[:-1]
        h = _xblock_math(h, gx_ref[...], wq_ref, k_ref, v_ref, wxo_ref)
    o_ref[...] = h


def _merge(oa, ob, z, gcol, x, wpa, wpb, wo, xblock=None, tm=512):
    M, D = x.shape
    tm = min(tm, M)
    row = lambda i: (i, 0)
    const = lambda i: (0, 0)
    once = pl.Buffered(1)
    wspec = pl.BlockSpec((D, D), const, pipeline_mode=once)
    args = [oa, ob, z, z, x, wpa, wpb, wo]
    specs = [pl.BlockSpec((tm, D), row), pl.BlockSpec((tm, D), row),
             pl.BlockSpec((tm, D), lambda i: (i, gcol)), pl.BlockSpec((tm, D), lambda i: (i, gcol + 1)),
             pl.BlockSpec((tm, D), row), wspec, wspec, wspec]
    if xblock is not None:
        gx, wq, mk, mv, wxo = xblock
        args += [gx.reshape(1, D), wq, mk, mv, wxo]
        specs += [pl.BlockSpec((1, D), const), wspec,
                  pl.BlockSpec(mk.shape, lambda i: (0, 0, 0)), pl.BlockSpec(mv.shape, lambda i: (0, 0, 0)), wspec]
    return pl.pallas_call(
        _merge_kernel,
        grid=(M // tm,),
        in_specs=specs,
        out_specs=pl.BlockSpec((tm, D), row),
        out_shape=jax.ShapeDtypeStruct((M, D), F32),
        compiler_params=_params("parallel"),
        name="merge",
    )(*args)


def _xattn_kernel(q_ref, mk_hbm, mv_hbm, o_ref, kbuf, vbuf, sem, *, n_steps, per_step, L):
    b = pl.program_id(0)
    hd = q_ref.shape[1] // MX_HEADS
    scale = hd ** -0.5
    slot = b % 2

    def head_copies(step, slot):
        cps = []
        for j in range(per_step):
            seq = step * per_step + j
            for h in range(MX_HEADS):
                cps.append(pltpu.make_async_copy(mk_hbm.at[seq, :, h, :], kbuf.at[slot, j, h], sem.at[0, slot]))
                cps.append(pltpu.make_async_copy(mv_hbm.at[seq, :, h, :], vbuf.at[slot, j, h], sem.at[1, slot]))
        return cps

    @pl.when(b == 0)
    def _():
        for cp in head_copies(0, 0):
            cp.start()

    @pl.when(b + 1 < n_steps)
    def _():
        for cp in head_copies(b + 1, 1 - slot):
            cp.start()

    for cp in head_copies(b, slot):
        cp.wait()

    q = (q_ref[...] * scale).astype(BF16)
    pairs = [(j, h) for j in range(per_step) for h in range(MX_HEADS)]
    s = [lax.dot_general(q[j * L:(j + 1) * L, h * hd:(h + 1) * hd], kbuf[slot, j, h].astype(BF16),
                         (((1,), (1,)), ((), ())), preferred_element_type=F32) for j, h in pairs]
    p = [jnp.exp(x - jnp.max(x, axis=1, keepdims=True)) for x in s]
    o = [jnp.dot(p[c].astype(BF16), vbuf[slot, j, h].astype(BF16), preferred_element_type=F32)
         / jnp.sum(p[c], axis=1, keepdims=True) for c, (j, h) in enumerate(pairs)]
    o_ref[...] = jnp.concatenate(
        [jnp.concatenate(o[j * MX_HEADS:(j + 1) * MX_HEADS], axis=1) for j in range(per_step)], axis=0)


def _xattn(q, mk, mv, B, L, per_step=4):
    D = q.shape[1]
    mem, hd = mk.shape[1], mk.shape[3]
    per_step = math.gcd(per_step, B)
    n_steps = B // per_step
    rows = per_step * L
    return pl.pallas_call(
        functools.partial(_xattn_kernel, n_steps=n_steps, per_step=per_step, L=L),
        grid=(n_steps,),
        in_specs=[pl.BlockSpec((rows, D), lambda b: (b, 0)),
                  pl.BlockSpec(memory_space=pl.ANY),
                  pl.BlockSpec(memory_space=pl.ANY)],
        out_specs=pl.BlockSpec((rows, D), lambda b: (b, 0)),
        out_shape=jax.ShapeDtypeStruct((B * L, D), F32),
        scratch_shapes=[pltpu.VMEM((2, per_step, MX_HEADS, mem, hd), F32),
                        pltpu.VMEM((2, per_step, MX_HEADS, mem, hd), F32),
                        pltpu.SemaphoreType.DMA((2, 2))],
        compiler_params=_params("arbitrary"),
        name="xattn",
    )(q, mk, mv)


def _ffn_kernel(x_ref, g_ref, wg_ref, wu_ref, wd_ref, fg_ref, o_ref, *, tf):
    x = x_ref[...]
    xn = _rms(x, g_ref[...]).astype(BF16)
    acc = x
    for c in range(wg_ref.shape[1] // tf):
        sl = slice(c * tf, (c + 1) * tf)
        a = jnp.dot(xn, wg_ref[:, sl], preferred_element_type=F32)
        u = jnp.dot(xn, wu_ref[:, sl], preferred_element_type=F32)
        acc = acc + jnp.dot((_silu(a) * u).astype(BF16), wd_ref[sl, :], preferred_element_type=F32)
    o_ref[...] = _rms(acc, fg_ref[...])


def _ffn(x, g, wg, wu, wd, fg, tm=1024, tf=256):
    M, D = x.shape
    FF = wg.shape[1]
    tm = min(tm, M)
    row = lambda i: (i, 0)
    const = lambda i: (0, 0)
    once = pl.Buffered(1)
    return pl.pallas_call(
        functools.partial(_ffn_kernel, tf=tf),
        grid=(M // tm,),
        in_specs=[pl.BlockSpec((tm, D), row), pl.BlockSpec((1, D), const),
                  pl.BlockSpec((D, FF), const, pipeline_mode=once),
                  pl.BlockSpec((D, FF), const, pipeline_mode=once),
                  pl.BlockSpec((FF, D), const, pipeline_mode=once),
                  pl.BlockSpec((1, D), const)],
        out_specs=pl.BlockSpec((tm, D), row),
        out_shape=jax.ShapeDtypeStruct((M, D), F32),
        compiler_params=_params("parallel"),
        name="ffn",
    )(x, g.reshape(1, D), wg, wu, wd, fg.reshape(1, D))


def _mixer(x2, B, L, chunk, s0, moba_fn, W, za_dtype, xblock=None):
    tm = 512 if za_dtype == BF16 else 256
    za, zf = _proj_in(x2, W["norm_mix_g"], W["w_in"], W["a_cols"], za_dtype, tm=tm)
    oa, S = _gla(za, zf, W["lb_logits"], W["hg_norm_g"], s0, B, L, chunk)
    ob = moba_fn(za, zf)
    h = _merge(oa, ob, za, _ZA_GA, x2, W["w_pa"], W["w_pb"], W["w_out"], xblock=xblock)
    return h, S, zf


def kernel(x_prompt, x_sample, cache_k, cache_v, cache_mem_k, cache_mem_v, state_hgrn, page_table, mem_prompt, norm_mix_g, w_in, hg_lb_logits, hg_norm_g, w_proj_a, w_proj_b, w_out, rel_bias, norm_x_g, w_xq, w_mk, w_mv, w_xo, norm_ffn_g, w_gate, w_up, w_down, final_norm_g):
    Bp, Lp, D = x_prompt.shape
    DB, T, _ = x_sample.shape
    l = 0
    win = w_in[l]
    hw = HG_HEADS * HG_DK
    hq, hf, hi, hg = (win[:, j * hw:(j + 1) * hw] for j in range(4))
    c0 = 4 * hw + MB_HEADS * MB_HD
    c1 = c0 + 2 * MB_KV_HEADS * MB_HD
    W = {
        "w_in": jnp.concatenate([hq, hi, hg, win[:, 4 * hw:c0], win[:, c1:], hf, win[:, c0:c1]],
                                axis=1).astype(BF16),
        "a_cols": win.shape[1] - hw - (c1 - c0),
        "norm_mix_g": norm_mix_g[l],
        "lb_logits": hg_lb_logits,
        "hg_norm_g": hg_norm_g[l],
        "w_pa": w_proj_a[l].astype(BF16), "w_pb": w_proj_b[l].astype(BF16), "w_out": w_out[l].astype(BF16),
    }
    wxq, wxo = w_xq[l].astype(BF16), w_xo[l].astype(BF16)
    wmkv = jnp.concatenate([w_mk[l], w_mv[l]], axis=1).astype(BF16)
    wg, wu, wd = w_gate[l].astype(BF16), w_up[l].astype(BF16), w_down[l].astype(BF16)

    bias_own, bias_prev = _bias_tiles(rel_bias)

    xp = x_prompt.reshape(Bp * Lp, D)

    assert Bp == 1, "the prompt kernels take one sequence"
    kw = MB_KV_HEADS * MB_HD

    def new_kv(zf, B, L):
        return (zf[:, _ZF_K * kw:(_ZF_K + 1) * kw].reshape(B, L, MB_KV_HEADS, MB_HD),
                zf[:, _ZF_V * kw:(_ZF_V + 1) * kw].reshape(B, L, MB_KV_HEADS, MB_HD))

    prompt_kv = []

    def moba_p(za, zf):
        ka, va, means, ko, vo = _moba_prep(zf, Lp, _ZF_K, _ZF_V)
        prompt_kv.extend([ko, vo])
        return _moba_prompt(za, Lp, 2 * _ZA_MQ, ka, va, means, bias_own, bias_prev)

    mem = mem_prompt.reshape(Bp * mem_prompt.shape[1], D)
    mkv = _matmul(mem, wmkv, tm=256, tn=512)
    mshape = (Bp, mem_prompt.shape[1], MX_HEADS, D // MX_HEADS)
    mkp, mvp = mkv[:, :D].reshape(mshape), mkv[:, D:].reshape(mshape)
    xblock = (norm_x_g[l], wxq, mkp[0].swapaxes(0, 1).astype(BF16), mvp[0].swapaxes(0, 1).astype(BF16), wxo)

    s0 = jnp.zeros((Bp, HG_HEADS, HG_DK, HG_DV), F32)
    hp, Sp, zfp = _mixer(xp, Bp, Lp, math.gcd(Lp, 64), s0, moba_p, W, BF16, xblock=xblock)
    kp, vp = (a.reshape(Bp, Lp, MB_KV_HEADS, MB_HD) for a in prompt_kv)
    yp = _ffn(hp, norm_ffn_g[l], wg, wu, wd, final_norm_g)

    xs = x_sample.reshape(DB * T, D)

    def moba_s(za, zf):
        return _moba_sample(za, zf, page_table, cache_k[l], cache_v[l], bias_own, bias_prev, DB, T,
                            _ZA_MQ, _ZF_K, _ZF_V)

    hs, Ss, zfs = _mixer(xs, DB, T, T, state_hgrn[l], moba_s, W, F32)
    ksn, vsn = new_kv(zfs, DB, T)
    qxs = _matmul(hs, wxq, g=norm_x_g[l], tn=512)
    oxs = _xattn(qxs, cache_mem_k[l], cache_mem_v[l], DB, T)
    hs = _matmul(oxs, wxo, res=hs, tn=512)
    ys = _ffn(hs, norm_ffn_g[l], wg, wu, wd, final_norm_g)

    return (yp.reshape(Bp, Lp, D), ys.reshape(DB, T, D),
            Sp[None].astype(state_hgrn.dtype), kp[None], vp[None],
            mkp[None], mvp[None],
            Ss[None].astype(state_hgrn.dtype), ksn[None], vsn[None])
```

```python
import functools
import math

import jax
import jax.numpy as jnp
import numpy as np
from jax import lax
from jax.experimental import pallas as pl
from jax.experimental.pallas import tpu as pltpu

F32 = jnp.float32
BF16 = jnp.bfloat16

EPS = 1e-6
HG_HEADS = 8
HG_DK = 128
HG_DV = 128
HG_SUB = 16
MB_HEADS = 8
MB_KV_HEADS = 2
MB_GROUP = MB_HEADS // MB_KV_HEADS
MB_HD = 128
MB_BLOCK = 256
MB_TOPK = 3
MB_UNROLL = 16
RB_BUCKETS = 32
RB_MAX_DIST = 128
MX_HEADS = 4
PAGE_SIZE = 128
NEG = -1e30
LOG2E = math.log2(math.e)

_ZA_HQ, _ZA_HI, _ZA_HG, _ZA_MQ, _ZA_GA = 0, 1, 2, 3, 4
_ZF_HF = 0
_ZF_K, _ZF_V = 4, 5
LANES = 128
SUBLANES = 8
VMEM_LIMIT = 56 * 1024 * 1024


def _params(*sem):
    return pltpu.CompilerParams(dimension_semantics=sem, vmem_limit_bytes=VMEM_LIMIT)


def _sigmoid(x):
    return 1.0 / (1.0 + jnp.exp(-x))


def _silu(x):
    return x * _sigmoid(x)


def _rms(x, g):
    return x * lax.rsqrt(jnp.mean(x * x, axis=-1, keepdims=True) + EPS) * g


def _matmul_kernel(*refs, norm, residual):
    refs = list(refs)
    x_ref = refs.pop(0)
    g_ref = refs.pop(0) if norm else None
    w_ref = refs.pop(0)
    r_ref = refs.pop(0) if residual else None
    o_ref, xs_ref = refs

    @pl.when(pl.program_id(1) == 0)
    def _():
        x = x_ref[...]
        if norm:
            x = _rms(x, g_ref[...])
        xs_ref[...] = x.astype(BF16)

    acc = jnp.dot(xs_ref[...], w_ref[...], preferred_element_type=F32)
    if residual:
        acc = acc + r_ref[...]
    o_ref[...] = acc.astype(o_ref.dtype)


def _matmul(x, w, g=None, res=None, tm=512, tn=512, out_dtype=F32):
    M, K = x.shape
    N = w.shape[1]
    tm = min(tm, M)
    tn = min(tn, N)
    assert M % tm == 0 and N % tn == 0
    args = [x]
    specs = [pl.BlockSpec((tm, K), lambda i, j: (i, 0))]
    if g is not None:
        args.append(g.reshape(1, K))
        specs.append(pl.BlockSpec((1, K), lambda i, j: (0, 0)))
    args.append(w)
    specs.append(pl.BlockSpec((K, tn), lambda i, j: (0, j)))
    if res is not None:
        args.append(res)
        specs.append(pl.BlockSpec((tm, tn), lambda i, j: (i, j)))
    return pl.pallas_call(
        functools.partial(_matmul_kernel, norm=g is not None, residual=res is not None),
        grid=(M // tm, N // tn),
        in_specs=specs,
        out_specs=pl.BlockSpec((tm, tn), lambda i, j: (i, j)),
        out_shape=jax.ShapeDtypeStruct((M, N), out_dtype),
        scratch_shapes=[pltpu.VMEM((tm, K), BF16)],
        compiler_params=_params("parallel", "arbitrary"),
        name="matmul",
    )(*args)


def _proj_in_kernel(x_ref, g_ref, w_ref, oa_ref, of_ref, *, tn):
    xs = _rms(x_ref[...], g_ref[...]).astype(BF16)
    a_cols = oa_ref.shape[1]
    for j in range(w_ref.shape[1] // tn):
        acc = jnp.dot(xs, w_ref[:, j * tn:(j + 1) * tn], preferred_element_type=F32)
        if j * tn < a_cols:
            oa_ref[:, j * tn:(j + 1) * tn] = acc.astype(oa_ref.dtype)
        else:
            of_ref[:, j * tn - a_cols:(j + 1) * tn - a_cols] = acc


def _proj_in(x, g, w, a_cols, a_dtype, tm=512, tn=512):
    M, K = x.shape
    N = w.shape[1]
    tm = min(tm, M)
    assert M % tm == 0 and a_cols % tn == 0 and N % tn == 0
    return pl.pallas_call(
        functools.partial(_proj_in_kernel, tn=tn),
        grid=(M // tm,),
        in_specs=[pl.BlockSpec((tm, K), lambda i: (i, 0)),
                  pl.BlockSpec((1, K), lambda i: (0, 0)),
                  pl.BlockSpec((K, N), lambda i: (0, 0), pipeline_mode=pl.Buffered(1))],
        out_specs=[pl.BlockSpec((tm, a_cols), lambda i: (i, 0)),
                   pl.BlockSpec((tm, N - a_cols), lambda i: (i, 0))],
        out_shape=[jax.ShapeDtypeStruct((M, a_cols), a_dtype),
                   jax.ShapeDtypeStruct((M, N - a_cols), F32)],
        compiler_params=_params("parallel"),
        name="proj_in",
    )(x, g.reshape(1, K), w)


def _subblock_cumsum(g, sub):
    row = lax.broadcasted_iota(jnp.int32, g.shape, 0) % sub
    s = 1
    while s < sub:
        g = g + jnp.where(row >= s, pltpu.roll(g, s, 0), 0.0)
        s *= 2
    return g


def _gla_kernel(q_ref, f_ref, i_ref, g_ref, lbl_ref, ng_ref, s0_ref, o_ref, so_ref, st_ref, *,
                C, sub, per_step, seqs):
    c = pl.program_id(1)
    nsub = C // sub
    mid = sub // 2 - 1

    @pl.when(c == 0)
    def _():
        st_ref[...] = s0_ref[...]

    lbl = lbl_ref[...]
    e = jnp.exp(lbl - jnp.max(lbl, axis=0, keepdims=True))
    lb_all = e[0:1] / jnp.sum(e, axis=0, keepdims=True)
    ng = ng_ref[...]

    row = lax.broadcasted_iota(jnp.int32, (C, C), 0)
    col = lax.broadcasted_iota(jnp.int32, (C, C), 1)
    diag_mask = (row // sub == col // sub) & (row >= col)

    W = HG_HEADS * HG_DK
    sls = [slice(h * HG_DK, (h + 1) * HG_DK) for h in range(HG_HEADS)]
    nt = (((1,), (1,)), ((), ()))

    def elementwise(rows):
        q3 = q_ref[rows, :].astype(F32).reshape(nsub, sub, W)
        f = lb_all + (1.0 - lb_all) * _sigmoid(f_ref[rows, :])
        k3 = (1.0 - f).reshape(nsub, sub, W)
        cs = _subblock_cumsum(jnp.log(f), sub).reshape(nsub, sub, W)

        T = cs[:, sub - 1:sub, :]
        m = cs[:, mid:mid + 1, :]
        qd = q3 * jnp.exp(cs - m)
        kd = k3 * jnp.exp(m - cs)
        qt = qd * jnp.exp(m)
        kh = kd * jnp.exp(T - m)

        PT = [jnp.zeros((1, W), F32)]
        for I in range(nsub):
            PT.append(PT[-1] + T[I])

        def scaled_k(I):
            parts = []
            for J in range(nsub):
                if J < I:
                    parts.append(kh[J] * jnp.exp(PT[I] - PT[J + 1]))
                else:
                    parts.append(jnp.zeros((sub, W), F32))
            return jnp.concatenate(parts, axis=0).astype(BF16) if nsub > 1 else parts[0].astype(BF16)

        return dict(
            qd=qd.reshape(C, W).astype(BF16), kd=kd.reshape(C, W).astype(BF16),
            qt=[qt[I].astype(BF16) for I in range(nsub)],
            ks=[scaled_k(I) for I in range(1, nsub + 1)],
            qc=jnp.concatenate([qt[I] * jnp.exp(PT[I]) for I in range(nsub)], axis=0).astype(BF16),
            decay=jnp.exp(PT[nsub]),
            v=i_ref[rows, :].astype(BF16),
            gate=_silu(g_ref[rows, :].astype(F32)))

    def intra(e):
        atts = []
        for sl in sls:
            att = jnp.where(diag_mask,
                            lax.dot_general(e["qd"][:, sl], e["kd"][:, sl], nt, preferred_element_type=F32), 0.0)
            if nsub > 1:
                blocks = [jnp.zeros((sub, C), F32)]
                for I in range(1, nsub):
                    blocks.append(lax.dot_general(e["qt"][I][:, sl], e["ks"][I - 1][:, sl], nt,
                                                  preferred_element_type=F32))
                att = att + jnp.concatenate(blocks, axis=0)
            atts.append(att.astype(BF16))
        return atts

    def recur(e, atts, rows, seq):
        sts = [st_ref[seq, h] for h in range(HG_HEADS)]
        outs = [jnp.dot(atts[h], e["v"][:, sl], preferred_element_type=F32)
                + jnp.dot(e["qc"][:, sl], sts[h].astype(BF16), preferred_element_type=F32)
                for h, sl in enumerate(sls)]
        for h, sl in enumerate(sls):
            dcol = jnp.transpose(jnp.broadcast_to(e["decay"][:, sl], (SUBLANES, HG_DK)))[:, 0:1]
            st_ref[seq, h] = (sts[h] * dcol
                              + lax.dot_general(e["ks"][-1][:, sl], e["v"][:, sl], (((0,), (0,)), ((), ())),
                                                preferred_element_type=F32))
        for h, sl in enumerate(sls):
            o = outs[h]
            o = o * lax.rsqrt(jnp.mean(o * o, axis=-1, keepdims=True) + EPS) * ng * e["gate"][:, sl]
            o_ref[rows, sl] = o.astype(o_ref.dtype)

    units = [(slice((s * per_step + u) * C, (s * per_step + u + 1) * C), s)
             for s in range(seqs) for u in range(per_step)]
    parts = [elementwise(rows) for rows, _ in units]
    atts = [intra(e) for e in parts]
    for e, a, (rows, seq) in zip(parts, atts, units):
        recur(e, a, rows, seq)

    @pl.when(c == pl.num_programs(1) - 1)
    def _():
        so_ref[...] = st_ref[...]


def _gla(za, zf, lb_logits, norm_g, s0, B, L, C):
    per_step = math.gcd(L // C, 4)
    seqs = math.gcd(B, 8) if C == L else 1
    R = seqs * per_step * C
    nc = L // (per_step * C)
    sub = min(HG_SUB, C)
    W = HG_HEADS * HG_DK

    def zspec(j):
        return pl.BlockSpec((R, W), lambda b, c, j=j: (b * nc + c, j))

    sspec = pl.BlockSpec((seqs, HG_HEADS, HG_DK, HG_DV), lambda b, c: (b, 0, 0, 0))
    return pl.pallas_call(
        functools.partial(_gla_kernel, C=C, sub=sub, per_step=per_step, seqs=seqs),
        grid=(B // seqs, nc),
        in_specs=[zspec(_ZA_HQ), zspec(_ZF_HF), zspec(_ZA_HI), zspec(_ZA_HG),
                  pl.BlockSpec((2, W), lambda b, c: (0, 0)),
                  pl.BlockSpec((1, HG_DV), lambda b, c: (0, 0)),
                  sspec],
        out_specs=[pl.BlockSpec((R, W), lambda b, c: (b * nc + c, 0)), sspec],
        out_shape=[jax.ShapeDtypeStruct((B * L, W), za.dtype),
                   jax.ShapeDtypeStruct((B, HG_HEADS, HG_DK, HG_DV), F32)],
        scratch_shapes=[pltpu.VMEM((seqs, HG_HEADS, HG_DK, HG_DV), F32)],
        compiler_params=_params("arbitrary", "arbitrary"),
        name="gla",
    )(za, zf, za, za, lb_logits, norm_g.reshape(1, HG_DV), s0)


def _bucket_upper_bounds():
    max_exact = RB_BUCKETS // 2
    d = np.arange(1, 4 * RB_MAX_DIST)
    large = max_exact + (np.log(d.astype(np.float32) / max_exact) / math.log(RB_MAX_DIST / max_exact)
                         * (RB_BUCKETS - max_exact)).astype(np.int32)
    bucket = np.where(d < max_exact, d, np.minimum(large, RB_BUCKETS - 1))
    bucket = np.concatenate([[0], bucket])
    return [int(np.max(np.nonzero(bucket == b)[0])) for b in range(RB_BUCKETS - 1)]


def _bias_kernel(rb_ref, own_ref, prev_ref):
    h = pl.program_id(0)
    i = lax.broadcasted_iota(jnp.int32, (MB_BLOCK, MB_BLOCK), 0)
    j = lax.broadcasted_iota(jnp.int32, (MB_BLOCK, MB_BLOCK), 1)
    dmax = _bucket_upper_bounds()

    def table(d):
        val = jnp.full(d.shape, rb_ref[RB_BUCKETS - 1, h], F32)
        for b in range(RB_BUCKETS - 2, -1, -1):
            val = jnp.where(d <= dmax[b], rb_ref[b, h], val)
        return val

    d = i - j
    own_ref[0] = jnp.where(d >= 0, table(d), NEG)
    prev_ref[0] = table(d + MB_BLOCK)


def _bias_tiles(rel_bias):
    shp = jax.ShapeDtypeStruct((MB_HEADS, MB_BLOCK, MB_BLOCK), F32)
    spec = pl.BlockSpec((1, MB_BLOCK, MB_BLOCK), lambda h: (h, 0, 0))
    return pl.pallas_call(
        _bias_kernel,
        grid=(MB_HEADS,),
        in_specs=[pl.BlockSpec(memory_space=pltpu.SMEM)],
        out_specs=[spec, spec],
        out_shape=[shp, shp],
        compiler_params=_params("arbitrary"),
        name="bias_tiles",
    )(rel_bias)


def _top_select(gate, lane, axis=1):
    sel = jnp.zeros(gate.shape, F32)
    lane = lane.astype(F32)
    for _ in range(MB_TOPK):
        m = jnp.max(gate, axis=axis, keepdims=True)
        idx = jnp.min(jnp.where(gate == m, lane, float(2 ** 30)), axis=axis, keepdims=True)
        hit = lane == idx
        found = jnp.where(m > -jnp.inf, 1.0, 0.0)
        sel = jnp.maximum(sel, jnp.where(hit, found, 0.0))
        gate = jnp.where(hit, -jnp.inf, gate)
    return sel > 0.0


def _moba_prep_kernel(k_ref, v_ref, ka_ref, vb_ref, mean_ref, ko_ref, vo_ref, *, per_step):
    rows = per_step * MB_BLOCK
    block = (lax.broadcasted_iota(jnp.int32, (rows, LANES), 0) // MB_BLOCK
             + pl.program_id(0) * per_step)
    lane = lax.broadcasted_iota(jnp.int32, (rows, LANES), 1)
    onehot = jnp.where(lane == block, 1.0, 0.0).astype(BF16)
    ones = jnp.ones((rows, LANES), BF16)
    for g in range(MB_KV_HEADS):
        k = k_ref[:, g * MB_HD:(g + 1) * MB_HD]
        v = v_ref[:, g * MB_HD:(g + 1) * MB_HD]
        ka_ref[g] = jnp.concatenate([k.astype(BF16), onehot], axis=1)
        vb_ref[g] = jnp.concatenate([v.astype(BF16), ones], axis=1)
        mean_ref[g] = jnp.mean(k.reshape(per_step, MB_BLOCK, MB_HD), axis=1)
        ko_ref[pl.ds(g, rows, stride=MB_KV_HEADS), :] = k
        vo_ref[pl.ds(g, rows, stride=MB_KV_HEADS), :] = v


def _moba_prep(z, L, kcol, vcol):
    nb = L // MB_BLOCK
    per_step = math.gcd(nb, SUBLANES)
    rows = per_step * MB_BLOCK
    KVH = MB_KV_HEADS
    return pl.pallas_call(
        functools.partial(_moba_prep_kernel, per_step=per_step),
        grid=(nb // per_step,),
        in_specs=[pl.BlockSpec((rows, KVH * MB_HD), lambda n: (n, kcol)),
                  pl.BlockSpec((rows, KVH * MB_HD), lambda n: (n, vcol))],
        out_specs=[pl.BlockSpec((KVH, rows, 2 * MB_HD), lambda n: (0, n, 0)),
                   pl.BlockSpec((KVH, rows, 2 * MB_HD), lambda n: (0, n, 0)),
                   pl.BlockSpec((KVH, per_step, MB_HD), lambda n: (0, n, 0)),
                   pl.BlockSpec((rows * KVH, MB_HD), lambda n: (n, 0)),
                   pl.BlockSpec((rows * KVH, MB_HD), lambda n: (n, 0))],
        out_shape=[jax.ShapeDtypeStruct((KVH, L, 2 * MB_HD), BF16),
                   jax.ShapeDtypeStruct((KVH, L, 2 * MB_HD), BF16),
                   jax.ShapeDtypeStruct((KVH, nb, MB_HD), F32),
                   jax.ShapeDtypeStruct((L * KVH, MB_HD), F32),
                   jax.ShapeDtypeStruct((L * KVH, MB_HD), F32)],
        compiler_params=_params("arbitrary"),
        name="moba_prep",
    )(z, z)


def _moba_prompt_kernel(q_ref, ka_ref, va_ref, mean_ref, own_ref, prev_ref, o_ref,
                        qa_ref, m_ref, acc_ref, s_ref):
    i = pl.program_id(1)
    G, B = MB_GROUP, MB_BLOCK
    R = G * B
    scale = MB_HD ** -0.5
    means = mean_ref[0]
    nbp = means.shape[0]
    blk = lax.broadcasted_iota(jnp.int32, (nbp, B), 0)
    no_block = jnp.zeros((LANES - nbp, B), F32)

    m_hi = means.astype(BF16)
    r1 = means - m_hi.astype(F32)
    m_mid = r1.astype(BF16)
    m_lo = (r1 - m_mid.astype(F32)).astype(BF16)
    nt = (((1,), (1,)), ((), ()))

    qbs = [q_ref[:, j * MB_HD:(j + 1) * MB_HD] for j in range(G)]
    gate = jnp.concatenate(
        [lax.dot_general(m_hi, qb, nt, preferred_element_type=F32)
         + lax.dot_general(m_mid, qb, nt, preferred_element_type=F32)
         + lax.dot_general(m_lo, qb, nt, preferred_element_type=F32) for qb in qbs], axis=1)
    blk_all = lax.broadcasted_iota(jnp.int32, (nbp, G * B), 0)
    sel_all = _top_select(jnp.where(blk_all < i, gate, -jnp.inf), blk_all, axis=0)

    for j in range(G):
        sel = sel_all[:, j * B:(j + 1) * B]
        b31 = prev_ref[j, 0:1, 0:1]
        mb = jnp.where(sel, jnp.where(blk == i - 1, 0.0, b31), NEG)
        mb = jnp.where(blk == i, 0.0, mb)
        mb = jnp.concatenate([mb, no_block], axis=0).T
        qa_ref[j * B:(j + 1) * B, :] = jnp.concatenate(
            [(qbs[j].astype(F32) * (scale * LOG2E)).astype(BF16), (mb * LOG2E).astype(BF16)], axis=1)

    def block_rows(n):
        return pl.ds(pl.multiple_of(n * B, B), B)

    def qk(n):
        return lax.dot_general(qa_ref[...], ka_ref[0, block_rows(n), :], (((1,), (1,)), ((), ())),
                               preferred_element_type=F32)

    def softmax_pv(s, n, first):
        vn = va_ref[0, block_rows(n), :]
        mx = jnp.max(s, axis=1, keepdims=True)
        if first:
            m_new = jnp.broadcast_to(mx, (R, LANES))
            p = jnp.exp2(s - mx)
            acc_ref[...] = jnp.dot(p.astype(BF16), vn, preferred_element_type=F32)
        else:
            m_prev = m_ref[...]
            m_new = jnp.maximum(m_prev, mx)
            alpha = jnp.exp2(m_prev - m_new)
            p = jnp.exp2(s - jnp.concatenate([m_new, m_new], axis=1))
            acc_ref[...] = (jnp.concatenate([alpha, alpha], axis=1) * acc_ref[...]
                            + jnp.dot(p.astype(BF16), vn, preferred_element_type=F32))
        m_ref[...] = m_new

    def own_logits():
        return qk(i) + own_ref[...].reshape(R, B) * LOG2E

    def prev_logits():
        return qk(i - 1) + prev_ref[...].reshape(R, B) * LOG2E

    @pl.when(i == 0)
    def _():
        softmax_pv(own_logits(), i, True)

    @pl.when(i == 1)
    def _():
        softmax_pv(own_logits(), i, True)
        softmax_pv(prev_logits(), i - 1, False)

    @pl.when(i >= 2)
    def _():
        s_own = own_logits()
        s_prev = prev_logits()
        softmax_pv(s_own, i, True)
        s_ref[...] = qk(0)
        softmax_pv(s_prev, i - 1, False)
        last = i - 2

        def run(n0, count):
            s_cur = s_ref[...]
            for u in range(count):
                s_next = qk(jnp.minimum(n0 + u + 1, last))
                softmax_pv(s_cur, n0 + u, False)
                s_cur = s_next
            s_ref[...] = s_cur

        def unrolled(t, carry):
            run(MB_UNROLL * t, MB_UNROLL)
            return carry

        total = i - 1
        n_main = total // MB_UNROLL
        lax.fori_loop(0, n_main, unrolled, 0)

        base = n_main * MB_UNROLL
        piece = MB_UNROLL // 2
        while piece >= 1:
            take = (total & piece) != 0

            @pl.when(take)
            def _(base=base, piece=piece):
                run(base, piece)

            base = base + jnp.where(take, piece, 0)
            piece //= 2

    acc = acc_ref[...]
    o = acc[:, :MB_HD] / acc[:, MB_HD:]
    for j in range(G):
        o_ref[:, j * MB_HD:(j + 1) * MB_HD] = o[j * B:(j + 1) * B].astype(o_ref.dtype)


def _moba_prompt(z, L, qcol, ka, vb, means, bias_own, bias_prev):
    nb = L // MB_BLOCK
    G, B = MB_GROUP, MB_BLOCK
    assert nb <= LANES, "block ids ride a 128-lane one-hot"
    assert z.dtype == BF16, "the gate matmul takes q exactly as stored"
    nbp = -(-nb // SUBLANES) * SUBLANES
    means_p = jnp.pad(means, ((0, 0), (0, nbp - nb), (0, 0)))
    return pl.pallas_call(
        _moba_prompt_kernel,
        grid=(MB_KV_HEADS, nb),
        in_specs=[pl.BlockSpec((B, G * MB_HD), lambda g, i: (i, qcol + g)),
                  pl.BlockSpec((1, L, 2 * MB_HD), lambda g, i: (g, 0, 0), pipeline_mode=pl.Buffered(1)),
                  pl.BlockSpec((1, L, 2 * MB_HD), lambda g, i: (g, 0, 0), pipeline_mode=pl.Buffered(1)),
                  pl.BlockSpec((1, nbp, MB_HD), lambda g, i: (g, 0, 0)),
                  pl.BlockSpec((G, B, B), lambda g, i: (g, 0, 0)),
                  pl.BlockSpec((G, B, B), lambda g, i: (g, 0, 0))],
        out_specs=pl.BlockSpec((B, G * MB_HD), lambda g, i: (i, g)),
        out_shape=jax.ShapeDtypeStruct((L, MB_HEADS * MB_HD), z.dtype),
        scratch_shapes=[pltpu.VMEM((G * B, 2 * MB_HD), BF16),
                        pltpu.VMEM((G * B, LANES), F32),
                        pltpu.VMEM((G * B, 2 * MB_HD), F32),
                        pltpu.VMEM((G * B, B), F32)],
        compiler_params=_params("arbitrary", "arbitrary"),
        name="moba_prompt",
    )(z, ka, vb, means_p, bias_own, bias_prev)


def _moba_sample_kernel(pt_ref, q_ref, kn_ref, vn_ref, ck_hbm, cv_hbm, own_ref, adj_ref, b31_ref, o_ref,
                        kbuf, vbuf, lg_ref, sem, *, T, n_pages, n_seq):
    b = pl.program_id(0)
    B = MB_BLOCK
    past = n_pages * PAGE_SIZE
    nb = past // B
    R = MB_HEADS * T
    half = R // MB_KV_HEADS
    scale = MB_HD ** -0.5

    KVH = MB_KV_HEADS
    prow = PAGE_SIZE * KVH

    def page_copies(seq, slot, j):
        dst = pl.ds(pl.multiple_of(j * prow, prow), prow)
        page = pt_ref[seq, j]
        return (pltpu.make_async_copy(ck_hbm.at[page], kbuf.at[slot, dst, :], sem.at[0, slot]),
                pltpu.make_async_copy(cv_hbm.at[page], vbuf.at[slot, dst, :], sem.at[1, slot]))

    def start_seq(seq, slot):
        def body(j, c):
            ck, cv = page_copies(seq, slot, j)
            ck.start()
            cv.start()
            return c
        lax.fori_loop(0, n_pages, body, 0)

    def wait_seq(seq, slot):
        def body(j, c):
            ck, cv = page_copies(seq, slot, j)
            ck.wait()
            cv.wait()
            return c
        lax.fori_loop(0, n_pages, body, 0)

    slot = b % 2

    @pl.when(b == 0)
    def _():
        start_seq(0, 0)

    @pl.when(b + 1 < n_seq)
    def _():
        start_seq(b + 1, 1 - slot)

    wait_seq(b, slot)

    def block(buf, n, g):
        return buf[slot, pl.ds(n * B * KVH + g, B, stride=KVH), :]

    q = q_ref[...]
    Q = jnp.concatenate([q[:, h * MB_HD:(h + 1) * MB_HD] for h in range(MB_HEADS)], axis=0)
    qf = Q * scale
    qs = qf.astype(BF16)
    grow = [slice(g * half, (g + 1) * half) for g in range(KVH)]

    mean_rows = [[] for _ in range(KVH)]
    for n in range(nb):
        for g in range(KVH):
            kf = block(kbuf, n, g)
            mean_rows[g].append(jnp.mean(kf, axis=0, keepdims=True))
            lg_ref[grow[g], n * B:(n + 1) * B] = lax.dot_general(
                qs[grow[g]], kf.astype(BF16), (((1,), (1,)), ((), ())), preferred_element_type=F32)

    gates = []
    for g in range(KVH):
        means = jnp.concatenate(mean_rows[g] + [jnp.zeros((LANES - nb, MB_HD), F32)], axis=0)
        gates.append(lax.dot_general(Q[grow[g]], means, (((1,), (1,)), ((), ())),
                                     precision=lax.Precision.HIGHEST, preferred_element_type=F32))
    gate = jnp.concatenate(gates, axis=0)
    lane = lax.broadcasted_iota(jnp.int32, (R, LANES), 1)
    sel = _top_select(jnp.where(lane < nb, gate, -jnp.inf), lane)
    mb = jnp.where(sel, jnp.where(lane == nb - 1, 0.0, b31_ref[...]), NEG)
    adj = adj_ref[...]

    def logits(n):
        s = lg_ref[:, n * B:(n + 1) * B] + mb[:, n:n + 1]
        return s + adj if n == nb - 1 else s

    kn = kn_ref[...]
    vn = vn_ref[...]
    s_own = jnp.concatenate(
        [lax.dot_general(qf[grow[g]], kn[:, g * MB_HD:(g + 1) * MB_HD], (((1,), (1,)), ((), ())),
                         preferred_element_type=F32) for g in range(KVH)], axis=0) + own_ref[...]

    m = jnp.max(s_own, axis=1, keepdims=True)
    for n in range(nb):
        m = jnp.maximum(m, jnp.max(logits(n), axis=1, keepdims=True))

    p_own = jnp.exp(s_own - m)
    l = jnp.sum(p_own, axis=1, keepdims=True)
    acc = [jnp.dot(p_own[grow[g]], vn[:, g * MB_HD:(g + 1) * MB_HD], preferred_element_type=F32)
           for g in range(KVH)]
    for n in range(nb):
        p = jnp.exp(logits(n) - m)
        l = l + jnp.sum(p, axis=1, keepdims=True)
        pb = p.astype(BF16)
        for g in range(KVH):
            acc[g] = acc[g] + jnp.dot(pb[grow[g]], block(vbuf, n, g).astype(BF16),
                                      preferred_element_type=F32)
    rows = jnp.concatenate(acc, axis=0) / l
    o_ref[...] = jnp.concatenate([rows[h * T:(h + 1) * T] for h in range(MB_HEADS)], axis=1)


def _moba_sample(z, zf, page_table, cache_k, cache_v, bias_own, bias_prev, DB, T, qcol, kcol, vcol):
    n_pages = page_table.shape[1]
    past = n_pages * PAGE_SIZE
    R = MB_HEADS * T
    W = MB_HEADS * MB_HD
    KW = MB_KV_HEADS * MB_HD
    ck = cache_k.reshape(cache_k.shape[0], PAGE_SIZE * MB_KV_HEADS, MB_HD)
    cv = cache_v.reshape(cache_v.shape[0], PAGE_SIZE * MB_KV_HEADS, MB_HD)
    own = bias_own[:, :T, :T].reshape(R, T)
    adj = bias_prev[:, :T, :].reshape(R, MB_BLOCK)
    b31 = bias_prev[:, :T, 0:1].reshape(R, 1)
    const = lambda b, pt: (0, 0)
    grid_spec = pltpu.PrefetchScalarGridSpec(
        num_scalar_prefetch=1,
        grid=(DB,),
        in_specs=[pl.BlockSpec((T, W), lambda b, pt: (b, qcol)),
                  pl.BlockSpec((T, KW), lambda b, pt: (b, kcol)),
                  pl.BlockSpec((T, KW), lambda b, pt: (b, vcol)),
                  pl.BlockSpec(memory_space=pl.ANY),
                  pl.BlockSpec(memory_space=pl.ANY),
                  pl.BlockSpec((R, T), const),
                  pl.BlockSpec((R, MB_BLOCK), const),
                  pl.BlockSpec((R, 1), const)],
        out_specs=pl.BlockSpec((T, W), lambda b, pt: (b, 0)),
        scratch_shapes=[pltpu.VMEM((2, past * MB_KV_HEADS, MB_HD), F32),
                        pltpu.VMEM((2, past * MB_KV_HEADS, MB_HD), F32),
                        pltpu.VMEM((R, past), F32),
                        pltpu.SemaphoreType.DMA((2, 2))],
    )
    return pl.pallas_call(
        functools.partial(_moba_sample_kernel, T=T, n_pages=n_pages, n_seq=DB),
        grid_spec=grid_spec,
        out_shape=jax.ShapeDtypeStruct((DB * T, W), F32),
        compiler_params=_params("arbitrary"),
        name="moba_sample",
    )(page_table, z, zf, zf, ck, cv, own, adj, b31)


def _xblock_math(x, g, wq_ref, k_ref, v_ref, wo_ref):
    q = jnp.dot(_rms(x, g).astype(BF16), wq_ref[...], preferred_element_type=F32)
    hd = k_ref.shape[2]
    scale = hd ** -0.5
    outs = []
    for h in range(MX_HEADS):
        qh = (q[:, h * hd:(h + 1) * hd] * scale).astype(BF16)
        s = lax.dot_general(qh, k_ref[h], (((1,), (1,)), ((), ())), preferred_element_type=F32)
        p = jnp.exp(s - jnp.max(s, axis=1, keepdims=True))
        o = jnp.dot(p.astype(BF16), v_ref[h], preferred_element_type=F32)
        outs.append((o / jnp.sum(p, axis=1, keepdims=True)).astype(BF16))
    return x + jnp.dot(jnp.concatenate(outs, axis=1), wo_ref[...], preferred_element_type=F32)


def _merge_kernel(oa_ref, ob_ref, ga_ref, gb_ref, x_ref, wpa_ref, wpb_ref, wo_ref, *rest):
    o_ref = rest[-1]
    a = jnp.dot(oa_ref[...].astype(BF16), wpa_ref[...], preferred_element_type=F32)
    b = jnp.dot(ob_ref[...].astype(BF16), wpb_ref[...], preferred_element_type=F32)
    merged = _sigmoid(ga_ref[...].astype(F32)) * a + _sigmoid(gb_ref[...].astype(F32)) * b
    h = x_ref[...] + jnp.dot(merged.astype(BF16), wo_ref[...], preferred_element_type=F32)
    if len(rest) > 1:
        gx_ref, wq_ref, k_ref, v_ref, wxo_ref = rest[:-1]
        h = _xblock_math(h, gx_ref[...], wq_ref, k_ref, v_ref, wxo_ref)
    o_ref[...] = h


def _merge(oa, ob, z, gcol, x, wpa, wpb, wo, xblock=None, tm=512):
    M, D = x.shape
    tm = min(tm, M)
    row = lambda i: (i, 0)
    const = lambda i: (0, 0)
    once = pl.Buffered(1)
    wspec = pl.BlockSpec((D, D), const, pipeline_mode=once)
    args = [oa, ob, z, z, x, wpa, wpb, wo]
    specs = [pl.BlockSpec((tm, D), row), pl.BlockSpec((tm, D), row),
             pl.BlockSpec((tm, D), lambda i: (i, gcol)), pl.BlockSpec((tm, D), lambda i: (i, gcol + 1)),
             pl.BlockSpec((tm, D), row), wspec, wspec, wspec]
    if xblock is not None:
        gx, wq, mk, mv, wxo = xblock
        args += [gx.reshape(1, D), wq, mk, mv, wxo]
        specs += [pl.BlockSpec((1, D), const), wspec,
                  pl.BlockSpec(mk.shape, lambda i: (0, 0, 0)), pl.BlockSpec(mv.shape, lambda i: (0, 0, 0)), wspec]
    return pl.pallas_call(
        _merge_kernel,
        grid=(M // tm,),
        in_specs=specs,
        out_specs=pl.BlockSpec((tm, D), row),
        out_shape=jax.ShapeDtypeStruct((M, D), F32),
        compiler_params=_params("parallel"),
        name="merge",
    )(*args)


def _xattn_kernel(q_ref, mk_hbm, mv_hbm, o_ref, kbuf, vbuf, sem, *, n_steps, per_step, L):
    b = pl.program_id(0)
    hd = q_ref.shape[1] // MX_HEADS
    scale = hd ** -0.5
    slot = b % 2

    def head_copies(step, slot):
        cps = []
        for j in range(per_step):
            seq = step * per_step + j
            for h in range(MX_HEADS):
                cps.append(pltpu.make_async_copy(mk_hbm.at[seq, :, h, :], kbuf.at[slot, j, h], sem.at[0, slot]))
                cps.append(pltpu.make_async_copy(mv_hbm.at[seq, :, h, :], vbuf.at[slot, j, h], sem.at[1, slot]))
        return cps

    @pl.when(b == 0)
    def _():
        for cp in head_copies(0, 0):
            cp.start()

    @pl.when(b + 1 < n_steps)
    def _():
        for cp in head_copies(b + 1, 1 - slot):
            cp.start()

    for cp in head_copies(b, slot):
        cp.wait()

    q = (q_ref[...] * scale).astype(BF16)
    pairs = [(j, h) for j in range(per_step) for h in range(MX_HEADS)]
    s = [lax.dot_general(q[j * L:(j + 1) * L, h * hd:(h + 1) * hd], kbuf[slot, j, h].astype(BF16),
                         (((1,), (1,)), ((), ())), preferred_element_type=F32) for j, h in pairs]
    p = [jnp.exp(x - jnp.max(x, axis=1, keepdims=True)) for x in s]
    o = [jnp.dot(p[c].astype(BF16), vbuf[slot, j, h].astype(BF16), preferred_element_type=F32)
         / jnp.sum(p[c], axis=1, keepdims=True) for c, (j, h) in enumerate(pairs)]
    o_ref[...] = jnp.concatenate(
        [jnp.concatenate(o[j * MX_HEADS:(j + 1) * MX_HEADS], axis=1) for j in range(per_step)], axis=0)


def _xattn(q, mk, mv, B, L, per_step=4):
    D = q.shape[1]
    mem, hd = mk.shape[1], mk.shape[3]
    per_step = math.gcd(per_step, B)
    n_steps = B // per_step
    rows = per_step * L
    return pl.pallas_call(
        functools.partial(_xattn_kernel, n_steps=n_steps, per_step=per_step, L=L),
        grid=(n_steps,),
        in_specs=[pl.BlockSpec((rows, D), lambda b: (b, 0)),
                  pl.BlockSpec(memory_space=pl.ANY),
                  pl.BlockSpec(memory_space=pl.ANY)],
        out_specs=pl.BlockSpec((rows, D), lambda b: (b, 0)),
        out_shape=jax.ShapeDtypeStruct((B * L, D), F32),
        scratch_shapes=[pltpu.VMEM((2, per_step, MX_HEADS, mem, hd), F32),
                        pltpu.VMEM((2, per_step, MX_HEADS, mem, hd), F32),
                        pltpu.SemaphoreType.DMA((2, 2))],
        compiler_params=_params("arbitrary"),
        name="xattn",
    )(q, mk, mv)


def _ffn_kernel(x_ref, g_ref, wg_ref, wu_ref, wd_ref, fg_ref, o_ref, *, tf):
    x = x_ref[...]
    xn = _rms(x, g_ref[...]).astype(BF16)
    acc = x
    for c in range(wg_ref.shape[1] // tf):
        sl = slice(c * tf, (c + 1) * tf)
        a = jnp.dot(xn, wg_ref[:, sl], preferred_element_type=F32)
        u = jnp.dot(xn, wu_ref[:, sl], preferred_element_type=F32)
        acc = acc + jnp.dot((_silu(a) * u).astype(BF16), wd_ref[sl, :], preferred_element_type=F32)
    o_ref[...] = _rms(acc, fg_ref[...])


def _ffn(x, g, wg, wu, wd, fg, tm=1024, tf=256):
    M, D = x.shape
    FF = wg.shape[1]
    tm = min(tm, M)
    row = lambda i: (i, 0)
    const = lambda i: (0, 0)
    once = pl.Buffered(1)
    return pl.pallas_call(
        functools.partial(_ffn_kernel, tf=tf),
        grid=(M // tm,),
        in_specs=[pl.BlockSpec((tm, D), row), pl.BlockSpec((1, D), const),
                  pl.BlockSpec((D, FF), const, pipeline_mode=once),
                  pl.BlockSpec((D, FF), const, pipeline_mode=once),
                  pl.BlockSpec((FF, D), const, pipeline_mode=once),
                  pl.BlockSpec((1, D), const)],
        out_specs=pl.BlockSpec((tm, D), row),
        out_shape=jax.ShapeDtypeStruct((M, D), F32),
        compiler_params=_params("parallel"),
        name="ffn",
    )(x, g.reshape(1, D), wg, wu, wd, fg.reshape(1, D))


def _mixer(x2, B, L, chunk, s0, moba_fn, W, za_dtype, xblock=None):
    tm = 512 if za_dtype == BF16 else 256
    za, zf = _proj_in(x2, W["norm_mix_g"], W["w_in"], W["a_cols"], za_dtype, tm=tm)
    oa, S = _gla(za, zf, W["lb_logits"], W["hg_norm_g"], s0, B, L, chunk)
    ob = moba_fn(za, zf)
    h = _merge(oa, ob, za, _ZA_GA, x2, W["w_pa"], W["w_pb"], W["w_out"], xblock=xblock)
    return h, S, zf


def kernel(x_prompt, x_sample, cache_k, cache_v, cache_mem_k, cache_mem_v, state_hgrn, page_table, mem_prompt, norm_mix_g, w_in, hg_lb_logits, hg_norm_g, w_proj_a, w_proj_b, w_out, rel_bias, norm_x_g, w_xq, w_mk, w_mv, w_xo, norm_ffn_g, w_gate, w_up, w_down, final_norm_g):
    Bp, Lp, D = x_prompt.shape
    DB, T, _ = x_sample.shape
    l = 0
    win = w_in[l]
    hw = HG_HEADS * HG_DK
    hq, hf, hi, hg = (win[:, j * hw:(j + 1) * hw] for j in range(4))
    c0 = 4 * hw + MB_HEADS * MB_HD
    c1 = c0 + 2 * MB_KV_HEADS * MB_HD
    W = {
        "w_in": jnp.concatenate([hq, hi, hg, win[:, 4 * hw:c0], win[:, c1:], hf, win[:, c0:c1]],
                                axis=1).astype(BF16),
        "a_cols": win.shape[1] - hw - (c1 - c0),
        "norm_mix_g": norm_mix_g[l],
        "lb_logits": hg_lb_logits,
        "hg_norm_g": hg_norm_g[l],
        "w_pa": w_proj_a[l].astype(BF16), "w_pb": w_proj_b[l].astype(BF16), "w_out": w_out[l].astype(BF16),
    }
    wxq, wxo = w_xq[l].astype(BF16), w_xo[l].astype(BF16)
    wmkv = jnp.concatenate([w_mk[l], w_mv[l]], axis=1).astype(BF16)
    wg, wu, wd = w_gate[l].astype(BF16), w_up[l].astype(BF16), w_down[l].astype(BF16)

    bias_own, bias_prev = _bias_tiles(rel_bias)

    xp = x_prompt.reshape(Bp * Lp, D)

    assert Bp == 1, "the prompt kernels take one sequence"
    kw = MB_KV_HEADS * MB_HD

    def new_kv(zf, B, L):
        return (zf[:, _ZF_K * kw:(_ZF_K + 1) * kw].reshape(B, L, MB_KV_HEADS, MB_HD),
                zf[:, _ZF_V * kw:(_ZF_V + 1) * kw].reshape(B, L, MB_KV_HEADS, MB_HD))

    prompt_kv = []

    def moba_p(za, zf):
        ka, va, means, ko, vo = _moba_prep(zf, Lp, _ZF_K, _ZF_V)
        prompt_kv.extend([ko, vo])
        return _moba_prompt(za, Lp, 2 * _ZA_MQ, ka, va, means, bias_own, bias_prev)

    mem = mem_prompt.reshape(Bp * mem_prompt.shape[1], D)
    mkv = _matmul(mem, wmkv, tm=256, tn=512)
    mshape = (Bp, mem_prompt.shape[1], MX_HEADS, D // MX_HEADS)
    mkp, mvp = mkv[:, :D].reshape(mshape), mkv[:, D:].reshape(mshape)
    xblock = (norm_x_g[l], wxq, mkp[0].swapaxes(0, 1).astype(BF16), mvp[0].swapaxes(0, 1).astype(BF16), wxo)

    s0 = jnp.zeros((Bp, HG_HEADS, HG_DK, HG_DV), F32)
    hp, Sp, zfp = _mixer(xp, Bp, Lp, math.gcd(Lp, 64), s0, moba_p, W, BF16, xblock=xblock)
    kp, vp = (a.reshape(Bp, Lp, MB_KV_HEADS, MB_HD) for a in prompt_kv)
    yp = _ffn(hp, norm_ffn_g[l], wg, wu, wd, final_norm_g)

    xs = x_sample.reshape(DB * T, D)

    def moba_s(za, zf):
        return _moba_sample(za, zf, page_table, cache_k[l], cache_v[l], bias_own, bias_prev, DB, T,
                            _ZA_MQ, _ZF_K, _ZF_V)

    hs, Ss, zfs = _mixer(xs, DB, T, T, state_hgrn[l], moba_s, W, F32)
    ksn, vsn = new_kv(zfs, DB, T)
    qxs = _matmul(hs, wxq, g=norm_x_g[l], tn=512)
    oxs = _xattn(qxs, cache_mem_k[l], cache_mem_v[l], DB, T)
    hs = _matmul(oxs, wxo, res=hs, tn=512)
    ys = _ffn(hs, norm_ffn_g[l], wg, wu, wd, final_norm_g)

    return (yp.reshape(Bp, Lp, D), ys.reshape(DB, T, D),
            Sp[None].astype(state_hgrn.dtype), kp[None], vp[None],
            mkp[None], mvp[None],
            Ss[None].astype(state_hgrn.dtype), ksn[None], vsn[None])
```
